```python
import jax, jax.numpy as jnp
from jax import lax
import numpy as np

D_MODEL = 1024
BATCH = 8
SEQ = 8192
DEPTH = 1
DEC_BATCH = 8
DEC_SEQ = 64
PAST_LEN = 2048

CHUNK = 64
D_PLE = 256
POOL_WINDOWS = (2, 4, 8, 16)
N_POOL_GROUPS = 4
POOL_WIDTH = D_MODEL // 2
POOL_GROUP = POOL_WIDTH // N_POOL_GROUPS
POOL_STATE = max(POOL_WINDOWS) - 1
N_HEADS = 8
HEAD_DIM = 64
ATT_WIDTH = N_HEADS * HEAD_DIM
N_PREV_CHUNKS = 8
REL_CLIP = 256
N_REL = REL_CLIP + CHUNK
N_GROUPS = 4
EXPERTS_PER_GROUP = 8
N_EXPERTS = N_GROUPS * EXPERTS_PER_GROUP
TOP_K = 2
D_EXPERT = 512
MOE_BLOCK = 256
EPS = 1e-6
D_IN = POOL_WIDTH + 3 * ATT_WIDTH + 2 * D_MODEL
ATT_CACHE = min(N_PREV_CHUNKS * CHUNK, PAST_LEN)

kernel_name = "pool_chunkattn_hmoe_streaming_step"


def rmsnorm(x, g):
    xf = x.astype(jnp.float32)
    y = xf * lax.rsqrt(jnp.mean(xf * xf, axis=-1, keepdims=True) + EPS) * g.astype(jnp.float32)
    return y.astype(x.dtype)


def tail_rows(t, n):
    L = t.shape[1]
    if L < n:
        t = jnp.pad(t, ((0, 0), (n - L, 0)) + ((0, 0),) * (t.ndim - 2))
    return t[:, t.shape[1] - n:]


def pool_mix(u, hist, pos0, w_pool, pool_scale):
    B, T, _ = u.shape
    cat = jnp.concatenate([hist.astype(u.dtype), u], axis=1)
    cs = jnp.cumsum(cat.astype(jnp.float32), axis=1)
    cs = jnp.concatenate([jnp.zeros((B, 1, POOL_WIDTH), jnp.float32), cs], axis=1)
    pos = pos0 + jnp.arange(T)
    means = []
    for g, w in enumerate(POOL_WINDOWS):
        sl = slice(g * POOL_GROUP, (g + 1) * POOL_GROUP)
        hi = cs[:, POOL_STATE + 1:POOL_STATE + 1 + T, sl]
        lo = cs[:, POOL_STATE + 1 - w:POOL_STATE + 1 - w + T, sl]
        cnt = jnp.minimum(pos + 1, w).astype(jnp.float32)
        means.append((hi - lo) / cnt[None, :, None])
    d = (jnp.concatenate(means, axis=-1) - u.astype(jnp.float32)).reshape(B, T, N_POOL_GROUPS, POOL_GROUP)
    y = jnp.einsum('btgc,gcd->btgd', d, w_pool.astype(jnp.float32)).reshape(B, T, POOL_WIDTH)
    y = y * pool_scale.astype(jnp.float32)
    return y.astype(u.dtype), cat[:, cat.shape[1] - POOL_STATE:]


def rel_bias(table, q_pos, k_pos):
    idx = jnp.clip(q_pos[:, None] - k_pos[None, :], -(CHUNK - 1), REL_CLIP) + (CHUNK - 1)
    return table.astype(jnp.float32)[:, idx]


def band_attend(q, k, v, bias, mask):
    s = jnp.einsum('bqhd,bkhd->bhqk', q.astype(jnp.float32), k.astype(jnp.float32)) * (HEAD_DIM ** -0.5) + bias
    if mask is not None:
        s = jnp.where(mask, s, -1e30)
    p = jax.nn.softmax(s, axis=-1)
    return jnp.einsum('bhqk,bkhd->bqhd', p, v.astype(jnp.float32)).astype(q.dtype)


def prompt_attention(q, k, v, table):
    B, T, H, Dh = q.shape
    nc = T // CHUNK
    pad = N_PREV_CHUNKS * CHUNK
    band = pad + CHUNK
    kp = jnp.pad(k, ((0, 0), (pad, 0), (0, 0), (0, 0)))
    vp = jnp.pad(v, ((0, 0), (pad, 0), (0, 0), (0, 0)))
    k_off = jnp.arange(band) - pad
    bias = rel_bias(table, jnp.arange(CHUNK), k_off)

    def one_chunk(c):
        s0 = c * CHUNK
        qc = lax.dynamic_slice_in_dim(q, s0, CHUNK, axis=1)
        kc = lax.dynamic_slice_in_dim(kp, s0, band, axis=1)
        vc = lax.dynamic_slice_in_dim(vp, s0, band, axis=1)
        mask = jnp.broadcast_to((s0 + k_off >= 0)[None, :], (CHUNK, band))
        return band_attend(qc, kc, vc, bias, mask)

    out = lax.map(one_chunk, jnp.arange(nc))
    return jnp.moveaxis(out, 0, 1).reshape(B, T, H * Dh)


def sample_attention(q, k, v, ck, cv, table):
    B, T = q.shape[0], q.shape[1]
    Lc = ck.shape[1]
    k_all = jnp.concatenate([ck.astype(k.dtype), k], axis=1)
    v_all = jnp.concatenate([cv.astype(v.dtype), v], axis=1)
    q_pos = PAST_LEN + jnp.arange(T)
    k_pos = jnp.concatenate([PAST_LEN - Lc + jnp.arange(Lc), q_pos])
    out = band_attend(q, k_all, v_all, rel_bias(table, q_pos, k_pos), None)
    return out.reshape(B, T, ATT_WIDTH)


def route(h, w_coarse, b_coarse, w_fine, b_fine):
    T = h.shape[0]
    cl = (h @ w_coarse).astype(jnp.float32) + b_coarse.astype(jnp.float32)
    cp = jax.nn.softmax(cl, axis=-1)
    grp = jnp.argmax(cl, axis=-1)
    gp = jnp.take_along_axis(cp, grp[:, None], axis=1)[:, 0]
    fl = ((h @ w_fine).astype(jnp.float32) + b_fine.astype(jnp.float32)).reshape(T, N_GROUPS, EXPERTS_PER_GROUP)
    fl = jnp.take_along_axis(fl, grp[:, None, None], axis=1)[:, 0]
    fp = jax.nn.softmax(fl, axis=-1)
    top_p, top_i = lax.top_k(fp, TOP_K)
    w = gp[:, None] * top_p / jnp.sum(top_p, axis=-1, keepdims=True)
    return grp[:, None] * EXPERTS_PER_GROUP + top_i, w


def moe_ffn(h, w_coarse, b_coarse, w_fine, b_fine, w_gate_e, w_up_e, w_down_e):
    lead = h.shape[:-1]
    x = h.reshape(-1, D_MODEL)
    T = x.shape[0]
    e, w = route(x, w_coarse, b_coarse, w_fine, b_fine)
    A = T * TOP_K
    e_flat = e.reshape(A)
    w_flat = w.reshape(A)
    tok = jnp.repeat(jnp.arange(T), TOP_K)
    order = jnp.argsort(e_flat)
    e_s, tok_s, w_s = e_flat[order], tok[order], w_flat[order]
    counts = jnp.bincount(e_flat, length=N_EXPERTS)
    padded = (counts + MOE_BLOCK - 1) // MOE_BLOCK * MOE_BLOCK
    start = jnp.cumsum(counts) - counts
    pend = jnp.cumsum(padded)
    pstart = pend - padded
    dest = pstart[e_s] + jnp.arange(A) - start[e_s]
    n_blocks = -(-(A + N_EXPERTS * (MOE_BLOCK - 1)) // MOE_BLOCK)
    P = n_blocks * MOE_BLOCK
    xd = jnp.zeros((P, D_MODEL), x.dtype).at[dest].set(x[tok_s])
    block_e = jnp.minimum(jnp.searchsorted(pend, jnp.arange(n_blocks) * MOE_BLOCK, side='right'), N_EXPERTS - 1)

    def expert_block(args):
        xb, eb = args
        g = xb @ w_gate_e[eb]
        u = xb @ w_up_e[eb]
        return (jax.nn.silu(g) * u) @ w_down_e[eb]

    yd = lax.map(expert_block, (xd.reshape(n_blocks, MOE_BLOCK, D_MODEL), block_e)).reshape(P, D_MODEL)
    y = jnp.zeros((T, D_MODEL), jnp.float32).at[tok_s].add(w_s[:, None] * yd[dest].astype(jnp.float32))
    return y.astype(h.dtype).reshape(lead + (D_MODEL,))


def trunk_layer(x, p, pool_hist, pos0, attend, g_mix, w_in, w_pool, pool_scale, g_q, g_k,
                w_br_pool, w_br_att, w_out, g_ffn, w_coarse, b_coarse, w_fine, b_fine,
                w_gate_e, w_up_e, w_down_e, g_ple, w_ple_gate, w_ple_proj):
    B, T, _ = x.shape
    h = rmsnorm(x, g_mix)
    z = h @ w_in
    cuts = [POOL_WIDTH, POOL_WIDTH + ATT_WIDTH, POOL_WIDTH + 2 * ATT_WIDTH,
            POOL_WIDTH + 3 * ATT_WIDTH, POOL_WIDTH + 3 * ATT_WIDTH + D_MODEL]
    u, q, k, v, ga, gb = jnp.split(z, cuts, axis=-1)
    a, pool_new = pool_mix(u, pool_hist, pos0, w_pool, pool_scale)
    q = rmsnorm(q.reshape(B, T, N_HEADS, HEAD_DIM), g_q)
    k = rmsnorm(k.reshape(B, T, N_HEADS, HEAD_DIM), g_k)
    v = v.reshape(B, T, N_HEADS, HEAD_DIM)
    o, k_state, v_state = attend(q, k, v)
    m = jax.nn.sigmoid(ga) * (a @ w_br_pool) + jax.nn.sigmoid(gb) * (o @ w_br_att)
    x = x + m @ w_out
    x = x + moe_ffn(rmsnorm(x, g_ffn), w_coarse, b_coarse, w_fine, b_fine, w_gate_e, w_up_e, w_down_e)
    gate = jax.nn.sigmoid(rmsnorm(x, g_ple) @ w_ple_gate)
    x = x + (p @ w_ple_proj) * gate
    return x, k_state, v_state, pool_new


def setup_inputs(seed: int = 0) -> dict:
    key = jax.random.key(seed)
    ks = jax.random.split(key, 28)

    def nrm(k, shape, scale=1.0):
        return scale * jax.random.normal(k, shape, jnp.float32)

    return {
        "x_prompt": nrm(ks[0], (BATCH, SEQ, D_MODEL)),
        "x_sample": nrm(ks[1], (DEC_BATCH, DEC_SEQ, D_MODEL)),
        "cache_k": nrm(ks[2], (DEPTH, DEC_BATCH, ATT_CACHE, N_HEADS, HEAD_DIM)),
        "cache_v": nrm(ks[3], (DEPTH, DEC_BATCH, ATT_CACHE, N_HEADS, HEAD_DIM)),
        "state_pool": nrm(ks[4], (DEPTH, DEC_BATCH, POOL_STATE, POOL_WIDTH)),
        "p_prompt": nrm(ks[5], (DEPTH, BATCH, SEQ, D_PLE)),
        "p_sample": nrm(ks[6], (DEPTH, DEC_BATCH, DEC_SEQ, D_PLE)),
        "g_mix": 1.0 + nrm(ks[7], (DEPTH, D_MODEL), 0.05),
        "w_in": nrm(ks[8], (DEPTH, D_MODEL, D_IN), D_MODEL ** -0.5),
        "w_pool": nrm(ks[9], (DEPTH, N_POOL_GROUPS, POOL_GROUP, POOL_GROUP), POOL_GROUP ** -0.5),
        "pool_scale": 1.0 + nrm(ks[10], (DEPTH, POOL_WIDTH), 0.1),
        "g_q": 1.0 + nrm(ks[11], (DEPTH, HEAD_DIM), 0.05),
        "g_k": 1.0 + nrm(ks[12], (DEPTH, HEAD_DIM), 0.05),
        "rel_table": nrm(ks[13], (DEPTH, N_HEADS, N_REL), 0.1),
        "w_br_pool": nrm(ks[14], (DEPTH, POOL_WIDTH, D_MODEL), POOL_WIDTH ** -0.5),
        "w_br_att": nrm(ks[15], (DEPTH, ATT_WIDTH, D_MODEL), ATT_WIDTH ** -0.5),
        "w_out": nrm(ks[16], (DEPTH, D_MODEL, D_MODEL), D_MODEL ** -0.5),
        "g_ffn": 1.0 + nrm(ks[17], (DEPTH, D_MODEL), 0.05),
        "w_coarse": nrm(ks[18], (DEPTH, D_MODEL, N_GROUPS), D_MODEL ** -0.5),
        "b_coarse": nrm(ks[19], (DEPTH, N_GROUPS), 0.01),
        "w_fine": nrm(ks[20], (DEPTH, D_MODEL, N_EXPERTS), D_MODEL ** -0.5),
        "b_fine": nrm(ks[21], (DEPTH, N_EXPERTS), 0.01),
        "w_gate_e": nrm(ks[22], (DEPTH, N_EXPERTS, D_MODEL, D_EXPERT), D_MODEL ** -0.5),
        "w_up_e": nrm(ks[23], (DEPTH, N_EXPERTS, D_MODEL, D_EXPERT), D_MODEL ** -0.5),
        "w_down_e": nrm(ks[24], (DEPTH, N_EXPERTS, D_EXPERT, D_MODEL), D_EXPERT ** -0.5),
        "g_ple": 1.0 + nrm(ks[25], (DEPTH, D_MODEL), 0.05),
        "w_ple_gate": nrm(ks[26], (DEPTH, D_MODEL, D_MODEL), D_MODEL ** -0.5),
        "w_ple_proj": nrm(ks[27], (DEPTH, D_PLE, D_MODEL), D_PLE ** -0.5),
    }


def reference(x_prompt, x_sample, cache_k, cache_v, state_pool, p_prompt, p_sample,
              g_mix, w_in, w_pool, pool_scale, g_q, g_k, rel_table, w_br_pool, w_br_att, w_out,
              g_ffn, w_coarse, b_coarse, w_fine, b_fine, w_gate_e, w_up_e, w_down_e,
              g_ple, w_ple_gate, w_ple_proj):
    xp, xs = x_prompt, x_sample
    kp_l, vp_l, pp_l, ks_l, vs_l, ps_l = [], [], [], [], [], []
    for l in range(DEPTH):
        lw = dict(g_mix=g_mix[l], w_in=w_in[l], w_pool=w_pool[l], pool_scale=pool_scale[l],
                  g_q=g_q[l], g_k=g_k[l], w_br_pool=w_br_pool[l], w_br_att=w_br_att[l], w_out=w_out[l],
                  g_ffn=g_ffn[l], w_coarse=w_coarse[l], b_coarse=b_coarse[l], w_fine=w_fine[l],
                  b_fine=b_fine[l], w_gate_e=w_gate_e[l], w_up_e=w_up_e[l], w_down_e=w_down_e[l],
                  g_ple=g_ple[l], w_ple_gate=w_ple_gate[l], w_ple_proj=w_ple_proj[l])
        table = rel_table[l]
        ck, cv = cache_k[l], cache_v[l]
        hist0 = jnp.zeros((xp.shape[0], POOL_STATE, POOL_WIDTH), xp.dtype)
        xp, kp, vp, pp = trunk_layer(
            xp, p_prompt[l], hist0, 0,
            lambda q, k, v: (prompt_attention(q, k, v, table), tail_rows(k, ATT_CACHE), tail_rows(v, ATT_CACHE)),
            **lw)
        xs, kn, vn, pn = trunk_layer(
            xs, p_sample[l], state_pool[l], PAST_LEN,
            lambda q, k, v: (sample_attention(q, k, v, ck, cv, table), k, v),
            **lw)
        kp_l.append(kp); vp_l.append(vp); pp_l.append(pp)
        ks_l.append(kn); vs_l.append(vn); ps_l.append(pn)
    return (xp, xs, jnp.stack(kp_l), jnp.stack(vp_l), jnp.stack(pp_l),
            jnp.stack(ks_l), jnp.stack(vs_l), jnp.stack(ps_l))
```

```python
import functools

import jax
import jax.numpy as jnp
import numpy as np
from jax import lax
from jax.experimental import pallas as pl
from jax.experimental.pallas import tpu as pltpu

CHUNK = 64
N_HEADS = 8
HEAD_DIM = 64
ATT_WIDTH = N_HEADS * HEAD_DIM
POOL_WINDOWS = (2, 4, 8, 16)
POOL_GROUP = 128
POOL_WIDTH = POOL_GROUP * len(POOL_WINDOWS)
POOL_STATE = max(POOL_WINDOWS) - 1
N_PREV_CHUNKS = 8
BAND_PREV = N_PREV_CHUNKS * CHUNK
REL_CLIP = 256
N_GROUPS = 4
EXPERTS_PER_GROUP = 8
N_EXPERTS = N_GROUPS * EXPERTS_PER_GROUP
PAST_LEN = 2048
EPS = 1e-6
MASKED = -1e30

LANES = 128
SUBLANES = 8
VMEM_LIMIT_BYTES = 56 * 1024 * 1024

POOL_HIST = 16
PAIRS_PER_GROUP = EXPERTS_PER_GROUP * (EXPERTS_PER_GROUP - 1) // 2
N_CLASSES = N_GROUPS * PAIRS_PER_GROUP
MOE_ROWS = 256
ROUTE_LANES = 128

_f32 = jnp.float32
_bf16 = jnp.bfloat16


def _dot(a, b):
    return jnp.dot(a, b, preferred_element_type=_f32)


def _rms_scale(x):
    return lax.rsqrt(jnp.mean(x * x, axis=-1, keepdims=True) + EPS)


def _split_dot(x, w_bf16):
    hi = x.astype(_bf16)
    lo = (x - hi.astype(_f32)).astype(_bf16)
    return _dot(hi, w_bf16) + _dot(lo, w_bf16)


def _load_row_tiles(ref, rows):
    return jnp.concatenate([ref[pl.ds(s, rows, stride=SUBLANES), :] for s in range(SUBLANES)], axis=-1)


def _store_row_tiles(ref, val, rows):
    for s in range(SUBLANES):
        ref[pl.ds(s, rows, stride=SUBLANES), :] = val[:, s * LANES:(s + 1) * LANES]


def _route_rows(logits):
    lane = lax.broadcasted_iota(jnp.int32, (1, ROUTE_LANES), 1)
    lane_f = lane.astype(_f32)
    neg = jnp.float32(-jnp.inf)
    far = jnp.float32(ROUTE_LANES)
    cmask = lane < N_GROUPS
    cl = jnp.where(cmask, logits, neg)
    cmax = jnp.max(cl, axis=-1, keepdims=True)
    grp = jnp.min(jnp.where(cl == cmax, lane_f, far), axis=-1, keepdims=True)
    csum = jnp.sum(jnp.where(cmask, jnp.exp(cl - cmax), 0.0), axis=-1, keepdims=True)
    gp = 1.0 / csum
    fine_grp = ((lane - N_GROUPS) >> 3).astype(_f32)
    fl = jnp.where(fine_grp == grp, logits, neg)
    m1 = jnp.max(fl, axis=-1, keepdims=True)
    i1 = jnp.min(jnp.where(fl == m1, lane_f, far), axis=-1, keepdims=True)
    fl2 = jnp.where(lane_f == i1, neg, fl)
    m2 = jnp.max(fl2, axis=-1, keepdims=True)
    i2 = jnp.min(jnp.where(fl2 == m2, lane_f, far), axis=-1, keepdims=True)
    e2 = jnp.exp(m2 - m1)
    denom = 1.0 + e2
    w1 = gp / denom
    w2 = gp * e2 / denom
    out = jnp.where(lane == 0, i1 - N_GROUPS,
                    jnp.where(lane == 1, i2 - N_GROUPS,
                              jnp.where(lane == 2, w1, jnp.where(lane == 3, w2, 0.0))))
    return out


def _mixer_kernel(x_ref, k0_ref, v0_ref, p0_ref, gmix_ref, win_ref, wpool_ref, pscale_ref, gq_ref, gk_ref,
                  bd_ref, bias_ref, wbp_ref, wba_ref, wout_ref, gffn_ref, wroute_ref, broute_ref,
                  *rest, tq, n_tiles, pos0):
    x1_ref, route_ref, kout_ref, vout_ref, pout_ref, kt_buf, v_buf, u_buf = rest[-8:]
    t = pl.program_id(1)
    band = BAND_PREV + tq

    @pl.when(t == 0)
    def _init_history():
        kt_buf[:, 0:BAND_PREV] = k0_ref[0].T.astype(_bf16)
        v_buf[0:BAND_PREV, :] = v0_ref[0].astype(_bf16)
        u_buf[0:POOL_HIST, :] = p0_ref[0]

    x = x_ref[0]
    h = (x * _rms_scale(x) * gmix_ref[...]).astype(_bf16)

    u = _dot(h, win_ref[:, 0:POOL_WIDTH])
    u_buf[POOL_HIST:POOL_HIST + tq, :] = u
    pout_ref[0] = u_buf[POOL_HIST + tq - POOL_STATE:POOL_HIST + tq, :]
    row = lax.broadcasted_iota(jnp.int32, (tq, 1), 0)
    pos1 = pos0 + t * tq + row + 1
    pooled = []
    for g, w in enumerate(POOL_WINDOWS):
        sl = slice(g * POOL_GROUP, (g + 1) * POOL_GROUP)
        acc = u[:, sl]
        for j in range(1, w):
            acc = acc + u_buf[POOL_HIST - j:POOL_HIST - j + tq, sl]
        cnt = jnp.minimum(pos1, w).astype(_f32)
        d = acc / cnt - u[:, sl]
        pooled.append(_dot(d.astype(_bf16), wpool_ref[g]))
    a = (jnp.concatenate(pooled, axis=-1) * pscale_ref[...]).astype(_bf16)

    q = _dot(h, win_ref[:, POOL_WIDTH:POOL_WIDTH + ATT_WIDTH])
    qn = (q * lax.rsqrt(_split_dot(q * q, bd_ref[...]) + EPS) * gq_ref[...]).astype(_bf16)
    k = _dot(h, win_ref[:, POOL_WIDTH + ATT_WIDTH:POOL_WIDTH + 2 * ATT_WIDTH])
    kn = k * lax.rsqrt(_split_dot(k * k, bd_ref[...]) + EPS) * gk_ref[...]
    v = _dot(h, win_ref[:, POOL_WIDTH + 2 * ATT_WIDTH:POOL_WIDTH + 3 * ATT_WIDTH])
    kout_ref[0] = kn
    vout_ref[0] = v
    kt_buf[:, BAND_PREV:band] = kn.T.astype(_bf16)
    v_buf[BAND_PREV:band, :] = v.astype(_bf16)

    lane = lax.broadcasted_iota(jnp.int32, (1, LANES), 1)
    even = lane < HEAD_DIM
    col = lax.broadcasted_iota(jnp.int32, (1, band), 1)
    started = col >= BAND_PREV - (pos0 + t * tq)
    heads = []
    for p in range(N_HEADS // 2):
        sl = slice(p * LANES, (p + 1) * LANES)
        qp = qn[:, sl]
        ktp = kt_buf[sl, :]
        vp = v_buf[:, sl]
        acc = None
        for half in range(2):
            keep = even if half == 0 else jnp.logical_not(even)
            qh = jnp.where(keep, qp, jnp.zeros_like(qp))
            vh = jnp.where(keep, vp, jnp.zeros_like(vp))
            s = _dot(qh, ktp) + bias_ref[2 * p + half]
            s = jnp.where(started, s, MASKED)
            e = jnp.exp(s - jnp.max(s, axis=-1, keepdims=True))
            pr = (e * (1.0 / jnp.sum(e, axis=-1, keepdims=True))).astype(_bf16)
            part = _dot(pr, vh)
            acc = part if acc is None else acc + part
        heads.append(acc)
    o = jnp.concatenate(heads, axis=-1).astype(_bf16)

    if n_tiles > 1:
        for c in range(BAND_PREV // tq):
            kt_buf[:, c * tq:(c + 1) * tq] = kt_buf[:, (c + 1) * tq:(c + 2) * tq]
            v_buf[c * tq:(c + 1) * tq, :] = v_buf[(c + 1) * tq:(c + 2) * tq, :]
        u_buf[0:POOL_HIST, :] = u_buf[tq:tq + POOL_HIST, :]

    d_model = x.shape[-1]
    gate_off = POOL_WIDTH + 3 * ATT_WIDTH
    ga = _dot(h, win_ref[:, gate_off:gate_off + d_model])
    gb = _dot(h, win_ref[:, gate_off + d_model:gate_off + 2 * d_model])
    m = jax.nn.sigmoid(ga) * _dot(a, wbp_ref[...]) + jax.nn.sigmoid(gb) * _dot(o, wba_ref[...])
    x1 = x + _dot(m.astype(_bf16), wout_ref[...])
    _store_row_tiles(x1_ref, x1, tq)

    h2 = x1 * _rms_scale(x1) * gffn_ref[...]
    logits = jnp.dot(h2, wroute_ref[...], precision=lax.Precision.HIGHEST,
                     preferred_element_type=_f32) + broute_ref[...]
    route_ref[...] = _route_rows(logits)


def _const_spec(shape):
    return pl.BlockSpec(shape, lambda *_: (0,) * len(shape), pipeline_mode=pl.Buffered(1))


def _mixer_call(x, k0, v0, p0, wts, shared, *, total_rows, tq, pos0, row_off, name):
    bsz, t_len, d_model = x.shape
    n_tiles = t_len // tq
    tail = min(BAND_PREV, t_len)
    tail_tiles = tail // tq
    band = BAND_PREV + tq
    blk_off = row_off // tq

    def tail_map(b, t):
        return (b, jnp.maximum(t - (n_tiles - tail_tiles), 0), 0)

    in_specs = [
        pl.BlockSpec((1, tq, d_model), lambda b, t: (b, t, 0)),
        pl.BlockSpec((1, BAND_PREV, ATT_WIDTH), lambda b, t: (b, 0, 0)),
        pl.BlockSpec((1, BAND_PREV, ATT_WIDTH), lambda b, t: (b, 0, 0)),
        pl.BlockSpec((1, POOL_HIST, POOL_WIDTH), lambda b, t: (b, 0, 0)),
        _const_spec((1, d_model)),
        _const_spec(wts["w_in"].shape),
        _const_spec(wts["w_pool"].shape),
        _const_spec((1, POOL_WIDTH)),
        _const_spec((1, ATT_WIDTH)),
        _const_spec((1, ATT_WIDTH)),
        _const_spec((ATT_WIDTH, ATT_WIDTH)),
        _const_spec((N_HEADS, tq, band)),
        _const_spec(wts["w_br_pool"].shape),
        _const_spec(wts["w_br_att"].shape),
        _const_spec(wts["w_out"].shape),
        _const_spec((1, d_model)),
        _const_spec((d_model, ROUTE_LANES)),
        _const_spec((1, ROUTE_LANES)),
    ]
    operands = [x, k0, v0, p0, wts["g_mix"], wts["w_in"], wts["w_pool"], wts["pool_scale"], wts["g_q"],
                wts["g_k"], wts["bd"], wts["bias"][:, :tq, :band], wts["w_br_pool"], wts["w_br_att"],
                wts["w_out"], wts["g_ffn"], wts["w_route"], wts["b_route"]]
    aliases = {}
    if shared is not None:
        aliases = {len(operands): 0, len(operands) + 1: 1}
        in_specs += [pl.BlockSpec(memory_space=pl.ANY)] * 2
        operands += list(shared)
    out_specs = [
        pl.BlockSpec((tq * SUBLANES, LANES), lambda b, t: (blk_off + b * n_tiles + t, 0)),
        pl.BlockSpec((tq, ROUTE_LANES), lambda b, t: (blk_off + b * n_tiles + t, 0)),
        pl.BlockSpec((1, tq, ATT_WIDTH), tail_map),
        pl.BlockSpec((1, tq, ATT_WIDTH), tail_map),
        pl.BlockSpec((1, POOL_STATE, POOL_WIDTH), lambda b, t: (b, 0, 0)),
    ]
    out_shape = [
        jax.ShapeDtypeStruct((total_rows * SUBLANES, LANES), _f32),
        jax.ShapeDtypeStruct((total_rows, ROUTE_LANES), _f32),
        jax.ShapeDtypeStruct((bsz, tail, ATT_WIDTH), _f32),
        jax.ShapeDtypeStruct((bsz, tail, ATT_WIDTH), _f32),
        jax.ShapeDtypeStruct((bsz, POOL_STATE, POOL_WIDTH), _f32),
    ]
    kern = functools.partial(_mixer_kernel, tq=tq, n_tiles=n_tiles, pos0=pos0)
    return pl.pallas_call(
        kern,
        grid=(bsz, n_tiles),
        in_specs=in_specs,
        out_specs=out_specs,
        out_shape=out_shape,
        scratch_shapes=[
            pltpu.VMEM((ATT_WIDTH, band), _bf16),
            pltpu.VMEM((band, ATT_WIDTH), _bf16),
            pltpu.VMEM((POOL_HIST + tq, POOL_WIDTH), _f32),
        ],
        input_output_aliases=aliases,
        compiler_params=pltpu.CompilerParams(
            dimension_semantics=("arbitrary", "arbitrary"), vmem_limit_bytes=VMEM_LIMIT_BYTES),
        name=name,
    )(*operands)


def _moe_kernel(ea_ref, eb_ref, nb_ref,
                src_cur_ref, src_nxt_ref, dst_ref, wrow_ref, gffn_ref,
                wga_ref, wua_ref, wda_ref, wgb_ref, wub_ref, wdb_ref, x1_hbm,
                x2_hbm, xbuf, obuf, gsem, ssem):
    del ea_ref, eb_ref
    b = pl.program_id(0)
    nb = nb_ref[0]
    tile_rows = MOE_ROWS * SUBLANES

    def gather_copy(idx_ref, r, slot):
        return pltpu.make_async_copy(
            x1_hbm.at[pl.ds(idx_ref[0, 0, r] * SUBLANES, SUBLANES), :],
            xbuf.at[slot, pl.ds(r * SUBLANES, SUBLANES), :],
            gsem.at[slot])

    def scatter_copy(r):
        return pltpu.make_async_copy(
            obuf.at[pl.ds(r * SUBLANES, SUBLANES), :],
            x2_hbm.at[pl.ds(dst_ref[0, 0, r] * SUBLANES, SUBLANES), :],
            ssem.at[0])

    def start_gather(idx_ref, slot):
        def body(r, carry):
            gather_copy(idx_ref, r, slot).start()
            return carry
        lax.fori_loop(0, MOE_ROWS, body, 0, unroll=8)

    def wait_scatter():
        pltpu.make_async_copy(obuf, x2_hbm.at[pl.ds(0, tile_rows), :], ssem.at[0]).wait()

    @pl.when(b < nb)
    def _block():
        slot = b % 2

        @pl.when(b == 0)
        def _():
            start_gather(src_cur_ref, 0)

        @pl.when(b + 1 < nb)
        def _():
            start_gather(src_nxt_ref, 1 - slot)

        pltpu.make_async_copy(x1_hbm.at[pl.ds(0, tile_rows), :], xbuf.at[slot], gsem.at[slot]).wait()

        x = _load_row_tiles(xbuf.at[slot], MOE_ROWS)
        h = (x * _rms_scale(x) * gffn_ref[...]).astype(_bf16)
        ri = lax.broadcasted_iota(jnp.int32, (MOE_ROWS, MOE_ROWS), 0)
        ci = lax.broadcasted_iota(jnp.int32, (MOE_ROWS, MOE_ROWS), 1)
        wa = jnp.sum(jnp.where(ri == ci, wrow_ref[0, 0:1, :], 0.0), axis=-1, keepdims=True)
        wb = jnp.sum(jnp.where(ri == ci, wrow_ref[0, 1:2, :], 0.0), axis=-1, keepdims=True)

        def expert(wg_ref, wu_ref, wd_ref):
            g = _dot(h, wg_ref[0])
            u = _dot(h, wu_ref[0])
            return _dot((jax.nn.silu(g) * u).astype(_bf16), wd_ref[0])

        y = wa * expert(wga_ref, wua_ref, wda_ref) + wb * expert(wgb_ref, wub_ref, wdb_ref)

        @pl.when(b > 0)
        def _():
            wait_scatter()

        _store_row_tiles(obuf, x + y, MOE_ROWS)

        def body(r, carry):
            scatter_copy(r).start()
            return carry
        lax.fori_loop(0, MOE_ROWS, body, 0, unroll=8)

        @pl.when(b == nb - 1)
        def _():
            wait_scatter()


def _moe_call(x1_buf, plan, wts, n_blocks):
    d_model = wts["w_gate_e"].shape[1]
    d_exp = wts["w_gate_e"].shape[2]
    last = n_blocks - 1

    def up_spec(which):
        return pl.BlockSpec((1, d_model, d_exp), lambda b, ea, eb, nb: ((ea, eb)[which][b], 0, 0))

    def down_spec(which):
        return pl.BlockSpec((1, d_exp, d_model), lambda b, ea, eb, nb: ((ea, eb)[which][b], 0, 0))

    idx_spec = functools.partial(pl.BlockSpec, (1, 1, MOE_ROWS), memory_space=pltpu.SMEM)
    grid_spec = pltpu.PrefetchScalarGridSpec(
        num_scalar_prefetch=3,
        grid=(n_blocks,),
        in_specs=[
            idx_spec(index_map=lambda b, *_: (b, 0, 0)),
            idx_spec(index_map=lambda b, *_: (jnp.minimum(b + 1, last), 0, 0)),
            idx_spec(index_map=lambda b, *_: (b, 0, 0)),
            pl.BlockSpec((1, SUBLANES, MOE_ROWS), lambda b, *_: (b, 0, 0)),
            pl.BlockSpec((1, d_model), lambda b, *_: (0, 0)),
            up_spec(0), up_spec(0), down_spec(0), up_spec(1), up_spec(1), down_spec(1),
            pl.BlockSpec(memory_space=pl.ANY),
        ],
        out_specs=pl.BlockSpec(memory_space=pl.ANY),
        scratch_shapes=[
            pltpu.VMEM((2, MOE_ROWS * SUBLANES, LANES), _f32),
            pltpu.VMEM((MOE_ROWS * SUBLANES, LANES), _f32),
            pltpu.SemaphoreType.DMA((2,)),
            pltpu.SemaphoreType.DMA((1,)),
        ],
    )
    n_out_rows = x1_buf.shape[0] + MOE_ROWS * SUBLANES
    return pl.pallas_call(
        _moe_kernel,
        grid_spec=grid_spec,
        out_shape=jax.ShapeDtypeStruct((n_out_rows, LANES), _f32),
        compiler_params=pltpu.CompilerParams(
            dimension_semantics=("arbitrary",), vmem_limit_bytes=VMEM_LIMIT_BYTES),
        name="moe",
    )(plan["ea"], plan["eb"], plan["nb"],
      plan["src"], plan["src"], plan["dst"], plan["wrow"], wts["g_ffn"],
      wts["w_gate_e"], wts["w_up_e"], wts["w_down_e"], wts["w_gate_e"], wts["w_up_e"], wts["w_down_e"],
      x1_buf)


_PAIR_A, _PAIR_B = np.triu_indices(EXPERTS_PER_GROUP, k=1)


def _dispatch_plan(route, n_blocks):
    total = route.shape[0]
    e1 = route[:, 0].astype(jnp.int32)
    e2 = route[:, 1].astype(jnp.int32)
    first = e1 < e2
    ea = jnp.where(first, e1, e2)
    eb = jnp.where(first, e2, e1)
    w_ab = jnp.where(first[:, None], route[:, 2:4], route[:, 3:1:-1])
    la = ea % EXPERTS_PER_GROUP
    lb = eb % EXPERTS_PER_GROUP
    cls = (ea // EXPERTS_PER_GROUP) * PAIRS_PER_GROUP + la * (2 * EXPERTS_PER_GROUP - 1 - la) // 2 + (lb - la - 1)
    order = jnp.argsort(cls, stable=True).astype(jnp.int32)
    cls_s = cls[order]
    counts = jnp.sum(cls[:, None] == jnp.arange(N_CLASSES, dtype=jnp.int32)[None, :], axis=0, dtype=jnp.int32)
    padded = (counts + MOE_ROWS - 1) // MOE_ROWS * MOE_ROWS
    pend = jnp.cumsum(padded)
    pstart = pend - padded
    start = jnp.cumsum(counts) - counts
    dest = pstart[cls_s] + jnp.arange(total, dtype=jnp.int32) - start[cls_s]
    n_slots = n_blocks * MOE_ROWS
    slot_ids = jnp.arange(n_slots, dtype=jnp.int32)
    src = jnp.zeros((n_slots,), jnp.int32).at[dest].set(order)
    dst = (total + slot_ids % MOE_ROWS).at[dest].set(order)
    wrow = jnp.zeros((n_slots, 2), _f32).at[dest].set(w_ab[order])
    wrow = jnp.pad(wrow.reshape(n_blocks, MOE_ROWS, 2).transpose(0, 2, 1), ((0, 0), (0, SUBLANES - 2), (0, 0)))
    nb = pend[-1] // MOE_ROWS
    blk = jnp.arange(n_blocks, dtype=jnp.int32)
    blk_cls = jnp.searchsorted(pend, jnp.minimum(blk, nb - 1) * MOE_ROWS, side="right").astype(jnp.int32)
    blk_cls = jnp.minimum(blk_cls, N_CLASSES - 1)
    grp = blk_cls // PAIRS_PER_GROUP
    pair = blk_cls % PAIRS_PER_GROUP
    blk_ea = grp * EXPERTS_PER_GROUP + jnp.asarray(_PAIR_A, jnp.int32)[pair]
    blk_eb = grp * EXPERTS_PER_GROUP + jnp.asarray(_PAIR_B, jnp.int32)[pair]
    return {
        "ea": blk_ea, "eb": blk_eb, "nb": nb.reshape(1).astype(jnp.int32),
        "src": src.reshape(n_blocks, 1, MOE_ROWS), "dst": dst.reshape(n_blocks, 1, MOE_ROWS), "wrow": wrow,
    }


def _ple_kernel(x2_ref, p_ref, gple_ref, wgate_ref, wproj_ref, out_ref, *, rows):
    x2 = _load_row_tiles(x2_ref, rows)
    hn = (x2 * _rms_scale(x2) * gple_ref[...]).astype(_bf16)
    gate = jax.nn.sigmoid(_dot(hn, wgate_ref[...]))
    out_ref[...] = x2 + _dot(p_ref[...].astype(_bf16), wproj_ref[...]) * gate


def _ple_call(x2_buf, p, wts, *, rows, row_off, name):
    n_rows, d_ple = p.shape
    d_model = wts["w_ple_gate"].shape[0]
    blk_off = row_off // rows
    return pl.pallas_call(
        functools.partial(_ple_kernel, rows=rows),
        grid=(n_rows // rows,),
        in_specs=[
            pl.BlockSpec((rows * SUBLANES, LANES), lambda i: (blk_off + i, 0)),
            pl.BlockSpec((rows, d_ple), lambda i: (i, 0)),
            _const_spec((1, d_model)),
            _const_spec((d_model, d_model)),
            _const_spec((d_ple, d_model)),
        ],
        out_specs=pl.BlockSpec((rows, d_model), lambda i: (i, 0)),
        out_shape=jax.ShapeDtypeStruct((n_rows, d_model), _f32),
        compiler_params=pltpu.CompilerParams(
            dimension_semantics=("arbitrary",), vmem_limit_bytes=VMEM_LIMIT_BYTES),
        name=name,
    )(x2_buf, p, wts["g_ple"], wts["w_ple_gate"], wts["w_ple_proj"])


def _band_bias(table, tq):
    band = BAND_PREV + tq
    qi = np.arange(tq)[:, None]
    kj = np.arange(band)[None, :]
    idx = np.clip(qi + BAND_PREV - kj, -(CHUNK - 1), REL_CLIP) + (CHUNK - 1)
    kc = kj // CHUNK
    qc = qi // CHUNK
    in_band = (kc >= qc) & (kc <= qc + N_PREV_CHUNKS)
    bias = table.astype(_f32)[:, jnp.asarray(idx, jnp.int32)]
    return jnp.where(jnp.asarray(in_band)[None], bias, MASKED)


def _layer_weights(l, tq, g_mix, w_in, w_pool, pool_scale, g_q, g_k, rel_table, w_br_pool, w_br_att, w_out,
                   g_ffn, w_coarse, b_coarse, w_fine, b_fine, w_gate_e, w_up_e, w_down_e, g_ple, w_ple_gate,
                   w_ple_proj):
    d_model = w_in.shape[1]
    pad = ROUTE_LANES - N_GROUPS - N_EXPERTS
    head_sum = np.kron(np.eye(N_HEADS), np.full((HEAD_DIM, HEAD_DIM), 1.0 / HEAD_DIM))
    return {
        "g_mix": g_mix[l].reshape(1, d_model),
        "w_in": w_in[l].astype(_bf16),
        "w_pool": w_pool[l].astype(_bf16),
        "pool_scale": pool_scale[l].reshape(1, POOL_WIDTH),
        "g_q": (jnp.tile(g_q[l], N_HEADS) * HEAD_DIM ** -0.5).reshape(1, ATT_WIDTH),
        "g_k": jnp.tile(g_k[l], N_HEADS).reshape(1, ATT_WIDTH),
        "bd": jnp.asarray(head_sum, _bf16),
        "bias": _band_bias(rel_table[l], tq),
        "w_br_pool": w_br_pool[l].astype(_bf16),
        "w_br_att": w_br_att[l].astype(_bf16),
        "w_out": w_out[l].astype(_bf16),
        "g_ffn": g_ffn[l].reshape(1, d_model),
        "w_route": jnp.concatenate([w_coarse[l], w_fine[l], jnp.zeros((d_model, pad), _f32)], axis=1),
        "b_route": jnp.concatenate([b_coarse[l], b_fine[l], jnp.zeros((pad,), _f32)]).reshape(1, ROUTE_LANES),
        "w_gate_e": w_gate_e[l].astype(_bf16),
        "w_up_e": w_up_e[l].astype(_bf16),
        "w_down_e": w_down_e[l].astype(_bf16),
        "g_ple": g_ple[l].reshape(1, d_model),
        "w_ple_gate": w_ple_gate[l].astype(_bf16),
        "w_ple_proj": w_ple_proj[l].astype(_bf16),
    }


def _pick_tile(t_len, want):
    tq = min(want, t_len)
    assert t_len % tq == 0 and tq % CHUNK == 0 and BAND_PREV % tq == 0
    return tq


def kernel(x_prompt, x_sample, cache_k, cache_v, state_pool, p_prompt, p_sample, g_mix, w_in, w_pool, pool_scale, g_q, g_k, rel_table, w_br_pool, w_br_att, w_out, g_ffn, w_coarse, b_coarse, w_fine, b_fine, w_gate_e, w_up_e, w_down_e, g_ple, w_ple_gate, w_ple_proj):
    depth = w_in.shape[0]
    bp, tp, d_model = x_prompt.shape
    bs, ts, _ = x_sample.shape
    assert d_model == SUBLANES * LANES
    tq_p = _pick_tile(tp, 256)
    tq_s = _pick_tile(ts, 256)
    rows_p, rows_s = bp * tp, bs * ts
    total = rows_p + rows_s
    ple_rows = 512 if rows_p % 512 == 0 and rows_s % 512 == 0 else CHUNK
    assert rows_p % MOE_ROWS == 0 and rows_p % ple_rows == 0 and rows_s % ple_rows == 0
    n_blocks = -(-(total + N_CLASSES * (MOE_ROWS - 1)) // MOE_ROWS)

    xp, xs = x_prompt, x_sample
    outs = [[] for _ in range(6)]
    for l in range(depth):
        wts = _layer_weights(l, max(tq_p, tq_s), g_mix, w_in, w_pool, pool_scale, g_q, g_k, rel_table, w_br_pool,
                             w_br_att, w_out, g_ffn, w_coarse, b_coarse, w_fine, b_fine, w_gate_e, w_up_e,
                             w_down_e, g_ple, w_ple_gate, w_ple_proj)
        zeros_kv = jnp.zeros((bp, BAND_PREV, ATT_WIDTH), _f32)
        zeros_pool = jnp.zeros((bp, POOL_HIST, POOL_WIDTH), _f32)
        x1_buf, route_buf, kp, vp, pp = _mixer_call(
            xp, zeros_kv, zeros_kv, zeros_pool, wts, None,
            total_rows=total, tq=tq_p, pos0=0, row_off=0, name="mixer_prompt")
        pool_hist = jnp.pad(state_pool[l], ((0, 0), (POOL_HIST - POOL_STATE, 0), (0, 0)))
        x1_buf, route_buf, kn, vn, pn = _mixer_call(
            xs, cache_k[l].reshape(bs, BAND_PREV, ATT_WIDTH), cache_v[l].reshape(bs, BAND_PREV, ATT_WIDTH),
            pool_hist, wts, (x1_buf, route_buf),
            total_rows=total, tq=tq_s, pos0=PAST_LEN, row_off=rows_p, name="mixer_sample")

        plan = _dispatch_plan(route_buf, n_blocks)
        x2_buf = _moe_call(x1_buf, plan, wts, n_blocks)

        xp = _ple_call(x2_buf, p_prompt[l].reshape(rows_p, -1), wts, rows=ple_rows, row_off=0,
                       name="ple_prompt").reshape(bp, tp, d_model)
        xs = _ple_call(x2_buf, p_sample[l].reshape(rows_s, -1), wts, rows=ple_rows, row_off=rows_p,
                       name="ple_sample").reshape(bs, ts, d_model)

        tail_p = min(BAND_PREV, tp)
        if tail_p < BAND_PREV:
            kp = jnp.pad(kp, ((0, 0), (BAND_PREV - tail_p, 0), (0, 0)))
            vp = jnp.pad(vp, ((0, 0), (BAND_PREV - tail_p, 0), (0, 0)))
        outs[0].append(kp.reshape(bp, BAND_PREV, N_HEADS, HEAD_DIM))
        outs[1].append(vp.reshape(bp, BAND_PREV, N_HEADS, HEAD_DIM))
        outs[2].append(pp)
        outs[3].append(kn.reshape(bs, ts, N_HEADS, HEAD_DIM))
        outs[4].append(vn.reshape(bs, ts, N_HEADS, HEAD_DIM))
        outs[5].append(pn)
    return (xp, xs) + tuple(jnp.stack(o) for o in outs)
```

```python
import functools

import jax
import jax.numpy as jnp
from jax import lax
from jax.experimental import pallas as pl
from jax.experimental.pallas import tpu as pltpu

CHUNK = 64
N_HEADS = 8
HEAD_DIM = 64
ATT_WIDTH = N_HEADS * HEAD_DIM
POOL_WINDOWS = (2, 4, 8, 16)
POOL_GROUP = 128
POOL_WIDTH = POOL_GROUP * len(POOL_WINDOWS)
POOL_STATE = max(POOL_WINDOWS) - 1
N_PREV_CHUNKS = 8
BAND_PREV = N_PREV_CHUNKS * CHUNK
REL_CLIP = 256
N_GROUPS = 4
EXPERTS_PER_GROUP = 8
N_EXPERTS = N_GROUPS * EXPERTS_PER_GROUP
PAST_LEN = 2048
EPS = 1e-6
MASKED = -1e30

LANES = 128
SUBLANES = 8
VMEM_LIMIT_BYTES = 56 * 1024 * 1024

POOL_HIST = 16
PAIRS_PER_GROUP = EXPERTS_PER_GROUP * (EXPERTS_PER_GROUP - 1) // 2
N_CLASSES = N_GROUPS * PAIRS_PER_GROUP
MOE_ROWS = 256
RANK_ROWS = 512
ROUTE_LANES = 128
ROW_TILE = MOE_ROWS * SUBLANES

_f32 = jnp.float32
_bf16 = jnp.bfloat16


def _dot(a, b):
    return jnp.dot(a, b, preferred_element_type=_f32)


def _rms_scale(x):
    return lax.rsqrt(jnp.mean(x * x, axis=-1, keepdims=True) + EPS)


def _split_dot(x, w_bf16):
    hi = x.astype(_bf16)
    lo = (x - hi.astype(_f32)).astype(_bf16)
    return _dot(hi, w_bf16) + _dot(lo, w_bf16)


def _load_row_tiles(ref, rows):
    return jnp.concatenate([ref[pl.ds(s, rows, stride=SUBLANES), :] for s in range(SUBLANES)], axis=-1)


def _store_row_tiles(ref, val, rows):
    for s in range(SUBLANES):
        ref[pl.ds(s, rows, stride=SUBLANES), :] = val[:, s * LANES:(s + 1) * LANES]


def _const_spec(shape):
    return pl.BlockSpec(shape, lambda *_: (0,) * len(shape), pipeline_mode=pl.Buffered(1))


def _route_rows(logits):
    lane = lax.broadcasted_iota(jnp.int32, (1, ROUTE_LANES), 1)
    lane_f = lane.astype(_f32)
    neg = jnp.float32(-jnp.inf)
    far = jnp.float32(ROUTE_LANES)
    cl = jnp.where(lane < N_GROUPS, logits, neg)
    cmax = jnp.max(cl, axis=-1, keepdims=True)
    grp = jnp.min(jnp.where(cl == cmax, lane_f, far), axis=-1, keepdims=True)
    fine_grp = ((lane - N_GROUPS) >> 3).astype(_f32)
    fl = jnp.where(fine_grp == grp, logits, neg)
    m1 = jnp.max(fl, axis=-1, keepdims=True)
    i1 = jnp.min(jnp.where(fl == m1, lane_f, far), axis=-1, keepdims=True)
    fl2 = jnp.where(lane_f == i1, neg, fl)
    m2 = jnp.max(fl2, axis=-1, keepdims=True)
    i2 = jnp.min(jnp.where(fl2 == m2, lane_f, far), axis=-1, keepdims=True)
    first_lane = N_GROUPS + EXPERTS_PER_GROUP * grp
    la = jnp.minimum(i1, i2) - first_lane
    lb = jnp.maximum(i1, i2) - first_lane
    return PAIRS_PER_GROUP * grp + la * (2 * EXPERTS_PER_GROUP - 1 - la) * 0.5 + (lb - la - 1.0)


def _mixer_kernel(x_ref, k0_ref, v0_ref, p0_ref, gmix_ref, win_ref, wpool_ref, pscale_ref, gq_ref, gk_ref,
                  bd_ref, bias_ref, wbp_ref, wba_ref, wout_ref, gffn_ref, wroute_ref, broute_ref,
                  *rest, tq, n_tiles, pos0):
    x1_ref, route_ref, kout_ref, vout_ref, pout_ref, kt_buf, v_buf, u_buf = rest[-8:]
    t = pl.program_id(1)
    band = BAND_PREV + tq

    @pl.when(t == 0)
    def _init_history():
        kt_buf[:, 0:BAND_PREV] = k0_ref[0].T.astype(_bf16)
        v_buf[0:BAND_PREV, :] = v0_ref[0].astype(_bf16)
        u_buf[0:POOL_HIST, :] = p0_ref[0]

    x = x_ref[0]
    h = (x * _rms_scale(x) * gmix_ref[...]).astype(_bf16)

    u = _dot(h, win_ref[:, 0:POOL_WIDTH])
    u_buf[POOL_HIST:POOL_HIST + tq, :] = u
    pout_ref[0] = u_buf[POOL_HIST + tq - POOL_STATE:POOL_HIST + tq, :]
    row = lax.broadcasted_iota(jnp.int32, (tq, 1), 0)
    pos1 = pos0 + t * tq + row + 1
    pooled = []
    for g, w in enumerate(POOL_WINDOWS):
        sl = slice(g * POOL_GROUP, (g + 1) * POOL_GROUP)
        acc = u[:, sl]
        for j in range(1, w):
            acc = acc + u_buf[POOL_HIST - j:POOL_HIST - j + tq, sl]
        cnt = jnp.minimum(pos1, w).astype(_f32)
        d = acc / cnt - u[:, sl]
        pooled.append(_dot(d.astype(_bf16), wpool_ref[g]))
    a = (jnp.concatenate(pooled, axis=-1) * pscale_ref[...]).astype(_bf16)

    q = _dot(h, win_ref[:, POOL_WIDTH:POOL_WIDTH + ATT_WIDTH])
    qn = (q * lax.rsqrt(_split_dot(q * q, bd_ref[...]) + EPS) * gq_ref[...]).astype(_bf16)
    k = _dot(h, win_ref[:, POOL_WIDTH + ATT_WIDTH:POOL_WIDTH + 2 * ATT_WIDTH])
    kn = k * lax.rsqrt(_split_dot(k * k, bd_ref[...]) + EPS) * gk_ref[...]
    v = _dot(h, win_ref[:, POOL_WIDTH + 2 * ATT_WIDTH:POOL_WIDTH + 3 * ATT_WIDTH])
    kout_ref[0] = kn
    vout_ref[0] = v
    kt_buf[:, BAND_PREV:band] = kn.T.astype(_bf16)
    v_buf[BAND_PREV:band, :] = v.astype(_bf16)

    lane = lax.broadcasted_iota(jnp.int32, (1, LANES), 1)
    even = lane < HEAD_DIM
    col = lax.broadcasted_iota(jnp.int32, (1, band), 1)
    started = col >= BAND_PREV - (pos0 + t * tq)
    heads = []
    for p in range(N_HEADS // 2):
        sl = slice(p * LANES, (p + 1) * LANES)
        qp = qn[:, sl]
        ktp = kt_buf[sl, :]
        vp = v_buf[:, sl]
        acc = None
        for half in range(2):
            keep = even if half == 0 else jnp.logical_not(even)
            qh = jnp.where(keep, qp, jnp.zeros_like(qp))
            vh = jnp.where(keep, vp, jnp.zeros_like(vp))
            s = _dot(qh, ktp) + bias_ref[2 * p + half]
            s = jnp.where(started, s, MASKED)
            e = jnp.exp(s - jnp.max(s, axis=-1, keepdims=True))
            pr = (e * (1.0 / jnp.sum(e, axis=-1, keepdims=True))).astype(_bf16)
            part = _dot(pr, vh)
            acc = part if acc is None else acc + part
        heads.append(acc)
    o = jnp.concatenate(heads, axis=-1).astype(_bf16)

    if n_tiles > 1:
        for c in range(BAND_PREV // tq):
            kt_buf[:, c * tq:(c + 1) * tq] = kt_buf[:, (c + 1) * tq:(c + 2) * tq]
            v_buf[c * tq:(c + 1) * tq, :] = v_buf[(c + 1) * tq:(c + 2) * tq, :]
        u_buf[0:POOL_HIST, :] = u_buf[tq:tq + POOL_HIST, :]

    d_model = x.shape[-1]
    gate_off = POOL_WIDTH + 3 * ATT_WIDTH
    ga = _dot(h, win_ref[:, gate_off:gate_off + d_model])
    gb = _dot(h, win_ref[:, gate_off + d_model:gate_off + 2 * d_model])
    m = jax.nn.sigmoid(ga) * _dot(a, wbp_ref[...]) + jax.nn.sigmoid(gb) * _dot(o, wba_ref[...])
    x1 = x + _dot(m.astype(_bf16), wout_ref[...])
    _store_row_tiles(x1_ref, x1, tq)

    h2 = x1 * _rms_scale(x1) * gffn_ref[...]
    h2_hi = h2.astype(_bf16)
    h2_lo = (h2 - h2_hi.astype(_f32)).astype(_bf16)
    logits = (_dot(h2_hi, wroute_ref[0]) + _dot(h2_lo, wroute_ref[0]) + _dot(h2_hi, wroute_ref[1])
              + broute_ref[...])
    route_ref[...] = jnp.broadcast_to(_route_rows(logits), route_ref.shape)


def _mixer_call(x, k0, v0, p0, wts, shared, *, total_rows, tq, pos0, row_off, name):
    bsz, t_len, d_model = x.shape
    n_tiles = t_len // tq
    tail = min(BAND_PREV, t_len)
    tail_tiles = tail // tq
    band = BAND_PREV + tq
    blk_off = row_off // tq

    def tail_map(b, t):
        return (b, jnp.maximum(t - (n_tiles - tail_tiles), 0), 0)

    in_specs = [
        pl.BlockSpec((1, tq, d_model), lambda b, t: (b, t, 0)),
        pl.BlockSpec((1, BAND_PREV, ATT_WIDTH), lambda b, t: (b, 0, 0)),
        pl.BlockSpec((1, BAND_PREV, ATT_WIDTH), lambda b, t: (b, 0, 0)),
        pl.BlockSpec((1, POOL_HIST, POOL_WIDTH), lambda b, t: (b, 0, 0)),
        _const_spec((1, d_model)),
        _const_spec(wts["w_in"].shape),
        _const_spec(wts["w_pool"].shape),
        _const_spec((1, POOL_WIDTH)),
        _const_spec((1, ATT_WIDTH)),
        _const_spec((1, ATT_WIDTH)),
        _const_spec((ATT_WIDTH, ATT_WIDTH)),
        _const_spec((N_HEADS, tq, band)),
        _const_spec(wts["w_br_pool"].shape),
        _const_spec(wts["w_br_att"].shape),
        _const_spec(wts["w_out"].shape),
        _const_spec((1, d_model)),
        _const_spec((2, d_model, ROUTE_LANES)),
        _const_spec((1, ROUTE_LANES)),
    ]
    operands = [x, k0, v0, p0, wts["g_mix"], wts["w_in"], wts["w_pool"], wts["pool_scale"], wts["g_q"],
                wts["g_k"], wts["bd"], wts["bias"][:, :tq, :band], wts["w_br_pool"], wts["w_br_att"],
                wts["w_out"], wts["g_ffn"], wts["w_route"], wts["b_route"]]
    aliases = {}
    if shared is not None:
        aliases = {len(operands): 0, len(operands) + 1: 1}
        in_specs += [pl.BlockSpec(memory_space=pl.ANY)] * 2
        operands += list(shared)
    out_specs = [
        pl.BlockSpec((tq * SUBLANES, LANES), lambda b, t: (blk_off + b * n_tiles + t, 0)),
        pl.BlockSpec((tq, ROUTE_LANES), lambda b, t: (blk_off + b * n_tiles + t, 0)),
        pl.BlockSpec((1, tq, ATT_WIDTH), tail_map),
        pl.BlockSpec((1, tq, ATT_WIDTH), tail_map),
        pl.BlockSpec((1, POOL_STATE, POOL_WIDTH), lambda b, t: (b, 0, 0)),
    ]
    out_shape = [
        jax.ShapeDtypeStruct((total_rows * SUBLANES, LANES), _f32),
        jax.ShapeDtypeStruct((total_rows, ROUTE_LANES), _f32),
        jax.ShapeDtypeStruct((bsz, tail, ATT_WIDTH), _f32),
        jax.ShapeDtypeStruct((bsz, tail, ATT_WIDTH), _f32),
        jax.ShapeDtypeStruct((bsz, POOL_STATE, POOL_WIDTH), _f32),
    ]
    kern = functools.partial(_mixer_kernel, tq=tq, n_tiles=n_tiles, pos0=pos0)
    return pl.pallas_call(
        kern,
        grid=(bsz, n_tiles),
        in_specs=in_specs,
        out_specs=out_specs,
        out_shape=out_shape,
        scratch_shapes=[
            pltpu.VMEM((ATT_WIDTH, band), _bf16),
            pltpu.VMEM((band, ATT_WIDTH), _bf16),
            pltpu.VMEM((POOL_HIST + tq, POOL_WIDTH), _f32),
        ],
        input_output_aliases=aliases,
        compiler_params=pltpu.CompilerParams(
            dimension_semantics=("arbitrary", "arbitrary"), vmem_limit_bytes=VMEM_LIMIT_BYTES),
        name=name,
    )(*operands)


def _lane_cumsum(x):
    lane = lax.broadcasted_iota(jnp.int32, x.shape, 1)
    shift = 1
    while shift < LANES:
        x = x + jnp.where(lane >= shift, pltpu.roll(x, shift, axis=1), 0.0)
        shift *= 2
    return x


def _rank_kernel(route_ref, dest_ref, counts_ref, hist, base):
    ph = pl.program_id(0)
    i = pl.program_id(1)
    lane = lax.broadcasted_iota(jnp.int32, (1, LANES), 1).astype(_f32)
    onehot = route_ref[...] == lane
    oh = jnp.where(onehot, 1.0, 0.0)
    colsum = jnp.sum(oh, axis=0, keepdims=True)

    @pl.when(jnp.logical_and(ph == 0, i == 0))
    def _():
        hist[...] = jnp.zeros_like(hist)

    @pl.when(ph == 0)
    def _():
        hist[...] += colsum

    @pl.when(jnp.logical_and(ph == 1, i == 0))
    def _():
        cnt = hist[...]
        padded = jnp.floor((cnt + (MOE_ROWS - 1)) * (1.0 / MOE_ROWS)) * MOE_ROWS
        base[...] = _lane_cumsum(padded) - padded
        counts_ref[...] = cnt

    @pl.when(ph == 1)
    def _():
        rows = route_ref.shape[0]
        ri = lax.broadcasted_iota(jnp.int32, (rows, rows), 0)
        ci = lax.broadcasted_iota(jnp.int32, (rows, rows), 1)
        earlier = jnp.where(ri > ci, 1.0, 0.0).astype(_bf16)
        prefix = _dot(earlier, oh.astype(_bf16))
        slot_col = jnp.sum(jnp.where(onehot, base[0:1, :] + prefix, 0.0), axis=-1, keepdims=True)
        base[...] += colsum
        slot_row = jnp.sum(jnp.where(ri == ci, slot_col, 0.0), axis=0, keepdims=True)
        dest_ref[0] = slot_row.astype(jnp.int32)


def _rank_call(route_buf):
    total = route_buf.shape[0]
    n_tiles = total // RANK_ROWS
    return pl.pallas_call(
        _rank_kernel,
        grid=(2, n_tiles),
        in_specs=[pl.BlockSpec((RANK_ROWS, ROUTE_LANES), lambda ph, i: (i, 0))],
        out_specs=[
            pl.BlockSpec((1, 1, RANK_ROWS), lambda ph, i: (ph * i, 0, 0)),
            pl.BlockSpec((SUBLANES, LANES), lambda ph, i: (0, 0)),
        ],
        out_shape=[
            jax.ShapeDtypeStruct((n_tiles, 1, RANK_ROWS), jnp.int32),
            jax.ShapeDtypeStruct((SUBLANES, LANES), _f32),
        ],
        scratch_shapes=[pltpu.VMEM((SUBLANES, LANES), _f32), pltpu.VMEM((SUBLANES, LANES), _f32)],
        compiler_params=pltpu.CompilerParams(dimension_semantics=("arbitrary", "arbitrary")),
        name="rank",
    )(route_buf)


def _block_tables(counts, n_blocks):
    cnt = counts[0, :N_CLASSES].astype(jnp.int32)
    padded = (cnt + MOE_ROWS - 1) // MOE_ROWS * MOE_ROWS
    pend = jnp.cumsum(padded)
    nb = pend[-1] // MOE_ROWS
    blk = jnp.minimum(jnp.arange(n_blocks, dtype=jnp.int32), nb - 1)
    blk_cls = jnp.sum(pend[None, :] <= (blk * MOE_ROWS)[:, None], axis=1, dtype=jnp.int32)
    blk_cls = jnp.minimum(blk_cls, N_CLASSES - 1)
    grp = blk_cls // PAIRS_PER_GROUP
    pair = blk_cls % PAIRS_PER_GROUP
    firsts = jnp.arange(1, EXPERTS_PER_GROUP, dtype=jnp.int32)
    pair_start = firsts * (2 * EXPERTS_PER_GROUP - 1 - firsts) // 2
    la = jnp.sum(pair[:, None] >= pair_start[None, :], axis=1, dtype=jnp.int32)
    lb = pair - la * (2 * EXPERTS_PER_GROUP - 1 - la) // 2 + la + 1
    zero_end = jnp.where(cnt > 0, pend, 0).astype(jnp.int32)
    return {
        "ea": grp * EXPERTS_PER_GROUP + la, "eb": grp * EXPERTS_PER_GROUP + lb,
        "nb": nb.reshape(1).astype(jnp.int32), "zero_end": zero_end,
    }


N_DISPATCH_BUFS = 3


def _dispatch_kernel(zero_end_ref, idx_ref, x1_hbm, xs_hbm, tbuf, zbuf, lsem, ssem, zsem):
    i = pl.program_id(0)
    n = pl.num_programs(0)
    slot = i % N_DISPATCH_BUFS

    def load_copy(tile, s):
        return pltpu.make_async_copy(x1_hbm.at[pl.ds(tile * ROW_TILE, ROW_TILE), :], tbuf.at[s], lsem.at[s])

    def wait_scatter(s):
        pltpu.make_async_copy(tbuf.at[s], xs_hbm.at[pl.ds(0, ROW_TILE), :], ssem.at[s]).wait()

    def zero_copy(c):
        end = zero_end_ref[c]
        return pltpu.make_async_copy(zbuf, xs_hbm.at[pl.ds((end - MOE_ROWS) * SUBLANES, ROW_TILE), :], zsem.at[0])

    @pl.when(i == 0)
    def _():
        zbuf[...] = jnp.zeros_like(zbuf)

        def start(c, carry):
            @pl.when(zero_end_ref[c] > 0)
            def _():
                zero_copy(c).start()
            return carry

        def wait(c, carry):
            @pl.when(zero_end_ref[c] > 0)
            def _():
                zero_copy(c).wait()
            return carry

        lax.fori_loop(0, N_CLASSES, start, 0)
        lax.fori_loop(0, N_CLASSES, wait, 0)
        load_copy(0, 0).start()

    @pl.when(i >= N_DISPATCH_BUFS - 1)
    def _():
        wait_scatter((i + 1) % N_DISPATCH_BUFS)

    @pl.when(i + 1 < n)
    def _():
        load_copy(i + 1, (i + 1) % N_DISPATCH_BUFS).start()

    load_copy(i, slot).wait()

    def body(r, carry):
        pltpu.make_async_copy(
            tbuf.at[slot, pl.ds(r * SUBLANES, SUBLANES), :],
            xs_hbm.at[pl.ds(idx_ref[0, 0, r] * SUBLANES, SUBLANES), :],
            ssem.at[slot]).start()
        return carry
    lax.fori_loop(0, MOE_ROWS, body, 0, unroll=8)

    @pl.when(i == n - 1)
    def _():
        @pl.when(n > 1)
        def _():
            wait_scatter((i + N_DISPATCH_BUFS - 1) % N_DISPATCH_BUFS)
        wait_scatter(slot)


def _dispatch_call(x1_buf, dest, zero_end, n_blocks):
    n_tiles = dest.shape[0]
    grid_spec = pltpu.PrefetchScalarGridSpec(
        num_scalar_prefetch=1,
        grid=(n_tiles,),
        in_specs=[
            pl.BlockSpec((1, 1, MOE_ROWS), lambda i, *_: (i, 0, 0), memory_space=pltpu.SMEM),
            pl.BlockSpec(memory_space=pl.ANY),
        ],
        out_specs=pl.BlockSpec(memory_space=pl.ANY),
        scratch_shapes=[
            pltpu.VMEM((N_DISPATCH_BUFS, ROW_TILE, LANES), _f32),
            pltpu.VMEM((ROW_TILE, LANES), _f32),
            pltpu.SemaphoreType.DMA((N_DISPATCH_BUFS,)),
            pltpu.SemaphoreType.DMA((N_DISPATCH_BUFS,)),
            pltpu.SemaphoreType.DMA((1,)),
        ],
    )
    return pl.pallas_call(
        _dispatch_kernel,
        grid_spec=grid_spec,
        out_shape=jax.ShapeDtypeStruct((n_blocks * ROW_TILE, LANES), _f32),
        compiler_params=pltpu.CompilerParams(dimension_semantics=("arbitrary",)),
        name="dispatch",
    )(zero_end, dest, x1_buf)


def _moe_kernel(ea_ref, eb_ref, nb_ref, xs_ref, gffn_ref, wrt_ref, brt_ref,
                wga_ref, wua_ref, wda_ref, wgb_ref, wub_ref, wdb_ref, out_ref):
    b = pl.program_id(0)

    @pl.when(b < nb_ref[0])
    def _block():
        x = _load_row_tiles(xs_ref, MOE_ROWS)
        h2 = x * _rms_scale(x) * gffn_ref[...]
        h = h2.astype(_bf16)
        ea = ea_ref[b]
        eb = eb_ref[b]
        grp = ea // EXPERTS_PER_GROUP

        def logit(col):
            return jnp.sum(h2 * wrt_ref[pl.ds(col, 1), :], axis=-1, keepdims=True) + brt_ref[col]

        cl = [logit(c) for c in range(N_GROUPS)]
        cmax = functools.reduce(jnp.maximum, cl)
        csum = functools.reduce(jnp.add, [jnp.exp(c - cmax) for c in cl])
        cl_own = functools.reduce(lambda acc, c: jnp.where(grp == c, cl[c], acc), range(1, N_GROUPS), cl[0])
        gp = jnp.exp(cl_own - cmax) / csum
        fa = logit(N_GROUPS + ea)
        fb = logit(N_GROUPS + eb)
        fmax = jnp.maximum(fa, fb)
        pa = jnp.exp(fa - fmax)
        pb = jnp.exp(fb - fmax)
        wa = gp * pa / (pa + pb)
        wb = gp * pb / (pa + pb)

        def expert(wg_ref, wu_ref, wd_ref):
            g = _dot(h, wg_ref[0])
            u = _dot(h, wu_ref[0])
            return _dot((jax.nn.silu(g) * u).astype(_bf16), wd_ref[0])

        y = wa * expert(wga_ref, wua_ref, wda_ref) + wb * expert(wgb_ref, wub_ref, wdb_ref)
        _store_row_tiles(out_ref, x + y, MOE_ROWS)


def _moe_call(xs_buf, tables, wts, n_blocks):
    d_model = wts["w_gate_e"].shape[1]
    d_exp = wts["w_gate_e"].shape[2]

    def used_block(b, ea, eb, nb):
        return (jnp.minimum(b, nb[0] - 1), 0)

    def up_spec(which):
        return pl.BlockSpec((1, d_model, d_exp), lambda b, ea, eb, nb: ((ea, eb)[which][b], 0, 0))

    def down_spec(which):
        return pl.BlockSpec((1, d_exp, d_model), lambda b, ea, eb, nb: ((ea, eb)[which][b], 0, 0))

    grid_spec = pltpu.PrefetchScalarGridSpec(
        num_scalar_prefetch=3,
        grid=(n_blocks,),
        in_specs=[
            pl.BlockSpec((ROW_TILE, LANES), used_block),
            pl.BlockSpec((1, d_model), lambda b, *_: (0, 0)),
            pl.BlockSpec((ROUTE_LANES, d_model), lambda b, *_: (0, 0)),
            pl.BlockSpec(memory_space=pltpu.SMEM),
            up_spec(0), up_spec(0), down_spec(0), up_spec(1), up_spec(1), down_spec(1),
        ],
        out_specs=pl.BlockSpec((ROW_TILE, LANES), used_block),
    )
    return pl.pallas_call(
        _moe_kernel,
        grid_spec=grid_spec,
        out_shape=jax.ShapeDtypeStruct(xs_buf.shape, _f32),
        compiler_params=pltpu.CompilerParams(
            dimension_semantics=("arbitrary",), vmem_limit_bytes=VMEM_LIMIT_BYTES),
        name="moe",
    )(tables["ea"], tables["eb"], tables["nb"], xs_buf, wts["g_ffn"], wts["w_route_t"], wts["b_route_flat"],
      wts["w_gate_e"], wts["w_up_e"], wts["w_down_e"], wts["w_gate_e"], wts["w_up_e"], wts["w_down_e"])


def _ple_kernel(idx_cur_ref, idx_nxt_ref, p_ref, gple_ref, wgate_ref, wproj_ref, x2s_hbm, out_ref, xbuf, gsem):
    i = pl.program_id(0)
    n = pl.num_programs(0)
    slot = i % 2

    def start_gather(idx_ref, s):
        def body(r, carry):
            pltpu.make_async_copy(
                x2s_hbm.at[pl.ds(idx_ref[0, 0, r] * SUBLANES, SUBLANES), :],
                xbuf.at[s, pl.ds(r * SUBLANES, SUBLANES), :],
                gsem.at[s]).start()
            return carry
        lax.fori_loop(0, MOE_ROWS, body, 0, unroll=8)

    @pl.when(i == 0)
    def _():
        start_gather(idx_cur_ref, 0)

    @pl.when(i + 1 < n)
    def _():
        start_gather(idx_nxt_ref, 1 - slot)

    pltpu.make_async_copy(x2s_hbm.at[pl.ds(0, ROW_TILE), :], xbuf.at[slot], gsem.at[slot]).wait()

    x2 = _load_row_tiles(xbuf.at[slot], MOE_ROWS)
    hn = (x2 * _rms_scale(x2) * gple_ref[...]).astype(_bf16)
    gate = jax.nn.sigmoid(_dot(hn, wgate_ref[...]))
    out_ref[...] = x2 + _dot(p_ref[...].astype(_bf16), wproj_ref[...]) * gate


def _ple_call(x2s_buf, dest, p, wts, *, row_off, name):
    n_rows, d_ple = p.shape
    d_model = wts["w_ple_gate"].shape[0]
    n_tiles = n_rows // MOE_ROWS
    blk_off = row_off // MOE_ROWS
    last = blk_off + n_tiles - 1
    idx_spec = functools.partial(pl.BlockSpec, (1, 1, MOE_ROWS), memory_space=pltpu.SMEM)
    return pl.pallas_call(
        _ple_kernel,
        grid=(n_tiles,),
        in_specs=[
            idx_spec(index_map=lambda i: (blk_off + i, 0, 0)),
            idx_spec(index_map=lambda i: (jnp.minimum(blk_off + i + 1, last), 0, 0)),
            pl.BlockSpec((MOE_ROWS, d_ple), lambda i: (i, 0)),
            _const_spec((1, d_model)),
            _const_spec((d_model, d_model)),
            _const_spec((d_ple, d_model)),
            pl.BlockSpec(memory_space=pl.ANY),
        ],
        out_specs=pl.BlockSpec((MOE_ROWS, d_model), lambda i: (i, 0)),
        out_shape=jax.ShapeDtypeStruct((n_rows, d_model), _f32),
        scratch_shapes=[pltpu.VMEM((2, ROW_TILE, LANES), _f32), pltpu.SemaphoreType.DMA((2,))],
        compiler_params=pltpu.CompilerParams(
            dimension_semantics=("arbitrary",), vmem_limit_bytes=VMEM_LIMIT_BYTES),
        name=name,
    )(dest, dest, p, wts["g_ple"], wts["w_ple_gate"], wts["w_ple_proj"], x2s_buf)


def _band_bias(table, tq):
    band = BAND_PREV + tq
    n_heads, n_rel = table.shape
    n_far = band - 1 - REL_CLIP
    length = band + tq
    n_near = length - n_far - n_rel
    tab = table.astype(_f32)
    f = jnp.concatenate([jnp.broadcast_to(tab[:, n_rel - 1:], (n_heads, n_far)), tab[:, ::-1],
                         jnp.broadcast_to(tab[:, :1], (n_heads, n_near))], axis=1)
    skew = jnp.tile(f, (1, tq))[:, :tq * (length - 1)].reshape(n_heads, tq, length - 1)
    bias = skew[:, :, tq - 1:tq - 1 + band]
    kc = lax.broadcasted_iota(jnp.int32, (tq, band), 1) // CHUNK
    qc = lax.broadcasted_iota(jnp.int32, (tq, band), 0) // CHUNK
    in_band = (kc >= qc) & (kc <= qc + N_PREV_CHUNKS)
    return jnp.where(in_band[None], bias, MASKED)


def _layer_weights(l, tq, g_mix, w_in, w_pool, pool_scale, g_q, g_k, rel_table, w_br_pool, w_br_att, w_out,
                   g_ffn, w_coarse, b_coarse, w_fine, b_fine, w_gate_e, w_up_e, w_down_e, g_ple, w_ple_gate,
                   w_ple_proj):
    d_model = w_in.shape[1]
    pad = ROUTE_LANES - N_GROUPS - N_EXPERTS
    head_of = jnp.arange(ATT_WIDTH, dtype=jnp.int32) // HEAD_DIM
    head_sum = jnp.where(head_of[:, None] == head_of[None, :], 1.0 / HEAD_DIM, 0.0)
    w_route = jnp.concatenate([w_coarse[l], w_fine[l], jnp.zeros((d_model, pad), _f32)], axis=1)
    w_route_hi = w_route.astype(_bf16)
    w_route_lo = (w_route - w_route_hi.astype(_f32)).astype(_bf16)
    b_route = jnp.concatenate([b_coarse[l], b_fine[l], jnp.zeros((pad,), _f32)])
    return {
        "g_mix": g_mix[l].reshape(1, d_model),
        "w_in": w_in[l].astype(_bf16),
        "w_pool": w_pool[l].astype(_bf16),
        "pool_scale": pool_scale[l].reshape(1, POOL_WIDTH),
        "g_q": (jnp.tile(g_q[l], N_HEADS) * HEAD_DIM ** -0.5).reshape(1, ATT_WIDTH),
        "g_k": jnp.tile(g_k[l], N_HEADS).reshape(1, ATT_WIDTH),
        "bd": head_sum.astype(_bf16),
        "bias": _band_bias(rel_table[l], tq),
        "w_br_pool": w_br_pool[l].astype(_bf16),
        "w_br_att": w_br_att[l].astype(_bf16),
        "w_out": w_out[l].astype(_bf16),
        "g_ffn": g_ffn[l].reshape(1, d_model),
        "w_route": jnp.stack([w_route_hi, w_route_lo]),
        "b_route": b_route.reshape(1, ROUTE_LANES),
        "w_route_t": w_route.T,
        "b_route_flat": b_route,
        "w_gate_e": w_gate_e[l].astype(_bf16),
        "w_up_e": w_up_e[l].astype(_bf16),
        "w_down_e": w_down_e[l].astype(_bf16),
        "g_ple": g_ple[l].reshape(1, d_model),
        "w_ple_gate": w_ple_gate[l].astype(_bf16),
        "w_ple_proj": w_ple_proj[l].astype(_bf16),
    }


def _pick_tile(t_len, want):
    tq = min(want, t_len)
    assert t_len % tq == 0 and tq % CHUNK == 0 and BAND_PREV % tq == 0
    return tq


def kernel(x_prompt, x_sample, cache_k, cache_v, state_pool, p_prompt, p_sample, g_mix, w_in, w_pool, pool_scale, g_q, g_k, rel_table, w_br_pool, w_br_att, w_out, g_ffn, w_coarse, b_coarse, w_fine, b_fine, w_gate_e, w_up_e, w_down_e, g_ple, w_ple_gate, w_ple_proj):
    depth = w_in.shape[0]
    bp, tp, d_model = x_prompt.shape
    bs, ts, _ = x_sample.shape
    assert d_model == SUBLANES * LANES
    tq_p = _pick_tile(tp, 256)
    tq_s = _pick_tile(ts, 256)
    rows_p, rows_s = bp * tp, bs * ts
    total = rows_p + rows_s
    assert rows_p % MOE_ROWS == 0 and rows_s % MOE_ROWS == 0 and total % RANK_ROWS == 0
    n_blocks = -(-(total + N_CLASSES * (MOE_ROWS - 1)) // MOE_ROWS)

    xp, xs = x_prompt, x_sample
    outs = [[] for _ in range(6)]
    for l in range(depth):
        wts = _layer_weights(l, max(tq_p, tq_s), g_mix, w_in, w_pool, pool_scale, g_q, g_k, rel_table, w_br_pool,
                             w_br_att, w_out, g_ffn, w_coarse, b_coarse, w_fine, b_fine, w_gate_e, w_up_e,
                             w_down_e, g_ple, w_ple_gate, w_ple_proj)
        zeros_kv = jnp.zeros((bp, BAND_PREV, ATT_WIDTH), _f32)
        zeros_pool = jnp.zeros((bp, POOL_HIST, POOL_WIDTH), _f32)
        x1_buf, route_buf, kp, vp, pp = _mixer_call(
            xp, zeros_kv, zeros_kv, zeros_pool, wts, None,
            total_rows=total, tq=tq_p, pos0=0, row_off=0, name="mixer_prompt")
        pool_hist = jnp.pad(state_pool[l], ((0, 0), (POOL_HIST - POOL_STATE, 0), (0, 0)))
        x1_buf, route_buf, kn, vn, pn = _mixer_call(
            xs, cache_k[l].reshape(bs, BAND_PREV, ATT_WIDTH), cache_v[l].reshape(bs, BAND_PREV, ATT_WIDTH),
            pool_hist, wts, (x1_buf, route_buf),
            total_rows=total, tq=tq_s, pos0=PAST_LEN, row_off=rows_p, name="mixer_sample")

        dest, counts = _rank_call(route_buf)
        dest = dest.reshape(total // MOE_ROWS, 1, MOE_ROWS)
        tables = _block_tables(counts, n_blocks)
        xs_buf = _dispatch_call(x1_buf, dest, tables["zero_end"], n_blocks)
        x2s_buf = _moe_call(xs_buf, tables, wts, n_blocks)

        xp = _ple_call(x2s_buf, dest, p_prompt[l].reshape(rows_p, -1), wts, row_off=0,
                       name="ple_prompt").reshape(bp, tp, d_model)
        xs = _ple_call(x2s_buf, dest, p_sample[l].reshape(rows_s, -1), wts, row_off=rows_p,
                       name="ple_sample").reshape(bs, ts, d_model)

        tail_p = min(BAND_PREV, tp)
        if tail_p < BAND_PREV:
            kp = jnp.pad(kp, ((0, 0), (BAND_PREV - tail_p, 0), (0, 0)))
            vp = jnp.pad(vp, ((0, 0), (BAND_PREV - tail_p, 0), (0, 0)))
        outs[0].append(kp.reshape(bp, BAND_PREV, N_HEADS, HEAD_DIM))
        outs[1].append(vp.reshape(bp, BAND_PREV, N_HEADS, HEAD_DIM))
        outs[2].append(pp)
        outs[3].append(kn.reshape(bs, ts, N_HEADS, HEAD_DIM))
        outs[4].append(vn.reshape(bs, ts, N_HEADS, HEAD_DIM))
        outs[5].append(pn)
    return (xp, xs) + tuple(jnp.stack(o) for o in outs)
```

```python
import functools

import jax
import jax.numpy as jnp
from jax import lax
from jax.experimental import pallas as pl
from jax.experimental.pallas import tpu as pltpu

CHUNK = 64
N_HEADS = 8
HEAD_DIM = 64
ATT_WIDTH = N_HEADS * HEAD_DIM
POOL_WINDOWS = (2, 4, 8, 16)
POOL_GROUP = 128
POOL_WIDTH = POOL_GROUP * len(POOL_WINDOWS)
POOL_STATE = max(POOL_WINDOWS) - 1
N_PREV_CHUNKS = 8
BAND_PREV = N_PREV_CHUNKS * CHUNK
REL_CLIP = 256
N_GROUPS = 4
EXPERTS_PER_GROUP = 8
N_EXPERTS = N_GROUPS * EXPERTS_PER_GROUP
PAST_LEN = 2048
EPS = 1e-6
MASKED = -1e30
LOG2E = 1.4426950408889634

LANES = 128
SUBLANES = 8
VMEM_LIMIT_BYTES = 56 * 1024 * 1024

POOL_HIST = 16
PAIRS_PER_GROUP = EXPERTS_PER_GROUP * (EXPERTS_PER_GROUP - 1) // 2
N_CLASSES = N_GROUPS * PAIRS_PER_GROUP
MOE_ROWS = 256
RANK_ROWS = 512
ROUTE_LANES = 128
ROW_TILE = MOE_ROWS * SUBLANES

_f32 = jnp.float32
_bf16 = jnp.bfloat16


def _dot(a, b):
    return jnp.dot(a, b, preferred_element_type=_f32)


def _rms_scale(x):
    return lax.rsqrt(jnp.mean(x * x, axis=-1, keepdims=True) + EPS)


def _split_dot(x, w_bf16):
    hi = x.astype(_bf16)
    lo = (x - hi.astype(_f32)).astype(_bf16)
    return _dot(hi, w_bf16) + _dot(lo, w_bf16)


def _load_row_tiles(ref, rows):
    return jnp.concatenate([ref[pl.ds(s, rows, stride=SUBLANES), :] for s in range(SUBLANES)], axis=-1)


def _store_row_tiles(ref, val, rows):
    for s in range(SUBLANES):
        ref[pl.ds(s, rows, stride=SUBLANES), :] = val[:, s * LANES:(s + 1) * LANES]


def _const_spec(shape):
    return pl.BlockSpec(shape, lambda *_: (0,) * len(shape), pipeline_mode=pl.Buffered(1))


def _route_rows(logits):
    lane = lax.broadcasted_iota(jnp.int32, (1, ROUTE_LANES), 1)
    lane_f = lane.astype(_f32)
    neg = jnp.float32(-jnp.inf)
    far = jnp.float32(ROUTE_LANES)
    cl = jnp.where(lane < N_GROUPS, logits, neg)
    cmax = jnp.max(cl, axis=-1, keepdims=True)
    grp = jnp.min(jnp.where(cl == cmax, lane_f, far), axis=-1, keepdims=True)
    fine_grp = ((lane - N_GROUPS) >> 3).astype(_f32)
    fl = jnp.where(fine_grp == grp, logits, neg)
    m1 = jnp.max(fl, axis=-1, keepdims=True)
    i1 = jnp.min(jnp.where(fl == m1, lane_f, far), axis=-1, keepdims=True)
    fl2 = jnp.where(lane_f == i1, neg, fl)
    m2 = jnp.max(fl2, axis=-1, keepdims=True)
    i2 = jnp.min(jnp.where(fl2 == m2, lane_f, far), axis=-1, keepdims=True)
    first_lane = N_GROUPS + EXPERTS_PER_GROUP * grp
    la = jnp.minimum(i1, i2) - first_lane
    lb = jnp.maximum(i1, i2) - first_lane
    return PAIRS_PER_GROUP * grp + la * (2 * EXPERTS_PER_GROUP - 1 - la) * 0.5 + (lb - la - 1.0)


def _mixer_kernel(x_ref, k0_ref, v0_ref, p0_ref, gmix_ref, win_ref, wpool_ref, pscale_ref, gq_ref, gk_ref,
                  bd_ref, bias_ref, wbp_ref, wba_ref, wout_ref, gffn_ref, wroute_ref, broute_ref,
                  *rest, tq, n_tiles, pos0):
    x1_ref, route_ref, kout_ref, vout_ref, pout_ref, kt_buf, v_buf, u_buf = rest[-8:]
    t = pl.program_id(1)
    band = BAND_PREV + tq

    @pl.when(t == 0)
    def _init_history():
        kt_buf[:, 0:BAND_PREV] = k0_ref[0].T.astype(_bf16)
        v_buf[0:BAND_PREV, :] = v0_ref[0].astype(_bf16)
        u_buf[0:POOL_HIST, :] = p0_ref[0]

    x = x_ref[0]
    h = (x * _rms_scale(x) * gmix_ref[...]).astype(_bf16)

    u = _dot(h, win_ref[:, 0:POOL_WIDTH])
    u_buf[POOL_HIST:POOL_HIST + tq, :] = u
    pout_ref[0] = u_buf[POOL_HIST + tq - POOL_STATE:POOL_HIST + tq, :]
    row = lax.broadcasted_iota(jnp.int32, (tq, 1), 0)
    pos1 = pos0 + t * tq + row + 1
    pooled = []
    for g, w in enumerate(POOL_WINDOWS):
        sl = slice(g * POOL_GROUP, (g + 1) * POOL_GROUP)
        acc = u[:, sl]
        for j in range(1, w):
            acc = acc + u_buf[POOL_HIST - j:POOL_HIST - j + tq, sl]
        cnt = jnp.minimum(pos1, w).astype(_f32)
        d = acc / cnt - u[:, sl]
        pooled.append(_dot(d.astype(_bf16), wpool_ref[g]))
    a = (jnp.concatenate(pooled, axis=-1) * pscale_ref[...]).astype(_bf16)

    q = _dot(h, win_ref[:, POOL_WIDTH:POOL_WIDTH + ATT_WIDTH])
    qn = (q * lax.rsqrt(_split_dot(q * q, bd_ref[...]) + EPS) * gq_ref[...]).astype(_bf16)
    k = _dot(h, win_ref[:, POOL_WIDTH + ATT_WIDTH:POOL_WIDTH + 2 * ATT_WIDTH])
    kn = k * lax.rsqrt(_split_dot(k * k, bd_ref[...]) + EPS) * gk_ref[...]
    v = _dot(h, win_ref[:, POOL_WIDTH + 2 * ATT_WIDTH:POOL_WIDTH + 3 * ATT_WIDTH])
    kout_ref[0] = kn
    vout_ref[0] = v
    kt_buf[:, BAND_PREV:band] = kn.T.astype(_bf16)
    v_buf[BAND_PREV:band, :] = v.astype(_bf16)

    lane = lax.broadcasted_iota(jnp.int32, (1, LANES), 1)
    even = lane < HEAD_DIM
    col = lax.broadcasted_iota(jnp.int32, (1, band), 1)
    started = col >= BAND_PREV - (pos0 + t * tq)
    heads = []
    for p in range(N_HEADS // 2):
        sl = slice(p * LANES, (p + 1) * LANES)
        qp = qn[:, sl]
        ktp = kt_buf[sl, :]
        vp = v_buf[:, sl]
        acc = None
        inv = []
        for half in range(2):
            keep = even if half == 0 else jnp.logical_not(even)
            qh = jnp.where(keep, qp, jnp.zeros_like(qp))
            vh = jnp.where(keep, vp, jnp.zeros_like(vp))
            s = _dot(qh, ktp) + bias_ref[2 * p + half]
            s = jnp.where(started, s, MASKED)
            e = jnp.exp2(s - jnp.max(s, axis=-1, keepdims=True))
            inv.append(1.0 / jnp.sum(e, axis=-1, keepdims=True))
            part = _dot(e.astype(_bf16), vh)
            acc = part if acc is None else acc + part
        heads.append(acc * jnp.where(even, inv[0], inv[1]))
    o = jnp.concatenate(heads, axis=-1).astype(_bf16)

    if n_tiles > 1:
        for c in range(BAND_PREV // tq):
            kt_buf[:, c * tq:(c + 1) * tq] = kt_buf[:, (c + 1) * tq:(c + 2) * tq]
            v_buf[c * tq:(c + 1) * tq, :] = v_buf[(c + 1) * tq:(c + 2) * tq, :]
        u_buf[0:POOL_HIST, :] = u_buf[tq:tq + POOL_HIST, :]

    d_model = x.shape[-1]
    gate_off = POOL_WIDTH + 3 * ATT_WIDTH
    ga = _dot(h, win_ref[:, gate_off:gate_off + d_model])
    gb = _dot(h, win_ref[:, gate_off + d_model:gate_off + 2 * d_model])
    m = jax.nn.sigmoid(ga) * _dot(a, wbp_ref[...]) + jax.nn.sigmoid(gb) * _dot(o, wba_ref[...])
    x1 = x + _dot(m.astype(_bf16), wout_ref[...])
    _store_row_tiles(x1_ref, x1, tq)

    h2 = x1 * _rms_scale(x1) * gffn_ref[...]
    h2_hi = h2.astype(_bf16)
    h2_lo = (h2 - h2_hi.astype(_f32)).astype(_bf16)
    hi_both = _dot(h2_hi, wroute_ref[...])
    logits = (hi_both[:, 0:ROUTE_LANES] + hi_both[:, ROUTE_LANES:] + _dot(h2_lo, wroute_ref[:, 0:ROUTE_LANES])
              + broute_ref[...])
    route_ref[...] = jnp.broadcast_to(_route_rows(logits), route_ref.shape)


def _mixer_call(x, k0, v0, p0, wts, shared, *, total_rows, tq, pos0, row_off, name):
    bsz, t_len, d_model = x.shape
    n_tiles = t_len // tq
    tail = min(BAND_PREV, t_len)
    tail_tiles = tail // tq
    band = BAND_PREV + tq
    blk_off = row_off // tq

    def tail_map(b, t):
        return (b, jnp.maximum(t - (n_tiles - tail_tiles), 0), 0)

    in_specs = [
        pl.BlockSpec((1, tq, d_model), lambda b, t: (b, t, 0)),
        pl.BlockSpec((1, BAND_PREV, ATT_WIDTH), lambda b, t: (b, 0, 0)),
        pl.BlockSpec((1, BAND_PREV, ATT_WIDTH), lambda b, t: (b, 0, 0)),
        pl.BlockSpec((1, POOL_HIST, POOL_WIDTH), lambda b, t: (b, 0, 0)),
        _const_spec((1, d_model)),
        _const_spec(wts["w_in"].shape),
        _const_spec(wts["w_pool"].shape),
        _const_spec((1, POOL_WIDTH)),
        _const_spec((1, ATT_WIDTH)),
        _const_spec((1, ATT_WIDTH)),
        _const_spec((ATT_WIDTH, ATT_WIDTH)),
        _const_spec((N_HEADS, tq, band)),
        _const_spec(wts["w_br_pool"].shape),
        _const_spec(wts["w_br_att"].shape),
        _const_spec(wts["w_out"].shape),
        _const_spec((1, d_model)),
        _const_spec((d_model, 2 * ROUTE_LANES)),
        _const_spec((1, ROUTE_LANES)),
    ]
    operands = [x, k0, v0, p0, wts["g_mix"], wts["w_in"], wts["w_pool"], wts["pool_scale"], wts["g_q"],
                wts["g_k"], wts["bd"], wts["bias"][:, :tq, :band], wts["w_br_pool"], wts["w_br_att"],
                wts["w_out"], wts["g_ffn"], wts["w_route"], wts["b_route"]]
    aliases = {}
    if shared is not None:
        aliases = {len(operands): 0, len(operands) + 1: 1}
        in_specs += [pl.BlockSpec(memory_space=pl.ANY)] * 2
        operands += list(shared)
    out_specs = [
        pl.BlockSpec((tq * SUBLANES, LANES), lambda b, t: (blk_off + b * n_tiles + t, 0)),
        pl.BlockSpec((tq, ROUTE_LANES), lambda b, t: (blk_off + b * n_tiles + t, 0)),
        pl.BlockSpec((1, tq, ATT_WIDTH), tail_map),
        pl.BlockSpec((1, tq, ATT_WIDTH), tail_map),
        pl.BlockSpec((1, POOL_STATE, POOL_WIDTH), lambda b, t: (b, 0, 0)),
    ]
    out_shape = [
        jax.ShapeDtypeStruct((total_rows * SUBLANES, LANES), _f32),
        jax.ShapeDtypeStruct((total_rows, ROUTE_LANES), _f32),
        jax.ShapeDtypeStruct((bsz, tail, ATT_WIDTH), _f32),
        jax.ShapeDtypeStruct((bsz, tail, ATT_WIDTH), _f32),
        jax.ShapeDtypeStruct((bsz, POOL_STATE, POOL_WIDTH), _f32),
    ]
    kern = functools.partial(_mixer_kernel, tq=tq, n_tiles=n_tiles, pos0=pos0)
    return pl.pallas_call(
        kern,
        grid=(bsz, n_tiles),
        in_specs=in_specs,
        out_specs=out_specs,
        out_shape=out_shape,
        scratch_shapes=[
            pltpu.VMEM((ATT_WIDTH, band), _bf16),
            pltpu.VMEM((band, ATT_WIDTH), _bf16),
            pltpu.VMEM((POOL_HIST + tq, POOL_WIDTH), _f32),
        ],
        input_output_aliases=aliases,
        compiler_params=pltpu.CompilerParams(
            dimension_semantics=("arbitrary", "arbitrary"), vmem_limit_bytes=VMEM_LIMIT_BYTES),
        name=name,
    )(*operands)


def _lane_cumsum(x):
    lane = lax.broadcasted_iota(jnp.int32, x.shape, 1)
    shift = 1
    while shift < LANES:
        x = x + jnp.where(lane >= shift, pltpu.roll(x, shift, axis=1), 0.0)
        shift *= 2
    return x


def _rank_kernel(route_ref, dest_ref, counts_ref, hist, base, before):
    ph = pl.program_id(0)
    i = pl.program_id(1)
    rows = route_ref.shape[0]
    lane = lax.broadcasted_iota(jnp.int32, (1, LANES), 1).astype(_f32)
    oh = jnp.where(route_ref[...] == lane, 1.0, 0.0)

    @pl.when(jnp.logical_and(ph == 0, i == 0))
    def _():
        hist[...] = jnp.zeros_like(hist)
        ri = lax.broadcasted_iota(jnp.int32, (rows, rows), 0)
        ci = lax.broadcasted_iota(jnp.int32, (rows, rows), 1)
        before[...] = jnp.where(ri < ci, 1.0, 0.0).astype(_bf16)

    @pl.when(ph == 0)
    def _():
        hist[...] += jnp.sum(oh, axis=0, keepdims=True)

    @pl.when(jnp.logical_and(ph == 1, i == 0))
    def _():
        cnt = hist[...]
        padded = jnp.floor((cnt + (MOE_ROWS - 1)) * (1.0 / MOE_ROWS)) * MOE_ROWS
        first = _lane_cumsum(padded) - padded
        ri = lax.broadcasted_iota(jnp.int32, (LANES, LANES), 0)
        ci = lax.broadcasted_iota(jnp.int32, (LANES, LANES), 1)
        first_col = jnp.sum(jnp.where(ri == ci, first[0:1, :], 0.0), axis=-1, keepdims=True)
        base[...] = jnp.broadcast_to(first_col, base.shape)
        counts_ref[...] = cnt

    @pl.when(ph == 1)
    def _():
        oh_t = oh.T
        earlier = _dot(oh_t.astype(_bf16), before[...])
        slot_row = jnp.sum(oh_t * (base[:, 0:1] + earlier), axis=0, keepdims=True)
        base[...] += jnp.sum(oh_t, axis=1, keepdims=True)
        dest_ref[0] = slot_row.astype(jnp.int32)


def _rank_call(route_buf):
    total = route_buf.shape[0]
    n_tiles = total // RANK_ROWS
    return pl.pallas_call(
        _rank_kernel,
        grid=(2, n_tiles),
        in_specs=[pl.BlockSpec((RANK_ROWS, ROUTE_LANES), lambda ph, i: (i, 0))],
        out_specs=[
            pl.BlockSpec((1, 1, RANK_ROWS), lambda ph, i: (ph * i, 0, 0)),
            pl.BlockSpec((SUBLANES, LANES), lambda ph, i: (0, 0)),
        ],
        out_shape=[
            jax.ShapeDtypeStruct((n_tiles, 1, RANK_ROWS), jnp.int32),
            jax.ShapeDtypeStruct((SUBLANES, LANES), _f32),
        ],
        scratch_shapes=[pltpu.VMEM((SUBLANES, LANES), _f32), pltpu.VMEM((LANES, LANES), _f32),
                        pltpu.VMEM((RANK_ROWS, RANK_ROWS), _bf16)],
        compiler_params=pltpu.CompilerParams(dimension_semantics=("arbitrary", "arbitrary")),
        name="rank",
    )(route_buf)


def _block_tables(counts, n_blocks):
    cnt = counts[0, :N_CLASSES].astype(jnp.int32)
    padded = (cnt + MOE_ROWS - 1) // MOE_ROWS * MOE_ROWS
    pend = jnp.cumsum(padded)
    nb = pend[-1] // MOE_ROWS
    blk = jnp.minimum(jnp.arange(n_blocks, dtype=jnp.int32), nb - 1)
    blk_cls = jnp.sum(pend[None, :] <= (blk * MOE_ROWS)[:, None], axis=1, dtype=jnp.int32)
    blk_cls = jnp.minimum(blk_cls, N_CLASSES - 1)
    grp = blk_cls // PAIRS_PER_GROUP
    pair = blk_cls % PAIRS_PER_GROUP
    firsts = jnp.arange(1, EXPERTS_PER_GROUP, dtype=jnp.int32)
    pair_start = firsts * (2 * EXPERTS_PER_GROUP - 1 - firsts) // 2
    la = jnp.sum(pair[:, None] >= pair_start[None, :], axis=1, dtype=jnp.int32)
    lb = pair - la * (2 * EXPERTS_PER_GROUP - 1 - la) // 2 + la + 1
    zero_end = jnp.where(cnt > 0, pend, 0).astype(jnp.int32)
    return {
        "ea": grp * EXPERTS_PER_GROUP + la, "eb": grp * EXPERTS_PER_GROUP + lb,
        "nb": nb.reshape(1).astype(jnp.int32), "zero_end": zero_end,
    }


N_DISPATCH_BUFS = 3


def _dispatch_kernel(zero_end_ref, idx_ref, x1_hbm, xs_hbm, tbuf, zbuf, lsem, ssem, zsem):
    i = pl.program_id(0)
    n = pl.num_programs(0)
    slot = i % N_DISPATCH_BUFS

    def load_copy(tile, s):
        return pltpu.make_async_copy(x1_hbm.at[pl.ds(tile * ROW_TILE, ROW_TILE), :], tbuf.at[s], lsem.at[s])

    def wait_scatter(s):
        pltpu.make_async_copy(tbuf.at[s], xs_hbm.at[pl.ds(0, ROW_TILE), :], ssem.at[s]).wait()

    def zero_copy(c):
        end = zero_end_ref[c]
        return pltpu.make_async_copy(zbuf, xs_hbm.at[pl.ds((end - MOE_ROWS) * SUBLANES, ROW_TILE), :], zsem.at[0])

    @pl.when(i == 0)
    def _():
        zbuf[...] = jnp.zeros_like(zbuf)

        def start(c, carry):
            @pl.when(zero_end_ref[c] > 0)
            def _():
                zero_copy(c).start()
            return carry

        def wait(c, carry):
            @pl.when(zero_end_ref[c] > 0)
            def _():
                zero_copy(c).wait()
            return carry

        lax.fori_loop(0, N_CLASSES, start, 0)
        lax.fori_loop(0, N_CLASSES, wait, 0)
        load_copy(0, 0).start()

    @pl.when(i >= N_DISPATCH_BUFS - 1)
    def _():
        wait_scatter((i + 1) % N_DISPATCH_BUFS)

    @pl.when(i + 1 < n)
    def _():
        load_copy(i + 1, (i + 1) % N_DISPATCH_BUFS).start()

    load_copy(i, slot).wait()

    def body(r, carry):
        pltpu.make_async_copy(
            tbuf.at[slot, pl.ds(r * SUBLANES, SUBLANES), :],
            xs_hbm.at[pl.ds(idx_ref[0, 0, r] * SUBLANES, SUBLANES), :],
            ssem.at[slot]).start()
        return carry
    lax.fori_loop(0, MOE_ROWS, body, 0, unroll=8)

    @pl.when(i == n - 1)
    def _():
        @pl.when(n > 1)
        def _():
            wait_scatter((i + N_DISPATCH_BUFS - 1) % N_DISPATCH_BUFS)
        wait_scatter(slot)


def _dispatch_call(x1_buf, dest, zero_end, n_blocks):
    n_tiles = dest.shape[0]
    grid_spec = pltpu.PrefetchScalarGridSpec(
        num_scalar_prefetch=1,
        grid=(n_tiles,),
        in_specs=[
            pl.BlockSpec((1, 1, MOE_ROWS), lambda i, *_: (i, 0, 0), memory_space=pltpu.SMEM),
            pl.BlockSpec(memory_space=pl.ANY),
        ],
        out_specs=pl.BlockSpec(memory_space=pl.ANY),
        scratch_shapes=[
            pltpu.VMEM((N_DISPATCH_BUFS, ROW_TILE, LANES), _f32),
            pltpu.VMEM((ROW_TILE, LANES), _f32),
            pltpu.SemaphoreType.DMA((N_DISPATCH_BUFS,)),
            pltpu.SemaphoreType.DMA((N_DISPATCH_BUFS,)),
            pltpu.SemaphoreType.DMA((1,)),
        ],
    )
    return pl.pallas_call(
        _dispatch_kernel,
        grid_spec=grid_spec,
        out_shape=jax.ShapeDtypeStruct((n_blocks * ROW_TILE, LANES), _f32),
        compiler_params=pltpu.CompilerParams(dimension_semantics=("arbitrary",)),
        name="dispatch",
    )(zero_end, dest, x1_buf)


def _moe_kernel(ea_ref, eb_ref, nb_ref, xs_ref, gffn_ref, wr_ref, br_ref,
                wga_ref, wua_ref, wda_ref, wgb_ref, wub_ref, wdb_ref, out_ref):
    b = pl.program_id(0)

    @pl.when(b < nb_ref[0])
    def _block():
        x = _load_row_tiles(xs_ref, MOE_ROWS)
        h = (x * _rms_scale(x) * gffn_ref[...]).astype(_bf16)
        ea = ea_ref[b]
        eb = eb_ref[b]
        grp = ea // EXPERTS_PER_GROUP

        logits = _dot(h, wr_ref[...]) + br_ref[...]
        lane = lax.broadcasted_iota(jnp.int32, (1, ROUTE_LANES), 1)

        def pick(col):
            return jnp.sum(jnp.where(lane == col, logits, 0.0), axis=-1, keepdims=True)

        coarse = lane < N_GROUPS
        cmax = jnp.max(jnp.where(coarse, logits, -jnp.inf), axis=-1, keepdims=True)
        csum = jnp.sum(jnp.where(coarse, jnp.exp(logits - cmax), 0.0), axis=-1, keepdims=True)
        gp = jnp.exp(pick(grp) - cmax) / csum
        fa = pick(N_GROUPS + ea)
        fb = pick(N_GROUPS + eb)
        fmax = jnp.maximum(fa, fb)
        pa = jnp.exp(fa - fmax)
        pb = jnp.exp(fb - fmax)
        wa = gp * pa / (pa + pb)
        wb = gp * pb / (pa + pb)

        def expert(wg_ref, wu_ref, wd_ref):
            g = _dot(h, wg_ref[0])
            u = _dot(h, wu_ref[0])
            return _dot((jax.nn.silu(g) * u).astype(_bf16), wd_ref[0])

        y = wa * expert(wga_ref, wua_ref, wda_ref) + wb * expert(wgb_ref, wub_ref, wdb_ref)
        _store_row_tiles(out_ref, x + y, MOE_ROWS)


def _moe_call(xs_buf, tables, wts, n_blocks):
    d_model = wts["w_gate_e"].shape[1]
    d_exp = wts["w_gate_e"].shape[2]

    def used_block(b, ea, eb, nb):
        return (jnp.minimum(b, nb[0] - 1), 0)

    def up_spec(which):
        return pl.BlockSpec((1, d_model, d_exp), lambda b, ea, eb, nb: ((ea, eb)[which][b], 0, 0))

    def down_spec(which):
        return pl.BlockSpec((1, d_exp, d_model), lambda b, ea, eb, nb: ((ea, eb)[which][b], 0, 0))

    grid_spec = pltpu.PrefetchScalarGridSpec(
        num_scalar_prefetch=3,
        grid=(n_blocks,),
        in_specs=[
            pl.BlockSpec((ROW_TILE, LANES), used_block),
            pl.BlockSpec((1, d_model), lambda b, *_: (0, 0)),
            pl.BlockSpec((d_model, ROUTE_LANES), lambda b, *_: (0, 0)),
            pl.BlockSpec((1, ROUTE_LANES), lambda b, *_: (0, 0)),
            up_spec(0), up_spec(0), down_spec(0), up_spec(1), up_spec(1), down_spec(1),
        ],
        out_specs=pl.BlockSpec((ROW_TILE, LANES), used_block),
    )
    return pl.pallas_call(
        _moe_kernel,
        grid_spec=grid_spec,
        out_shape=jax.ShapeDtypeStruct(xs_buf.shape, _f32),
        compiler_params=pltpu.CompilerParams(
            dimension_semantics=("arbitrary",), vmem_limit_bytes=VMEM_LIMIT_BYTES),
        name="moe",
    )(tables["ea"], tables["eb"], tables["nb"], xs_buf, wts["g_ffn"], wts["w_route_hi"], wts["b_route"],
      wts["w_gate_e"], wts["w_up_e"], wts["w_down_e"], wts["w_gate_e"], wts["w_up_e"], wts["w_down_e"])


def _ple_kernel(idx_ref, idx_nxt_ref, p_ref, gple_ref, wgate_ref, wproj_ref, x2s_hbm, out_ref, buf0, buf1, gsem):
    i = pl.program_id(0)
    n = pl.num_programs(0)

    def row_copy(idx_ref, r, off, buf, s):
        return pltpu.make_async_copy(
            x2s_hbm.at[pl.ds(idx_ref[0, 0, off + r] * SUBLANES, SUBLANES), :],
            buf.at[pl.ds(r * SUBLANES, SUBLANES), :],
            gsem.at[s])

    def start_gather(idx_ref, off, buf, s):
        for r in range(MOE_ROWS):
            row_copy(idx_ref, r, off, buf, s).start()

    def wait_gather(buf, s):
        pltpu.make_async_copy(x2s_hbm.at[pl.ds(0, ROW_TILE), :], buf, gsem.at[s]).wait()

    def tile(buf, half):
        rows = pl.ds(half * MOE_ROWS, MOE_ROWS)
        x2 = _load_row_tiles(buf, MOE_ROWS)
        hn = (x2 * _rms_scale(x2) * gple_ref[...]).astype(_bf16)
        gate = jax.nn.sigmoid(_dot(hn, wgate_ref[...]))
        out_ref[rows, :] = x2 + _dot(p_ref[rows, :].astype(_bf16), wproj_ref[...]) * gate

    @pl.when(i == 0)
    def _():
        def body(r, carry):
            row_copy(idx_ref, r, 0, buf0, 0).start()
            return carry
        lax.fori_loop(0, MOE_ROWS, body, 0, unroll=8)

    start_gather(idx_ref, MOE_ROWS, buf1, 1)
    wait_gather(buf0, 0)
    tile(buf0, 0)
    start_gather(idx_nxt_ref, 0, buf0, 0)
    wait_gather(buf1, 1)
    tile(buf1, 1)

    @pl.when(i == n - 1)
    def _():
        wait_gather(buf0, 0)


def _ple_call(x2s_buf, dest, p, wts, *, row_off, name):
    n_rows, d_ple = p.shape
    d_model = wts["w_ple_gate"].shape[0]
    step_rows = 2 * MOE_ROWS
    n_steps = n_rows // step_rows
    blk_off = row_off // step_rows
    last = blk_off + n_steps - 1
    dest = dest.reshape(-1, 1, step_rows)
    idx_spec = functools.partial(pl.BlockSpec, (1, 1, step_rows), memory_space=pltpu.SMEM)
    return pl.pallas_call(
        _ple_kernel,
        grid=(n_steps,),
        in_specs=[
            idx_spec(index_map=lambda i: (blk_off + i, 0, 0)),
            idx_spec(index_map=lambda i: (jnp.minimum(blk_off + i + 1, last), 0, 0)),
            pl.BlockSpec((step_rows, d_ple), lambda i: (i, 0)),
            _const_spec((1, d_model)),
            _const_spec((d_model, d_model)),
            _const_spec((d_ple, d_model)),
            pl.BlockSpec(memory_space=pl.ANY),
        ],
        out_specs=pl.BlockSpec((step_rows, d_model), lambda i: (i, 0)),
        out_shape=jax.ShapeDtypeStruct((n_rows, d_model), _f32),
        scratch_shapes=[pltpu.VMEM((ROW_TILE, LANES), _f32), pltpu.VMEM((ROW_TILE, LANES), _f32),
                        pltpu.SemaphoreType.DMA((2,))],
        compiler_params=pltpu.CompilerParams(
            dimension_semantics=("arbitrary",), vmem_limit_bytes=VMEM_LIMIT_BYTES),
        name=name,
    )(dest, dest, p, wts["g_ple"], wts["w_ple_gate"], wts["w_ple_proj"], x2s_buf)


def _band_bias(table, tq):
    band = BAND_PREV + tq
    n_heads, n_rel = table.shape
    n_far = band - 1 - REL_CLIP
    length = band + tq
    n_near = length - n_far - n_rel
    tab = table.astype(_f32)
    f = jnp.concatenate([jnp.broadcast_to(tab[:, n_rel - 1:], (n_heads, n_far)), tab[:, ::-1],
                         jnp.broadcast_to(tab[:, :1], (n_heads, n_near))], axis=1)
    skew = jnp.tile(f, (1, tq))[:, :tq * (length - 1)].reshape(n_heads, tq, length - 1)
    bias = skew[:, :, tq - 1:tq - 1 + band]
    kc = lax.broadcasted_iota(jnp.int32, (tq, band), 1) // CHUNK
    qc = lax.broadcasted_iota(jnp.int32, (tq, band), 0) // CHUNK
    in_band = (kc >= qc) & (kc <= qc + N_PREV_CHUNKS)
    return jnp.where(in_band[None], bias, MASKED)


def _layer_weights(l, tq, g_mix, w_in, w_pool, pool_scale, g_q, g_k, rel_table, w_br_pool, w_br_att, w_out,
                   g_ffn, w_coarse, b_coarse, w_fine, b_fine, w_gate_e, w_up_e, w_down_e, g_ple, w_ple_gate,
                   w_ple_proj):
    d_model = w_in.shape[1]
    pad = ROUTE_LANES - N_GROUPS - N_EXPERTS
    head_of = jnp.arange(ATT_WIDTH, dtype=jnp.int32) // HEAD_DIM
    head_sum = jnp.where(head_of[:, None] == head_of[None, :], 1.0 / HEAD_DIM, 0.0)
    w_route = jnp.concatenate([w_coarse[l], w_fine[l], jnp.zeros((d_model, pad), _f32)], axis=1)
    w_route_hi = w_route.astype(_bf16)
    w_route_lo = (w_route - w_route_hi.astype(_f32)).astype(_bf16)
    b_route = jnp.concatenate([b_coarse[l], b_fine[l], jnp.zeros((pad,), _f32)])
    return {
        "g_mix": g_mix[l].reshape(1, d_model),
        "w_in": w_in[l].astype(_bf16),
        "w_pool": w_pool[l].astype(_bf16),
        "pool_scale": pool_scale[l].reshape(1, POOL_WIDTH),
        "g_q": (jnp.tile(g_q[l], N_HEADS) * (HEAD_DIM ** -0.5 * LOG2E)).reshape(1, ATT_WIDTH),
        "g_k": jnp.tile(g_k[l], N_HEADS).reshape(1, ATT_WIDTH),
        "bd": head_sum.astype(_bf16),
        "bias": _band_bias(rel_table[l] * LOG2E, tq),
        "w_br_pool": w_br_pool[l].astype(_bf16),
        "w_br_att": w_br_att[l].astype(_bf16),
        "w_out": w_out[l].astype(_bf16),
        "g_ffn": g_ffn[l].reshape(1, d_model),
        "w_route": jnp.concatenate([w_route_hi, w_route_lo], axis=1),
        "b_route": b_route.reshape(1, ROUTE_LANES),
        "w_route_hi": w_route_hi,
        "w_gate_e": w_gate_e[l].astype(_bf16),
        "w_up_e": w_up_e[l].astype(_bf16),
        "w_down_e": w_down_e[l].astype(_bf16),
        "g_ple": g_ple[l].reshape(1, d_model),
        "w_ple_gate": w_ple_gate[l].astype(_bf16),
        "w_ple_proj": w_ple_proj[l].astype(_bf16),
    }


def _pick_tile(t_len, want):
    tq = min(want, t_len)
    assert t_len % tq == 0 and tq % CHUNK == 0 and BAND_PREV % tq == 0
    return tq


def kernel(x_prompt, x_sample, cache_k, cache_v, state_pool, p_prompt, p_sample, g_mix, w_in, w_pool, pool_scale, g_q, g_k, rel_table, w_br_pool, w_br_att, w_out, g_ffn, w_coarse, b_coarse, w_fine, b_fine, w_gate_e, w_up_e, w_down_e, g_ple, w_ple_gate, w_ple_proj):
    depth = w_in.shape[0]
    bp, tp, d_model = x_prompt.shape
    bs, ts, _ = x_sample.shape
    assert d_model == SUBLANES * LANES
    tq_p = _pick_tile(tp, 256)
    tq_s = _pick_tile(ts, 256)
    rows_p, rows_s = bp * tp, bs * ts
    total = rows_p + rows_s
    assert rows_p % (2 * MOE_ROWS) == 0 and rows_s % (2 * MOE_ROWS) == 0 and total % RANK_ROWS == 0
    n_blocks = -(-(total + N_CLASSES * (MOE_ROWS - 1)) // MOE_ROWS)

    xp, xs = x_prompt, x_sample
    outs = [[] for _ in range(6)]
    for l in range(depth):
        wts = _layer_weights(l, max(tq_p, tq_s), g_mix, w_in, w_pool, pool_scale, g_q, g_k, rel_table, w_br_pool,
                             w_br_att, w_out, g_ffn, w_coarse, b_coarse, w_fine, b_fine, w_gate_e, w_up_e,
                             w_down_e, g_ple, w_ple_gate, w_ple_proj)
        zeros_kv = jnp.zeros((bp, BAND_PREV, ATT_WIDTH), _f32)
        zeros_pool = jnp.zeros((bp, POOL_HIST, POOL_WIDTH), _f32)
        x1_buf, route_buf, kp, vp, pp = _mixer_call(
            xp, zeros_kv, zeros_kv, zeros_pool, wts, None,
            total_rows=total, tq=tq_p, pos0=0, row_off=0, name="mixer_prompt")
        pool_hist = jnp.pad(state_pool[l], ((0, 0), (POOL_HIST - POOL_STATE, 0), (0, 0)))
        x1_buf, route_buf, kn, vn, pn = _mixer_call(
            xs, cache_k[l].reshape(bs, BAND_PREV, ATT_WIDTH), cache_v[l].reshape(bs, BAND_PREV, ATT_WIDTH),
            pool_hist, wts, (x1_buf, route_buf),
            total_rows=total, tq=tq_s, pos0=PAST_LEN, row_off=rows_p, name="mixer_sample")

        dest, counts = _rank_call(route_buf)
        dest = dest.reshape(total // MOE_ROWS, 1, MOE_ROWS)
        tables = _block_tables(counts, n_blocks)
        xs_buf = _dispatch_call(x1_buf, dest, tables["zero_end"], n_blocks)
        x2s_buf = _moe_call(xs_buf, tables, wts, n_blocks)

        xp = _ple_call(x2s_buf, dest, p_prompt[l].reshape(rows_p, -1), wts, row_off=0,
                       name="ple_prompt").reshape(bp, tp, d_model)
        xs = _ple_call(x2s_buf, dest, p_sample[l].reshape(rows_s, -1), wts, row_off=rows_p,
                       name="ple_sample").reshape(bs, ts, d_model)

        tail_p = min(BAND_PREV, tp)
        if tail_p < BAND_PREV:
            kp = jnp.pad(kp, ((0, 0), (BAND_PREV - tail_p, 0), (0, 0)))
            vp = jnp.pad(vp, ((0, 0), (BAND_PREV - tail_p, 0), (0, 0)))
        outs[0].append(kp.reshape(bp, BAND_PREV, N_HEADS, HEAD_DIM))
        outs[1].append(vp.reshape(bp, BAND_PREV, N_HEADS, HEAD_DIM))
        outs[2].append(pp)
        outs[3].append(kn.reshape(bs, ts, N_HEADS, HEAD_DIM))
        outs[4].append(vn.reshape(bs, ts, N_HEADS, HEAD_DIM))
        outs[5].append(pn)
    return (xp, xs) + tuple(jnp.stack(o) for o in outs)
```

```python
import functools

import jax
import jax.numpy as jnp
from jax import lax
from jax.experimental import pallas as pl
from jax.experimental.pallas import tpu as pltpu

CHUNK = 64
N_HEADS = 8
HEAD_DIM = 64
ATT_WIDTH = N_HEADS * HEAD_DIM
POOL_WINDOWS = (2, 4, 8, 16)
POOL_GROUP = 128
POOL_WIDTH = POOL_GROUP * len(POOL_WINDOWS)
POOL_STATE = max(POOL_WINDOWS) - 1
N_PREV_CHUNKS = 8
BAND_PREV = N_PREV_CHUNKS * CHUNK
REL_CLIP = 256
N_GROUPS = 4
EXPERTS_PER_GROUP = 8
N_EXPERTS = N_GROUPS * EXPERTS_PER_GROUP
PAST_LEN = 2048
EPS = 1e-6
MASKED = -1e30
LOG2E = 1.4426950408889634

LANES = 128
SUBLANES = 8
VMEM_LIMIT_BYTES = 56 * 1024 * 1024

POOL_HIST = 16
PAIRS_PER_GROUP = EXPERTS_PER_GROUP * (EXPERTS_PER_GROUP - 1) // 2
N_CLASSES = N_GROUPS * PAIRS_PER_GROUP
MOE_ROWS = 256
RANK_ROWS = (1536, 1024, 512)
ROUTE_LANES = 128
ROW_TILE = MOE_ROWS * SUBLANES

_f32 = jnp.float32
_bf16 = jnp.bfloat16


def _dot(a, b):
    return jnp.dot(a, b, preferred_element_type=_f32)


def _rms_scale(x):
    return lax.rsqrt(jnp.mean(x * x, axis=-1, keepdims=True) + EPS)


def _split_dot(x, w_bf16):
    hi = x.astype(_bf16)
    lo = (x - hi.astype(_f32)).astype(_bf16)
    return _dot(hi, w_bf16) + _dot(lo, w_bf16)


def _load_row_tiles(ref, rows):
    return jnp.concatenate([ref[pl.ds(s, rows, stride=SUBLANES), :] for s in range(SUBLANES)], axis=-1)


def _store_row_tiles(ref, val, rows, row0=0):
    for s in range(SUBLANES):
        ref[pl.ds(row0 * SUBLANES + s, rows, stride=SUBLANES), :] = val[:, s * LANES:(s + 1) * LANES]


def _const_spec(shape):
    return pl.BlockSpec(shape, lambda *_: (0,) * len(shape), pipeline_mode=pl.Buffered(1))


def _route_rows(logits):
    lane = lax.broadcasted_iota(jnp.int32, (1, ROUTE_LANES), 1)
    lane_f = lane.astype(_f32)
    neg = jnp.float32(-jnp.inf)
    far = jnp.float32(ROUTE_LANES)
    cl = jnp.where(lane < N_GROUPS, logits, neg)
    cmax = jnp.max(cl, axis=-1, keepdims=True)
    grp = jnp.min(jnp.where(cl == cmax, lane_f, far), axis=-1, keepdims=True)
    fine_grp = ((lane - N_GROUPS) >> 3).astype(_f32)
    fl = jnp.where(fine_grp == grp, logits, neg)
    m1 = jnp.max(fl, axis=-1, keepdims=True)
    i1 = jnp.min(jnp.where(fl == m1, lane_f, far), axis=-1, keepdims=True)
    fl2 = jnp.where(lane_f == i1, neg, fl)
    m2 = jnp.max(fl2, axis=-1, keepdims=True)
    i2 = jnp.min(jnp.where(fl2 == m2, lane_f, far), axis=-1, keepdims=True)
    first_lane = N_GROUPS + EXPERTS_PER_GROUP * grp
    la = jnp.minimum(i1, i2) - first_lane
    lb = jnp.maximum(i1, i2) - first_lane
    return PAIRS_PER_GROUP * grp + la * (2 * EXPERTS_PER_GROUP - 1 - la) * 0.5 + (lb - la - 1.0)


def _mixer_kernel(x_ref, k0_ref, v0_ref, p0_ref, gmix_ref, win_ref, wpool_ref, pscale_ref, gq_ref, gk_ref,
                  bd_ref, bias_ref, wbp_ref, wba_ref, wout_ref, gffn_ref, wroute_ref, broute_ref,
                  *rest, tq, n_sub, n_steps, pos0):
    x1_ref, route_ref, kout_ref, vout_ref, pout_ref, kt_buf, v_buf, u_buf = rest[-8:]
    t = pl.program_id(1)
    band = BAND_PREV + tq
    step_rows = n_sub * tq
    d_model = x_ref.shape[-1]
    lane = lax.broadcasted_iota(jnp.int32, (1, LANES), 1)
    even = lane < HEAD_DIM
    col = lax.broadcasted_iota(jnp.int32, (1, band), 1)
    row = lax.broadcasted_iota(jnp.int32, (tq, 1), 0)

    @pl.when(t == 0)
    def _init_history():
        kt_buf[:, 0:BAND_PREV] = k0_ref[0].T.astype(_bf16)
        v_buf[0:BAND_PREV, :] = v0_ref[0].astype(_bf16)
        u_buf[0:POOL_HIST, :] = p0_ref[0]

    def pool_phase(j, st):
        r0 = j * tq
        x = x_ref[0, r0:r0 + tq, :]
        h = (x * _rms_scale(x) * gmix_ref[...]).astype(_bf16)
        u = _dot(h, win_ref[:, 0:POOL_WIDTH])
        u_buf[POOL_HIST + r0:POOL_HIST + r0 + tq, :] = u
        pos1 = pos0 + t * step_rows + r0 + row + 1
        pooled = []
        for g, w in enumerate(POOL_WINDOWS):
            sl = slice(g * POOL_GROUP, (g + 1) * POOL_GROUP)
            acc = u[:, sl]
            for back in range(1, w):
                acc = acc + u_buf[POOL_HIST + r0 - back:POOL_HIST + r0 - back + tq, sl]
            cnt = jnp.minimum(pos1, w).astype(_f32)
            d = acc / cnt - u[:, sl]
            pooled.append(_dot(d.astype(_bf16), wpool_ref[g]))
        st["x"], st["h"] = x, h
        st["a"] = (jnp.concatenate(pooled, axis=-1) * pscale_ref[...]).astype(_bf16)

    def qkv_phase(j, st):
        r0 = j * tq
        h = st["h"]
        q = _dot(h, win_ref[:, POOL_WIDTH:POOL_WIDTH + ATT_WIDTH])
        st["qn"] = (q * lax.rsqrt(_split_dot(q * q, bd_ref[...]) + EPS) * gq_ref[...]).astype(_bf16)
        k = _dot(h, win_ref[:, POOL_WIDTH + ATT_WIDTH:POOL_WIDTH + 2 * ATT_WIDTH])
        kn = k * lax.rsqrt(_split_dot(k * k, bd_ref[...]) + EPS) * gk_ref[...]
        v = _dot(h, win_ref[:, POOL_WIDTH + 2 * ATT_WIDTH:POOL_WIDTH + 3 * ATT_WIDTH])
        kout_ref[0, r0:r0 + tq, :] = kn
        vout_ref[0, r0:r0 + tq, :] = v
        kt_buf[:, BAND_PREV + r0:band + r0] = kn.T.astype(_bf16)
        v_buf[BAND_PREV + r0:band + r0, :] = v.astype(_bf16)

    def attention_phase(j, st):
        r0 = j * tq
        qn = st["qn"]
        started = col >= BAND_PREV - (pos0 + t * step_rows + r0)
        heads = []
        for p in range(N_HEADS // 2):
            sl = slice(p * LANES, (p + 1) * LANES)
            qp = qn[:, sl]
            ktp = kt_buf[sl, r0:r0 + band]
            vp = v_buf[r0:r0 + band, sl]
            acc = None
            inv = []
            for half in range(2):
                keep = even if half == 0 else jnp.logical_not(even)
                qh = jnp.where(keep, qp, jnp.zeros_like(qp))
                vh = jnp.where(keep, vp, jnp.zeros_like(vp))
                s = _dot(qh, ktp) + bias_ref[2 * p + half]
                s = jnp.where(started, s, MASKED)
                e = jnp.exp2(s - jnp.max(s, axis=-1, keepdims=True))
                inv.append(1.0 / jnp.sum(e, axis=-1, keepdims=True))
                part = _dot(e.astype(_bf16), vh)
                acc = part if acc is None else acc + part
            heads.append(acc * jnp.where(even, inv[0], inv[1]))
        st["o"] = jnp.concatenate(heads, axis=-1).astype(_bf16)

    def merge_phase(j, st):
        r0 = j * tq
        x, h = st["x"], st["h"]
        gate_off = POOL_WIDTH + 3 * ATT_WIDTH
        ga = _dot(h, win_ref[:, gate_off:gate_off + d_model])
        gb = _dot(h, win_ref[:, gate_off + d_model:gate_off + 2 * d_model])
        m = (jax.nn.sigmoid(ga) * _dot(st["a"], wbp_ref[...])
             + jax.nn.sigmoid(gb) * _dot(st["o"], wba_ref[...]))
        x1 = x + _dot(m.astype(_bf16), wout_ref[...])
        _store_row_tiles(x1_ref, x1, tq, r0)
        h2 = x1 * _rms_scale(x1) * gffn_ref[...]
        h2_hi = h2.astype(_bf16)
        h2_lo = (h2 - h2_hi.astype(_f32)).astype(_bf16)
        hi_both = _dot(h2_hi, wroute_ref[...])
        logits = (hi_both[:, 0:ROUTE_LANES] + hi_both[:, ROUTE_LANES:]
                  + _dot(h2_lo, wroute_ref[:, 0:ROUTE_LANES]) + broute_ref[...])
        route_ref[r0:r0 + tq, :] = jnp.broadcast_to(_route_rows(logits), (tq, ROUTE_LANES))

    states = [{} for _ in range(n_sub)]
    for phase in (pool_phase, qkv_phase, attention_phase, merge_phase):
        for j in range(n_sub):
            phase(j, states[j])

    pout_ref[0] = u_buf[POOL_HIST + step_rows - POOL_STATE:POOL_HIST + step_rows, :]
    if n_steps > 1:
        chunk = min(step_rows, BAND_PREV)
        for c in range(BAND_PREV // chunk):
            dst = slice(c * chunk, (c + 1) * chunk)
            src = slice(step_rows + c * chunk, step_rows + (c + 1) * chunk)
            kt_buf[:, dst] = kt_buf[:, src]
            v_buf[dst, :] = v_buf[src, :]
        u_buf[0:POOL_HIST, :] = u_buf[step_rows:step_rows + POOL_HIST, :]


def _mixer_call(x, k0, v0, p0, wts, shared, *, total_rows, tq, n_sub, pos0, row_off, name):
    bsz, t_len, d_model = x.shape
    step_rows = n_sub * tq
    n_steps = t_len // step_rows
    tail = min(BAND_PREV, t_len)
    assert t_len % step_rows == 0 and tail % step_rows == 0 and row_off % step_rows == 0
    tail_steps = tail // step_rows
    band = BAND_PREV + tq
    blk_off = row_off // step_rows

    def tail_map(b, t):
        return (b, jnp.maximum(t - (n_steps - tail_steps), 0), 0)

    in_specs = [
        pl.BlockSpec((1, step_rows, d_model), lambda b, t: (b, t, 0)),
        pl.BlockSpec((1, BAND_PREV, ATT_WIDTH), lambda b, t: (b, 0, 0)),
        pl.BlockSpec((1, BAND_PREV, ATT_WIDTH), lambda b, t: (b, 0, 0)),
        pl.BlockSpec((1, POOL_HIST, POOL_WIDTH), lambda b, t: (b, 0, 0)),
        _const_spec((1, d_model)),
        _const_spec(wts["w_in"].shape),
        _const_spec(wts["w_pool"].shape),
        _const_spec((1, POOL_WIDTH)),
        _const_spec((1, ATT_WIDTH)),
        _const_spec((1, ATT_WIDTH)),
        _const_spec((ATT_WIDTH, ATT_WIDTH)),
        _const_spec((N_HEADS, tq, band)),
        _const_spec(wts["w_br_pool"].shape),
        _const_spec(wts["w_br_att"].shape),
        _const_spec(wts["w_out"].shape),
        _const_spec((1, d_model)),
        _const_spec((d_model, 2 * ROUTE_LANES)),
        _const_spec((1, ROUTE_LANES)),
    ]
    operands = [x, k0, v0, p0, wts["g_mix"], wts["w_in"], wts["w_pool"], wts["pool_scale"], wts["g_q"],
                wts["g_k"], wts["bd"], wts["bias"][:, :tq, :band], wts["w_br_pool"], wts["w_br_att"],
                wts["w_out"], wts["g_ffn"], wts["w_route"], wts["b_route"]]
    aliases = {}
    if shared is not None:
        aliases = {len(operands): 0, len(operands) + 1: 1}
        in_specs += [pl.BlockSpec(memory_space=pl.ANY)] * 2
        operands += list(shared)
    out_specs = [
        pl.BlockSpec((step_rows * SUBLANES, LANES), lambda b, t: (blk_off + b * n_steps + t, 0)),
        pl.BlockSpec((step_rows, ROUTE_LANES), lambda b, t: (blk_off + b * n_steps + t, 0)),
        pl.BlockSpec((1, step_rows, ATT_WIDTH), tail_map),
        pl.BlockSpec((1, step_rows, ATT_WIDTH), tail_map),
        pl.BlockSpec((1, POOL_STATE, POOL_WIDTH), lambda b, t: (b, 0, 0)),
    ]
    out_shape = [
        jax.ShapeDtypeStruct((total_rows * SUBLANES, LANES), _f32),
        jax.ShapeDtypeStruct((total_rows, ROUTE_LANES), _f32),
        jax.ShapeDtypeStruct((bsz, tail, ATT_WIDTH), _f32),
        jax.ShapeDtypeStruct((bsz, tail, ATT_WIDTH), _f32),
        jax.ShapeDtypeStruct((bsz, POOL_STATE, POOL_WIDTH), _f32),
    ]
    kern = functools.partial(_mixer_kernel, tq=tq, n_sub=n_sub, n_steps=n_steps, pos0=pos0)
    return pl.pallas_call(
        kern,
        grid=(bsz, n_steps),
        in_specs=in_specs,
        out_specs=out_specs,
        out_shape=out_shape,
        scratch_shapes=[
            pltpu.VMEM((ATT_WIDTH, BAND_PREV + step_rows), _bf16),
            pltpu.VMEM((BAND_PREV + step_rows, ATT_WIDTH), _bf16),
            pltpu.VMEM((POOL_HIST + step_rows, POOL_WIDTH), _f32),
        ],
        input_output_aliases=aliases,
        compiler_params=pltpu.CompilerParams(
            dimension_semantics=("arbitrary", "arbitrary"), vmem_limit_bytes=VMEM_LIMIT_BYTES),
        name=name,
    )(*operands)


def _lane_cumsum(x):
    lane = lax.broadcasted_iota(jnp.int32, x.shape, 1)
    shift = 1
    while shift < LANES:
        x = x + jnp.where(lane >= shift, pltpu.roll(x, shift, axis=1), 0.0)
        shift *= 2
    return x


def _rank_kernel(route_ref, dest_ref, counts_ref, hist, base, before):
    ph = pl.program_id(0)
    i = pl.program_id(1)
    rows = route_ref.shape[0]
    lane = lax.broadcasted_iota(jnp.int32, (1, LANES), 1).astype(_f32)
    oh = jnp.where(route_ref[...] == lane, 1.0, 0.0)

    @pl.when(jnp.logical_and(ph == 0, i == 0))
    def _():
        hist[...] = jnp.zeros_like(hist)
        ri = lax.broadcasted_iota(jnp.int32, (rows, rows), 0)
        ci = lax.broadcasted_iota(jnp.int32, (rows, rows), 1)
        before[...] = jnp.where(ri < ci, 1.0, 0.0).astype(_bf16)

    @pl.when(ph == 0)
    def _():
        hist[...] += jnp.sum(oh, axis=0, keepdims=True)

    @pl.when(jnp.logical_and(ph == 1, i == 0))
    def _():
        cnt = hist[...]
        padded = jnp.floor((cnt + (MOE_ROWS - 1)) * (1.0 / MOE_ROWS)) * MOE_ROWS
        first = _lane_cumsum(padded) - padded
        ri = lax.broadcasted_iota(jnp.int32, (LANES, LANES), 0)
        ci = lax.broadcasted_iota(jnp.int32, (LANES, LANES), 1)
        first_col = jnp.sum(jnp.where(ri == ci, first[0:1, :], 0.0), axis=-1, keepdims=True)
        base[...] = jnp.broadcast_to(first_col, base.shape)
        counts_ref[...] = cnt

    @pl.when(ph == 1)
    def _():
        oh_t = oh.T
        earlier = _dot(oh_t.astype(_bf16), before[...])
        slot_row = jnp.sum(oh_t * (base[:, 0:1] + earlier), axis=0, keepdims=True)
        base[...] += jnp.sum(oh_t, axis=1, keepdims=True)
        dest_ref[0] = slot_row.astype(jnp.int32)


def _rank_call(route_buf):
    total = route_buf.shape[0]
    rows = next(r for r in RANK_ROWS if total % r == 0)
    n_tiles = total // rows
    return pl.pallas_call(
        _rank_kernel,
        grid=(2, n_tiles),
        in_specs=[pl.BlockSpec((rows, ROUTE_LANES), lambda ph, i: (i, 0))],
        out_specs=[
            pl.BlockSpec((1, 1, rows), lambda ph, i: (ph * i, 0, 0)),
            pl.BlockSpec((SUBLANES, LANES), lambda ph, i: (0, 0)),
        ],
        out_shape=[
            jax.ShapeDtypeStruct((n_tiles, 1, rows), jnp.int32),
            jax.ShapeDtypeStruct((SUBLANES, LANES), _f32),
        ],
        scratch_shapes=[pltpu.VMEM((SUBLANES, LANES), _f32), pltpu.VMEM((LANES, LANES), _f32),
                        pltpu.VMEM((rows, rows), _bf16)],
        compiler_params=pltpu.CompilerParams(dimension_semantics=("arbitrary", "arbitrary")),
        name="rank",
    )(route_buf)


def _block_tables(counts, n_blocks):
    cnt = counts[0, :N_CLASSES].astype(jnp.int32)
    padded = (cnt + MOE_ROWS - 1) // MOE_ROWS * MOE_ROWS
    pend = jnp.cumsum(padded)
    nb = pend[-1] // MOE_ROWS
    blk = jnp.minimum(jnp.arange(n_blocks, dtype=jnp.int32), nb - 1)
    blk_cls = jnp.sum(pend[None, :] <= (blk * MOE_ROWS)[:, None], axis=1, dtype=jnp.int32)
    blk_cls = jnp.minimum(blk_cls, N_CLASSES - 1)
    grp = blk_cls // PAIRS_PER_GROUP
    pair = blk_cls % PAIRS_PER_GROUP
    firsts = jnp.arange(1, EXPERTS_PER_GROUP, dtype=jnp.int32)
    pair_start = firsts * (2 * EXPERTS_PER_GROUP - 1 - firsts) // 2
    la = jnp.sum(pair[:, None] >= pair_start[None, :], axis=1, dtype=jnp.int32)
    lb = pair - la * (2 * EXPERTS_PER_GROUP - 1 - la) // 2 + la + 1
    zero_end = jnp.where(cnt > 0, pend, 0).astype(jnp.int32)
    return {
        "ea": grp * EXPERTS_PER_GROUP + la, "eb": grp * EXPERTS_PER_GROUP + lb,
        "nb": nb.reshape(1).astype(jnp.int32), "zero_end": zero_end,
    }


N_DISPATCH_BUFS = 3


def _dispatch_kernel(zero_end_ref, idx_ref, x1_hbm, xs_hbm, tbuf, zbuf, lsem, ssem, zsem):
    i = pl.program_id(0)
    n = pl.num_programs(0)
    slot = i % N_DISPATCH_BUFS

    def load_copy(tile, s):
        return pltpu.make_async_copy(x1_hbm.at[pl.ds(tile * ROW_TILE, ROW_TILE), :], tbuf.at[s], lsem.at[s])

    def wait_scatter(s):
        pltpu.make_async_copy(tbuf.at[s], xs_hbm.at[pl.ds(0, ROW_TILE), :], ssem.at[s]).wait()

    def zero_copy(c):
        end = zero_end_ref[c]
        return pltpu.make_async_copy(zbuf, xs_hbm.at[pl.ds((end - MOE_ROWS) * SUBLANES, ROW_TILE), :], zsem.at[0])

    @pl.when(i == 0)
    def _():
        zbuf[...] = jnp.zeros_like(zbuf)

        def start(c, carry):
            @pl.when(zero_end_ref[c] > 0)
            def _():
                zero_copy(c).start()
            return carry

        def wait(c, carry):
            @pl.when(zero_end_ref[c] > 0)
            def _():
                zero_copy(c).wait()
            return carry

        lax.fori_loop(0, N_CLASSES, start, 0)
        lax.fori_loop(0, N_CLASSES, wait, 0)
        load_copy(0, 0).start()

    @pl.when(i >= N_DISPATCH_BUFS - 1)
    def _():
        wait_scatter((i + 1) % N_DISPATCH_BUFS)

    @pl.when(i + 1 < n)
    def _():
        load_copy(i + 1, (i + 1) % N_DISPATCH_BUFS).start()

    load_copy(i, slot).wait()

    def body(r, carry):
        pltpu.make_async_copy(
            tbuf.at[slot, pl.ds(r * SUBLANES, SUBLANES), :],
            xs_hbm.at[pl.ds(idx_ref[0, 0, r] * SUBLANES, SUBLANES), :],
            ssem.at[slot]).start()
        return carry
    lax.fori_loop(0, MOE_ROWS, body, 0, unroll=8)

    @pl.when(i == n - 1)
    def _():
        @pl.when(n > 1)
        def _():
            wait_scatter((i + N_DISPATCH_BUFS - 1) % N_DISPATCH_BUFS)
        wait_scatter(slot)


def _dispatch_call(x1_buf, dest, zero_end, n_blocks):
    n_tiles = dest.shape[0]
    grid_spec = pltpu.PrefetchScalarGridSpec(
        num_scalar_prefetch=1,
        grid=(n_tiles,),
        in_specs=[
            pl.BlockSpec((1, 1, MOE_ROWS), lambda i, *_: (i, 0, 0), memory_space=pltpu.SMEM),
            pl.BlockSpec(memory_space=pl.ANY),
        ],
        out_specs=pl.BlockSpec(memory_space=pl.ANY),
        scratch_shapes=[
            pltpu.VMEM((N_DISPATCH_BUFS, ROW_TILE, LANES), _f32),
            pltpu.VMEM((ROW_TILE, LANES), _f32),
            pltpu.SemaphoreType.DMA((N_DISPATCH_BUFS,)),
            pltpu.SemaphoreType.DMA((N_DISPATCH_BUFS,)),
            pltpu.SemaphoreType.DMA((1,)),
        ],
    )
    return pl.pallas_call(
        _dispatch_kernel,
        grid_spec=grid_spec,
        out_shape=jax.ShapeDtypeStruct((n_blocks * ROW_TILE, LANES), _f32),
        compiler_params=pltpu.CompilerParams(dimension_semantics=("arbitrary",)),
        name="dispatch",
    )(zero_end, dest, x1_buf)


def _moe_kernel(ea_ref, eb_ref, nb_ref, xs_ref, gffn_ref, wr_ref, br_ref,
                wga_ref, wua_ref, wda_ref, wgb_ref, wub_ref, wdb_ref, out_ref):
    b = pl.program_id(0)

    @pl.when(b < nb_ref[0])
    def _block():
        x = _load_row_tiles(xs_ref, MOE_ROWS)
        h = (x * _rms_scale(x) * gffn_ref[...]).astype(_bf16)
        ea = ea_ref[b]
        eb = eb_ref[b]
        grp = ea // EXPERTS_PER_GROUP

        logits = _dot(h, wr_ref[...]) + br_ref[...]
        lane = lax.broadcasted_iota(jnp.int32, (1, ROUTE_LANES), 1)

        def pick(col):
            return jnp.sum(jnp.where(lane == col, logits, 0.0), axis=-1, keepdims=True)

        coarse = lane < N_GROUPS
        cmax = jnp.max(jnp.where(coarse, logits, -jnp.inf), axis=-1, keepdims=True)
        csum = jnp.sum(jnp.where(coarse, jnp.exp(logits - cmax), 0.0), axis=-1, keepdims=True)
        gp = jnp.exp(pick(grp) - cmax) / csum
        fa = pick(N_GROUPS + ea)
        fb = pick(N_GROUPS + eb)
        fmax = jnp.maximum(fa, fb)
        pa = jnp.exp(fa - fmax)
        pb = jnp.exp(fb - fmax)
        wa = gp * pa / (pa + pb)
        wb = gp * pb / (pa + pb)

        def expert(wg_ref, wu_ref, wd_ref):
            g = _dot(h, wg_ref[0])
            u = _dot(h, wu_ref[0])
            return _dot((jax.nn.silu(g) * u).astype(_bf16), wd_ref[0])

        y = wa * expert(wga_ref, wua_ref, wda_ref) + wb * expert(wgb_ref, wub_ref, wdb_ref)
        _store_row_tiles(out_ref, x + y, MOE_ROWS)


def _moe_call(xs_buf, tables, wts, n_blocks):
    d_model = wts["w_gate_e"].shape[1]
    d_exp = wts["w_gate_e"].shape[2]

    def used_block(b, ea, eb, nb):
        return (jnp.minimum(b, nb[0] - 1), 0)

    def up_spec(which):
        return pl.BlockSpec((1, d_model, d_exp), lambda b, ea, eb, nb: ((ea, eb)[which][b], 0, 0))

    def down_spec(which):
        return pl.BlockSpec((1, d_exp, d_model), lambda b, ea, eb, nb: ((ea, eb)[which][b], 0, 0))

    grid_spec = pltpu.PrefetchScalarGridSpec(
        num_scalar_prefetch=3,
        grid=(n_blocks,),
        in_specs=[
            pl.BlockSpec((ROW_TILE, LANES), used_block),
            pl.BlockSpec((1, d_model), lambda b, *_: (0, 0)),
            pl.BlockSpec((d_model, ROUTE_LANES), lambda b, *_: (0, 0)),
            pl.BlockSpec((1, ROUTE_LANES), lambda b, *_: (0, 0)),
            up_spec(0), up_spec(0), down_spec(0), up_spec(1), up_spec(1), down_spec(1),
        ],
        out_specs=pl.BlockSpec((ROW_TILE, LANES), used_block),
    )
    return pl.pallas_call(
        _moe_kernel,
        grid_spec=grid_spec,
        out_shape=jax.ShapeDtypeStruct(xs_buf.shape, _f32),
        compiler_params=pltpu.CompilerParams(
            dimension_semantics=("arbitrary",), vmem_limit_bytes=VMEM_LIMIT_BYTES),
        name="moe",
    )(tables["ea"], tables["eb"], tables["nb"], xs_buf, wts["g_ffn"], wts["w_route_hi"], wts["b_route"],
      wts["w_gate_e"], wts["w_up_e"], wts["w_down_e"], wts["w_gate_e"], wts["w_up_e"], wts["w_down_e"])


def _ple_kernel(idx_ref, idx_nxt_ref, p_ref, gple_ref, wgate_ref, wproj_ref, x2s_hbm, out_ref, buf0, buf1, gsem):
    i = pl.program_id(0)
    n = pl.num_programs(0)

    def row_copy(idx_ref, r, off, buf, s):
        return pltpu.make_async_copy(
            x2s_hbm.at[pl.ds(idx_ref[0, 0, off + r] * SUBLANES, SUBLANES), :],
            buf.at[pl.ds(r * SUBLANES, SUBLANES), :],
            gsem.at[s])

    def start_gather(idx_ref, off, buf, s):
        for r in range(MOE_ROWS):
            row_copy(idx_ref, r, off, buf, s).start()

    def wait_gather(buf, s):
        pltpu.make_async_copy(x2s_hbm.at[pl.ds(0, ROW_TILE), :], buf, gsem.at[s]).wait()

    def tile(buf, half):
        rows = pl.ds(half * MOE_ROWS, MOE_ROWS)
        x2 = _load_row_tiles(buf, MOE_ROWS)
        hn = (x2 * _rms_scale(x2) * gple_ref[...]).astype(_bf16)
        gate = jax.nn.sigmoid(_dot(hn, wgate_ref[...]))
        out_ref[rows, :] = x2 + _dot(p_ref[rows, :].astype(_bf16), wproj_ref[...]) * gate

    @pl.when(i == 0)
    def _():
        def body(r, carry):
            row_copy(idx_ref, r, 0, buf0, 0).start()
            return carry
        lax.fori_loop(0, MOE_ROWS, body, 0, unroll=8)

    start_gather(idx_ref, MOE_ROWS, buf1, 1)
    wait_gather(buf0, 0)
    tile(buf0, 0)
    start_gather(idx_nxt_ref, 0, buf0, 0)
    wait_gather(buf1, 1)
    tile(buf1, 1)

    @pl.when(i == n - 1)
    def _():
        wait_gather(buf0, 0)


def _ple_call(x2s_buf, dest, p, wts, *, row_off, name):
    n_rows, d_ple = p.shape
    d_model = wts["w_ple_gate"].shape[0]
    step_rows = 2 * MOE_ROWS
    n_steps = n_rows // step_rows
    blk_off = row_off // step_rows
    last = blk_off + n_steps - 1
    dest = dest.reshape(-1, 1, step_rows)
    idx_spec = functools.partial(pl.BlockSpec, (1, 1, step_rows), memory_space=pltpu.SMEM)
    return pl.pallas_call(
        _ple_kernel,
        grid=(n_steps,),
        in_specs=[
            idx_spec(index_map=lambda i: (blk_off + i, 0, 0)),
            idx_spec(index_map=lambda i: (jnp.minimum(blk_off + i + 1, last), 0, 0)),
            pl.BlockSpec((step_rows, d_ple), lambda i: (i, 0)),
            _const_spec((1, d_model)),
            _const_spec((d_model, d_model)),
            _const_spec((d_ple, d_model)),
            pl.BlockSpec(memory_space=pl.ANY),
        ],
        out_specs=pl.BlockSpec((step_rows, d_model), lambda i: (i, 0)),
        out_shape=jax.ShapeDtypeStruct((n_rows, d_model), _f32),
        scratch_shapes=[pltpu.VMEM((ROW_TILE, LANES), _f32), pltpu.VMEM((ROW_TILE, LANES), _f32),
                        pltpu.SemaphoreType.DMA((2,))],
        compiler_params=pltpu.CompilerParams(
            dimension_semantics=("arbitrary",), vmem_limit_bytes=VMEM_LIMIT_BYTES),
        name=name,
    )(dest, dest, p, wts["g_ple"], wts["w_ple_gate"], wts["w_ple_proj"], x2s_buf)


def _band_bias(table, tq):
    band = BAND_PREV + tq
    n_heads, n_rel = table.shape
    n_far = band - 1 - REL_CLIP
    length = band + tq
    n_near = length - n_far - n_rel
    tab = table.astype(_f32)
    f = jnp.concatenate([jnp.broadcast_to(tab[:, n_rel - 1:], (n_heads, n_far)), tab[:, ::-1],
                         jnp.broadcast_to(tab[:, :1], (n_heads, n_near))], axis=1)
    skew = jnp.tile(f, (1, tq))[:, :tq * (length - 1)].reshape(n_heads, tq, length - 1)
    bias = skew[:, :, tq - 1:tq - 1 + band]
    kc = lax.broadcasted_iota(jnp.int32, (tq, band), 1) // CHUNK
    qc = lax.broadcasted_iota(jnp.int32, (tq, band), 0) // CHUNK
    in_band = (kc >= qc) & (kc <= qc + N_PREV_CHUNKS)
    return jnp.where(in_band[None], bias, MASKED)


def _layer_weights(l, tq, g_mix, w_in, w_pool, pool_scale, g_q, g_k, rel_table, w_br_pool, w_br_att, w_out,
                   g_ffn, w_coarse, b_coarse, w_fine, b_fine, w_gate_e, w_up_e, w_down_e, g_ple, w_ple_gate,
                   w_ple_proj):
    d_model = w_in.shape[1]
    pad = ROUTE_LANES - N_GROUPS - N_EXPERTS
    head_of = jnp.arange(ATT_WIDTH, dtype=jnp.int32) // HEAD_DIM
    head_sum = jnp.where(head_of[:, None] == head_of[None, :], 1.0 / HEAD_DIM, 0.0)
    w_route = jnp.concatenate([w_coarse[l], w_fine[l], jnp.zeros((d_model, pad), _f32)], axis=1)
    w_route_hi = w_route.astype(_bf16)
    w_route_lo = (w_route - w_route_hi.astype(_f32)).astype(_bf16)
    b_route = jnp.concatenate([b_coarse[l], b_fine[l], jnp.zeros((pad,), _f32)])
    return {
        "g_mix": g_mix[l].reshape(1, d_model),
        "w_in": w_in[l].astype(_bf16),
        "w_pool": w_pool[l].astype(_bf16),
        "pool_scale": pool_scale[l].reshape(1, POOL_WIDTH),
        "g_q": (jnp.tile(g_q[l], N_HEADS) * (HEAD_DIM ** -0.5 * LOG2E)).reshape(1, ATT_WIDTH),
        "g_k": jnp.tile(g_k[l], N_HEADS).reshape(1, ATT_WIDTH),
        "bd": head_sum.astype(_bf16),
        "bias": _band_bias(rel_table[l] * LOG2E, tq),
        "w_br_pool": w_br_pool[l].astype(_bf16),
        "w_br_att": w_br_att[l].astype(_bf16),
        "w_out": w_out[l].astype(_bf16),
        "g_ffn": g_ffn[l].reshape(1, d_model),
        "w_route": jnp.concatenate([w_route_hi, w_route_lo], axis=1),
        "b_route": b_route.reshape(1, ROUTE_LANES),
        "w_route_hi": w_route_hi,
        "w_gate_e": w_gate_e[l].astype(_bf16),
        "w_up_e": w_up_e[l].astype(_bf16),
        "w_down_e": w_down_e[l].astype(_bf16),
        "g_ple": g_ple[l].reshape(1, d_model),
        "w_ple_gate": w_ple_gate[l].astype(_bf16),
        "w_ple_proj": w_ple_proj[l].astype(_bf16),
    }


def _pick_tile(t_len, want):
    tq = min(want, t_len)
    assert t_len % tq == 0 and tq % CHUNK == 0 and BAND_PREV % tq == 0
    return tq


def kernel(x_prompt, x_sample, cache_k, cache_v, state_pool, p_prompt, p_sample, g_mix, w_in, w_pool, pool_scale, g_q, g_k, rel_table, w_br_pool, w_br_att, w_out, g_ffn, w_coarse, b_coarse, w_fine, b_fine, w_gate_e, w_up_e, w_down_e, g_ple, w_ple_gate, w_ple_proj):
    depth = w_in.shape[0]
    bp, tp, d_model = x_prompt.shape
    bs, ts, _ = x_sample.shape
    assert d_model == SUBLANES * LANES
    tq_p = _pick_tile(tp, 256)
    tq_s = _pick_tile(ts, 256)
    sub_p = 2 if tp % (2 * tq_p) == 0 and BAND_PREV % (2 * tq_p) == 0 else 1
    rows_p, rows_s = bp * tp, bs * ts
    total = rows_p + rows_s
    assert rows_p % (2 * MOE_ROWS) == 0 and rows_s % (2 * MOE_ROWS) == 0 and total % RANK_ROWS[-1] == 0
    n_blocks = -(-(total + N_CLASSES * (MOE_ROWS - 1)) // MOE_ROWS)

    xp, xs = x_prompt, x_sample
    outs = [[] for _ in range(6)]
    for l in range(depth):
        wts = _layer_weights(l, max(tq_p, tq_s), g_mix, w_in, w_pool, pool_scale, g_q, g_k, rel_table, w_br_pool,
                             w_br_att, w_out, g_ffn, w_coarse, b_coarse, w_fine, b_fine, w_gate_e, w_up_e,
                             w_down_e, g_ple, w_ple_gate, w_ple_proj)
        zeros_kv = jnp.zeros((bp, BAND_PREV, ATT_WIDTH), _f32)
        zeros_pool = jnp.zeros((bp, POOL_HIST, POOL_WIDTH), _f32)
        x1_buf, route_buf, kp, vp, pp = _mixer_call(
            xp, zeros_kv, zeros_kv, zeros_pool, wts, None,
            total_rows=total, tq=tq_p, n_sub=sub_p, pos0=0, row_off=0, name="mixer_prompt")
        pool_hist = jnp.pad(state_pool[l], ((0, 0), (POOL_HIST - POOL_STATE, 0), (0, 0)))
        x1_buf, route_buf, kn, vn, pn = _mixer_call(
            xs, cache_k[l].reshape(bs, BAND_PREV, ATT_WIDTH), cache_v[l].reshape(bs, BAND_PREV, ATT_WIDTH),
            pool_hist, wts, (x1_buf, route_buf),
            total_rows=total, tq=tq_s, n_sub=1, pos0=PAST_LEN, row_off=rows_p, name="mixer_sample")

        dest, counts = _rank_call(route_buf)
        dest = dest.reshape(total // MOE_ROWS, 1, MOE_ROWS)
        tables = _block_tables(counts, n_blocks)
        xs_buf = _dispatch_call(x1_buf, dest, tables["zero_end"], n_blocks)
        x2s_buf = _moe_call(xs_buf, tables, wts, n_blocks)

        xp = _ple_call(x2s_buf, dest, p_prompt[l].reshape(rows_p, -1), wts, row_off=0,
                       name="ple_prompt").reshape(bp, tp, d_model)
        xs = _ple_call(x2s_buf, dest, p_sample[l].reshape(rows_s, -1), wts, row_off=rows_p,
                       name="ple_sample").reshape(bs, ts, d_model)

        tail_p = min(BAND_PREV, tp)
        if tail_p < BAND_PREV:
            kp = jnp.pad(kp, ((0, 0), (BAND_PREV - tail_p, 0), (0, 0)))
            vp = jnp.pad(vp, ((0, 0), (BAND_PREV - tail_p, 0), (0, 0)))
        outs[0].append(kp.reshape(bp, BAND_PREV, N_HEADS, HEAD_DIM))
        outs[1].append(vp.reshape(bp, BAND_PREV, N_HEADS, HEAD_DIM))
        outs[2].append(pp)
        outs[3].append(kn.reshape(bs, ts, N_HEADS, HEAD_DIM))
        outs[4].append(vn.reshape(bs, ts, N_HEADS, HEAD_DIM))
        outs[5].append(pn)
    return (xp, xs) + tuple(jnp.stack(o) for o in outs)
```

```python
import functools

import jax
import jax.numpy as jnp
from jax import lax
from jax.experimental import pallas as pl
from jax.experimental.pallas import tpu as pltpu

CHUNK = 64
N_HEADS = 8
HEAD_DIM = 64
ATT_WIDTH = N_HEADS * HEAD_DIM
POOL_WINDOWS = (2, 4, 8, 16)
POOL_GROUP = 128
POOL_WIDTH = POOL_GROUP * len(POOL_WINDOWS)
POOL_STATE = max(POOL_WINDOWS) - 1
N_PREV_CHUNKS = 8
BAND_PREV = N_PREV_CHUNKS * CHUNK
REL_CLIP = 256
N_GROUPS = 4
EXPERTS_PER_GROUP = 8
N_EXPERTS = N_GROUPS * EXPERTS_PER_GROUP
PAST_LEN = 2048
EPS = 1e-6
MASKED = -1e30
LOG2E = 1.4426950408889634

LANES = 128
SUBLANES = 8
VMEM_LIMIT_BYTES = 56 * 1024 * 1024

POOL_HIST = 16
POOL_PAD = SUBLANES
POOL_BASE = POOL_PAD + POOL_HIST
POOL_LEVELS = 3
ATT_ROWS = 256
PAIRS_PER_GROUP = EXPERTS_PER_GROUP * (EXPERTS_PER_GROUP - 1) // 2
N_CLASSES = N_GROUPS * PAIRS_PER_GROUP
MOE_ROWS = 256
RANK_ROWS = (1536, 1024, 512)
ROUTE_LANES = 128
ROW_TILE = MOE_ROWS * SUBLANES

_f32 = jnp.float32
_bf16 = jnp.bfloat16


def _dot(a, b):
    return jnp.dot(a, b, preferred_element_type=_f32)


def _rms_scale(x):
    return lax.rsqrt(jnp.mean(x * x, axis=-1, keepdims=True) + EPS)


def _split_dot(x, w_bf16):
    hi = x.astype(_bf16)
    lo = (x - hi.astype(_f32)).astype(_bf16)
    return _dot(hi, w_bf16) + _dot(lo, w_bf16)


def _load_row_tiles(ref, rows, row0=0):
    return jnp.concatenate(
        [ref[pl.ds(row0 * SUBLANES + s, rows, stride=SUBLANES), :] for s in range(SUBLANES)], axis=-1)


def _store_row_tiles(ref, val, rows, row0=0):
    for s in range(SUBLANES):
        ref[pl.ds(row0 * SUBLANES + s, rows, stride=SUBLANES), :] = val[:, s * LANES:(s + 1) * LANES]


def _const_spec(shape):
    return pl.BlockSpec(shape, lambda *_: (0,) * len(shape), pipeline_mode=pl.Buffered(1))


def _route_rows(logits):
    lane = lax.broadcasted_iota(jnp.int32, (1, ROUTE_LANES), 1)
    lane_f = lane.astype(_f32)
    neg = jnp.float32(-jnp.inf)
    far = jnp.float32(ROUTE_LANES)
    cl = jnp.where(lane < N_GROUPS, logits, neg)
    cmax = jnp.max(cl, axis=-1, keepdims=True)
    grp = jnp.min(jnp.where(cl == cmax, lane_f, far), axis=-1, keepdims=True)
    fine_grp = ((lane - N_GROUPS) >> 3).astype(_f32)
    fl = jnp.where(fine_grp == grp, logits, neg)
    m1 = jnp.max(fl, axis=-1, keepdims=True)
    i1 = jnp.min(jnp.where(fl == m1, lane_f, far), axis=-1, keepdims=True)
    fl2 = jnp.where(lane_f == i1, neg, fl)
    m2 = jnp.max(fl2, axis=-1, keepdims=True)
    i2 = jnp.min(jnp.where(fl2 == m2, lane_f, far), axis=-1, keepdims=True)
    first_lane = N_GROUPS + EXPERTS_PER_GROUP * grp
    la = jnp.minimum(i1, i2) - first_lane
    lb = jnp.maximum(i1, i2) - first_lane
    return PAIRS_PER_GROUP * grp + la * (2 * EXPERTS_PER_GROUP - 1 - la) * 0.5 + (lb - la - 1.0)


def _mixer_kernel(x_ref, k0_ref, v0_ref, p0_ref, gmix_ref, win_ref, wpool_ref, pscale_ref, gq_ref, gk_ref,
                  bd_ref, bias_ref, wbp_ref, wba_ref, wout_ref, gffn_ref, wroute_ref, broute_ref,
                  *rest, tq, n_sub, n_steps, pos0):
    x1_ref, route_ref, kout_ref, vout_ref, pout_ref, kt_buf, v_buf, u_buf, s_buf = rest[-9:]
    t = pl.program_id(1)
    att_rows = bias_ref.shape[1]
    band = BAND_PREV + att_rows
    step_rows = n_sub * tq
    d_model = x_ref.shape[-1]
    lane = lax.broadcasted_iota(jnp.int32, (1, LANES), 1)
    even = lane < HEAD_DIM
    col = lax.broadcasted_iota(jnp.int32, (1, band), 1)
    row = lax.broadcasted_iota(jnp.int32, (tq, 1), 0)

    @pl.when(t == 0)
    def _init_history():
        kt_buf[:, 0:BAND_PREV] = k0_ref[0].T.astype(_bf16)
        v_buf[0:BAND_PREV, :] = v0_ref[0].astype(_bf16)
        u_buf[0:POOL_PAD, :] = jnp.zeros((POOL_PAD, POOL_WIDTH), _f32)
        u_buf[POOL_PAD:POOL_BASE, :] = p0_ref[0]
        s_buf[:, :, 0:POOL_PAD, :] = jnp.zeros(s_buf.shape[:2] + (POOL_PAD, POOL_WIDTH), _f32)

    def pool_phase(j, st):
        r0 = j * tq
        x = x_ref[0, r0:r0 + tq, :]
        h = (x * _rms_scale(x) * gmix_ref[...]).astype(_bf16)
        u = _dot(h, win_ref[:, 0:POOL_WIDTH])
        u_buf[POOL_BASE + r0:POOL_BASE + r0 + tq, :] = u
        pos1 = pos0 + t * step_rows + r0 + row + 1
        w0 = POOL_PAD + r0
        span = POOL_HIST + tq
        pooled = []
        for g, w in enumerate(POOL_WINDOWS):
            sl = slice(g * POOL_GROUP, (g + 1) * POOL_GROUP)
            acc = u_buf[w0:w0 + span, sl] + u_buf[w0 - 1:w0 - 1 + span, sl]
            shift = 2
            while shift < w:
                level = s_buf.at[j, shift.bit_length() - 2]
                level[POOL_PAD:POOL_PAD + span, sl] = acc
                acc = acc + level[POOL_PAD - shift:POOL_PAD - shift + span, sl]
                shift *= 2
            cnt = jnp.minimum(pos1, w).astype(_f32)
            d = acc[POOL_HIST:, :] / cnt - u[:, sl]
            pooled.append(_dot(d.astype(_bf16), wpool_ref[g]))
        st["x"], st["h"] = x, h
        st["a"] = (jnp.concatenate(pooled, axis=-1) * pscale_ref[...]).astype(_bf16)

    def qkv_phase(j, st):
        r0 = j * tq
        h = st["h"]
        q = _dot(h, win_ref[:, POOL_WIDTH:POOL_WIDTH + ATT_WIDTH])
        st["qn"] = (q * lax.rsqrt(_split_dot(q * q, bd_ref[...]) + EPS) * gq_ref[...]).astype(_bf16)
        k = _dot(h, win_ref[:, POOL_WIDTH + ATT_WIDTH:POOL_WIDTH + 2 * ATT_WIDTH])
        kn = k * lax.rsqrt(_split_dot(k * k, bd_ref[...]) + EPS) * gk_ref[...]
        v = _dot(h, win_ref[:, POOL_WIDTH + 2 * ATT_WIDTH:POOL_WIDTH + 3 * ATT_WIDTH])
        kout_ref[0, r0:r0 + tq, :] = kn
        vout_ref[0, r0:r0 + tq, :] = v
        kt_buf[:, BAND_PREV + r0:BAND_PREV + r0 + tq] = kn.T.astype(_bf16)
        v_buf[BAND_PREV + r0:BAND_PREV + r0 + tq, :] = v.astype(_bf16)

    def attention_phase(j, st):
        qn = st["qn"]
        heads = []
        for p in range(N_HEADS // 2):
            sl = slice(p * LANES, (p + 1) * LANES)
            blocks = []
            for r in range(tq // att_rows):
                k0 = j * tq + r * att_rows
                qp = qn[r * att_rows:(r + 1) * att_rows, sl]
                ktp = kt_buf[sl, k0:k0 + band]
                vp = v_buf[k0:k0 + band, sl]
                started = col >= BAND_PREV - (pos0 + t * step_rows + k0)
                acc = None
                inv = []
                for half in range(2):
                    keep = even if half == 0 else jnp.logical_not(even)
                    qh = jnp.where(keep, qp, jnp.zeros_like(qp))
                    vh = jnp.where(keep, vp, jnp.zeros_like(vp))
                    s = _dot(qh, ktp) + bias_ref[2 * p + half]
                    s = jnp.where(started, s, MASKED)
                    e = jnp.exp2(s - jnp.max(s, axis=-1, keepdims=True))
                    inv.append(1.0 / jnp.sum(e, axis=-1, keepdims=True))
                    part = _dot(e.astype(_bf16), vh)
                    acc = part if acc is None else acc + part
                blocks.append(acc * jnp.where(even, inv[0], inv[1]))
            heads.append(jnp.concatenate(blocks, axis=0))
        st["o"] = jnp.concatenate(heads, axis=-1).astype(_bf16)

    def merge_phase(j, st):
        r0 = j * tq
        x, h = st["x"], st["h"]
        gate_off = POOL_WIDTH + 3 * ATT_WIDTH
        ga = _dot(h, win_ref[:, gate_off:gate_off + d_model])
        gb = _dot(h, win_ref[:, gate_off + d_model:gate_off + 2 * d_model])
        m = (jax.nn.sigmoid(ga) * _dot(st["a"], wbp_ref[...])
             + jax.nn.sigmoid(gb) * _dot(st["o"], wba_ref[...]))
        x1 = x + _dot(m.astype(_bf16), wout_ref[...])
        _store_row_tiles(x1_ref, x1, tq, r0)
        h2 = x1 * _rms_scale(x1) * gffn_ref[...]
        h2_hi = h2.astype(_bf16)
        h2_lo = (h2 - h2_hi.astype(_f32)).astype(_bf16)
        hi_both = _dot(h2_hi, wroute_ref[...])
        logits = (hi_both[:, 0:ROUTE_LANES] + hi_both[:, ROUTE_LANES:]
                  + _dot(h2_lo, wroute_ref[:, 0:ROUTE_LANES]) + broute_ref[...])
        route_ref[r0:r0 + tq, :] = jnp.broadcast_to(_route_rows(logits), (tq, ROUTE_LANES))

    states = [{} for _ in range(n_sub)]
    for phase in (pool_phase, qkv_phase, attention_phase, merge_phase):
        for j in range(n_sub):
            phase(j, states[j])

    pout_ref[0] = u_buf[POOL_BASE + step_rows - POOL_STATE:POOL_BASE + step_rows, :]
    if n_steps > 1:
        chunk = min(step_rows, BAND_PREV)
        for c in range(BAND_PREV // chunk):
            dst = slice(c * chunk, (c + 1) * chunk)
            src = slice(step_rows + c * chunk, step_rows + (c + 1) * chunk)
            kt_buf[:, dst] = kt_buf[:, src]
            v_buf[dst, :] = v_buf[src, :]
        u_buf[POOL_PAD:POOL_BASE, :] = u_buf[POOL_PAD + step_rows:POOL_BASE + step_rows, :]


def _mixer_call(x, k0, v0, p0, wts, shared, *, total_rows, tq, n_sub, pos0, row_off, name):
    bsz, t_len, d_model = x.shape
    step_rows = n_sub * tq
    n_steps = t_len // step_rows
    tail = min(BAND_PREV, t_len)
    assert t_len % step_rows == 0 and tail % step_rows == 0 and row_off % step_rows == 0
    tail_steps = tail // step_rows
    att_rows = min(ATT_ROWS, tq)
    assert tq % att_rows == 0
    band = BAND_PREV + att_rows
    blk_off = row_off // step_rows

    def tail_map(b, t):
        return (b, jnp.maximum(t - (n_steps - tail_steps), 0), 0)

    in_specs = [
        pl.BlockSpec((1, step_rows, d_model), lambda b, t: (b, t, 0)),
        pl.BlockSpec((1, BAND_PREV, ATT_WIDTH), lambda b, t: (b, 0, 0), pipeline_mode=pl.Buffered(1)),
        pl.BlockSpec((1, BAND_PREV, ATT_WIDTH), lambda b, t: (b, 0, 0), pipeline_mode=pl.Buffered(1)),
        pl.BlockSpec((1, POOL_HIST, POOL_WIDTH), lambda b, t: (b, 0, 0)),
        _const_spec((1, d_model)),
        _const_spec(wts["w_in"].shape),
        _const_spec(wts["w_pool"].shape),
        _const_spec((1, POOL_WIDTH)),
        _const_spec((1, ATT_WIDTH)),
        _const_spec((1, ATT_WIDTH)),
        _const_spec((ATT_WIDTH, ATT_WIDTH)),
        _const_spec((N_HEADS, att_rows, band)),
        _const_spec(wts["w_br_pool"].shape),
        _const_spec(wts["w_br_att"].shape),
        _const_spec(wts["w_out"].shape),
        _const_spec((1, d_model)),
        _const_spec((d_model, 2 * ROUTE_LANES)),
        _const_spec((1, ROUTE_LANES)),
    ]
    operands = [x, k0, v0, p0, wts["g_mix"], wts["w_in"], wts["w_pool"], wts["pool_scale"], wts["g_q"],
                wts["g_k"], wts["bd"], wts["bias"][:, :att_rows, :band], wts["w_br_pool"], wts["w_br_att"],
                wts["w_out"], wts["g_ffn"], wts["w_route"], wts["b_route"]]
    aliases = {}
    if shared is not None:
        aliases = {len(operands): 0, len(operands) + 1: 1}
        in_specs += [pl.BlockSpec(memory_space=pl.ANY)] * 2
        operands += list(shared)
    out_specs = [
        pl.BlockSpec((step_rows * SUBLANES, LANES), lambda b, t: (blk_off + b * n_steps + t, 0)),
        pl.BlockSpec((step_rows, ROUTE_LANES), lambda b, t: (blk_off + b * n_steps + t, 0)),
        pl.BlockSpec((1, step_rows, ATT_WIDTH), tail_map),
        pl.BlockSpec((1, step_rows, ATT_WIDTH), tail_map),
        pl.BlockSpec((1, POOL_STATE, POOL_WIDTH), lambda b, t: (b, 0, 0)),
    ]
    out_shape = [
        jax.ShapeDtypeStruct((total_rows * SUBLANES, LANES), _f32),
        jax.ShapeDtypeStruct((total_rows, ROUTE_LANES), _f32),
        jax.ShapeDtypeStruct((bsz, tail, ATT_WIDTH), _f32),
        jax.ShapeDtypeStruct((bsz, tail, ATT_WIDTH), _f32),
        jax.ShapeDtypeStruct((bsz, POOL_STATE, POOL_WIDTH), _f32),
    ]
    kern = functools.partial(_mixer_kernel, tq=tq, n_sub=n_sub, n_steps=n_steps, pos0=pos0)
    return pl.pallas_call(
        kern,
        grid=(bsz, n_steps),
        in_specs=in_specs,
        out_specs=out_specs,
        out_shape=out_shape,
        scratch_shapes=[
            pltpu.VMEM((ATT_WIDTH, BAND_PREV + step_rows), _bf16),
            pltpu.VMEM((BAND_PREV + step_rows, ATT_WIDTH), _bf16),
            pltpu.VMEM((POOL_BASE + step_rows, POOL_WIDTH), _f32),
            pltpu.VMEM((n_sub, POOL_LEVELS, POOL_BASE + tq, POOL_WIDTH), _f32),
        ],
        input_output_aliases=aliases,
        compiler_params=pltpu.CompilerParams(
            dimension_semantics=("arbitrary", "arbitrary"), vmem_limit_bytes=VMEM_LIMIT_BYTES),
        name=name,
    )(*operands)


def _lane_cumsum(x):
    lane = lax.broadcasted_iota(jnp.int32, x.shape, 1)
    shift = 1
    while shift < LANES:
        x = x + jnp.where(lane >= shift, pltpu.roll(x, shift, axis=1), 0.0)
        shift *= 2
    return x


def _rank_kernel(route_ref, dest_ref, counts_ref, hist, base, before):
    ph = pl.program_id(0)
    i = pl.program_id(1)
    rows = route_ref.shape[0]
    lane = lax.broadcasted_iota(jnp.int32, (1, LANES), 1).astype(_f32)
    oh = jnp.where(route_ref[...] == lane, 1.0, 0.0)

    @pl.when(jnp.logical_and(ph == 0, i == 0))
    def _():
        hist[...] = jnp.zeros_like(hist)
        ri = lax.broadcasted_iota(jnp.int32, (rows, rows), 0)
        ci = lax.broadcasted_iota(jnp.int32, (rows, rows), 1)
        before[...] = jnp.where(ri < ci, 1.0, 0.0).astype(_bf16)

    @pl.when(ph == 0)
    def _():
        hist[...] += jnp.sum(oh, axis=0, keepdims=True)

    @pl.when(jnp.logical_and(ph == 1, i == 0))
    def _():
        cnt = hist[...]
        padded = jnp.floor((cnt + (MOE_ROWS - 1)) * (1.0 / MOE_ROWS)) * MOE_ROWS
        first = _lane_cumsum(padded) - padded
        ri = lax.broadcasted_iota(jnp.int32, (LANES, LANES), 0)
        ci = lax.broadcasted_iota(jnp.int32, (LANES, LANES), 1)
        first_col = jnp.sum(jnp.where(ri == ci, first[0:1, :], 0.0), axis=-1, keepdims=True)
        base[...] = jnp.broadcast_to(first_col, base.shape)
        counts_ref[...] = cnt

    @pl.when(ph == 1)
    def _():
        oh_t = oh.T
        earlier = _dot(oh_t.astype(_bf16), before[...])
        slot_row = jnp.sum(oh_t * (base[:, 0:1] + earlier), axis=0, keepdims=True)
        base[...] += jnp.sum(oh_t, axis=1, keepdims=True)
        dest_ref[0] = slot_row.astype(jnp.int32)


def _rank_call(route_buf):
    total = route_buf.shape[0]
    rows = next(r for r in RANK_ROWS if total % r == 0)
    n_tiles = total // rows
    return pl.pallas_call(
        _rank_kernel,
        grid=(2, n_tiles),
        in_specs=[pl.BlockSpec((rows, ROUTE_LANES), lambda ph, i: (i, 0))],
        out_specs=[
            pl.BlockSpec((1, 1, rows), lambda ph, i: (ph * i, 0, 0)),
            pl.BlockSpec((SUBLANES, LANES), lambda ph, i: (0, 0)),
        ],
        out_shape=[
            jax.ShapeDtypeStruct((n_tiles, 1, rows), jnp.int32),
            jax.ShapeDtypeStruct((SUBLANES, LANES), _f32),
        ],
        scratch_shapes=[pltpu.VMEM((SUBLANES, LANES), _f32), pltpu.VMEM((LANES, LANES), _f32),
                        pltpu.VMEM((rows, rows), _bf16)],
        compiler_params=pltpu.CompilerParams(dimension_semantics=("arbitrary", "arbitrary")),
        name="rank",
    )(route_buf)


def _block_tables(counts, n_blocks):
    cnt = counts[0, :N_CLASSES].astype(jnp.int32)
    padded = (cnt + MOE_ROWS - 1) // MOE_ROWS * MOE_ROWS
    pend = jnp.cumsum(padded)
    nb = pend[-1] // MOE_ROWS
    blk = jnp.minimum(jnp.arange(n_blocks, dtype=jnp.int32), nb - 1)
    blk_cls = jnp.sum(pend[None, :] <= (blk * MOE_ROWS)[:, None], axis=1, dtype=jnp.int32)
    blk_cls = jnp.minimum(blk_cls, N_CLASSES - 1)
    grp = blk_cls // PAIRS_PER_GROUP
    pair = blk_cls % PAIRS_PER_GROUP
    firsts = jnp.arange(1, EXPERTS_PER_GROUP, dtype=jnp.int32)
    pair_start = firsts * (2 * EXPERTS_PER_GROUP - 1 - firsts) // 2
    la = jnp.sum(pair[:, None] >= pair_start[None, :], axis=1, dtype=jnp.int32)
    lb = pair - la * (2 * EXPERTS_PER_GROUP - 1 - la) // 2 + la + 1
    zero_end = jnp.where(cnt > 0, pend, 0).astype(jnp.int32)
    return {
        "ea": grp * EXPERTS_PER_GROUP + la, "eb": grp * EXPERTS_PER_GROUP + lb,
        "nb": nb.reshape(1).astype(jnp.int32), "zero_end": zero_end,
    }


N_DISPATCH_BUFS = 3


def _dispatch_kernel(zero_end_ref, idx_ref, x1_hbm, xs_hbm, tbuf, zbuf, lsem, ssem, zsem):
    i = pl.program_id(0)
    n = pl.num_programs(0)
    slot = i % N_DISPATCH_BUFS

    def load_copy(tile, s):
        return pltpu.make_async_copy(x1_hbm.at[pl.ds(tile * ROW_TILE, ROW_TILE), :], tbuf.at[s], lsem.at[s])

    def wait_scatter(s):
        pltpu.make_async_copy(tbuf.at[s], xs_hbm.at[pl.ds(0, ROW_TILE), :], ssem.at[s]).wait()

    def zero_copy(c):
        end = zero_end_ref[c]
        return pltpu.make_async_copy(zbuf, xs_hbm.at[pl.ds((end - MOE_ROWS) * SUBLANES, ROW_TILE), :], zsem.at[0])

    @pl.when(i == 0)
    def _():
        zbuf[...] = jnp.zeros_like(zbuf)

        def start(c, carry):
            @pl.when(zero_end_ref[c] > 0)
            def _():
                zero_copy(c).start()
            return carry

        def wait(c, carry):
            @pl.when(zero_end_ref[c] > 0)
            def _():
                zero_copy(c).wait()
            return carry

        lax.fori_loop(0, N_CLASSES, start, 0)
        lax.fori_loop(0, N_CLASSES, wait, 0)
        load_copy(0, 0).start()

    @pl.when(i >= N_DISPATCH_BUFS - 1)
    def _():
        wait_scatter((i + 1) % N_DISPATCH_BUFS)

    @pl.when(i + 1 < n)
    def _():
        load_copy(i + 1, (i + 1) % N_DISPATCH_BUFS).start()

    load_copy(i, slot).wait()

    def body(r, carry):
        pltpu.make_async_copy(
            tbuf.at[slot, pl.ds(r * SUBLANES, SUBLANES), :],
            xs_hbm.at[pl.ds(idx_ref[0, 0, r] * SUBLANES, SUBLANES), :],
            ssem.at[slot]).start()
        return carry
    lax.fori_loop(0, MOE_ROWS, body, 0, unroll=8)

    @pl.when(i == n - 1)
    def _():
        @pl.when(n > 1)
        def _():
            wait_scatter((i + N_DISPATCH_BUFS - 1) % N_DISPATCH_BUFS)
        wait_scatter(slot)


def _dispatch_call(x1_buf, dest, zero_end, n_blocks):
    n_tiles = dest.shape[0]
    grid_spec = pltpu.PrefetchScalarGridSpec(
        num_scalar_prefetch=1,
        grid=(n_tiles,),
        in_specs=[
            pl.BlockSpec((1, 1, MOE_ROWS), lambda i, *_: (i, 0, 0), memory_space=pltpu.SMEM),
            pl.BlockSpec(memory_space=pl.ANY),
        ],
        out_specs=pl.BlockSpec(memory_space=pl.ANY),
        scratch_shapes=[
            pltpu.VMEM((N_DISPATCH_BUFS, ROW_TILE, LANES), _f32),
            pltpu.VMEM((ROW_TILE, LANES), _f32),
            pltpu.SemaphoreType.DMA((N_DISPATCH_BUFS,)),
            pltpu.SemaphoreType.DMA((N_DISPATCH_BUFS,)),
            pltpu.SemaphoreType.DMA((1,)),
        ],
    )
    return pl.pallas_call(
        _dispatch_kernel,
        grid_spec=grid_spec,
        out_shape=jax.ShapeDtypeStruct((n_blocks * ROW_TILE, LANES), _f32),
        compiler_params=pltpu.CompilerParams(dimension_semantics=("arbitrary",)),
        name="dispatch",
    )(zero_end, dest, x1_buf)


MOE_PAIR = 2


def _moe_kernel(ea_ref, eb_ref, nb_ref, xs_ref, gffn_ref, wr_ref, br_ref, *rest):
    out_ref = rest[-1]
    step = pl.program_id(0)
    nb = nb_ref[0]
    lane = lax.broadcasted_iota(jnp.int32, (1, ROUTE_LANES), 1)

    def load_phase(k, st):
        x = _load_row_tiles(xs_ref, MOE_ROWS, k * MOE_ROWS)
        st["x"] = x
        st["h"] = (x * _rms_scale(x) * gffn_ref[...]).astype(_bf16)

    def weight_phase(k, st):
        ea = ea_ref[step * MOE_PAIR + k]
        eb = eb_ref[step * MOE_PAIR + k]
        grp = ea // EXPERTS_PER_GROUP
        logits = _dot(st["h"], wr_ref[...]) + br_ref[...]

        def pick(col):
            return jnp.sum(jnp.where(lane == col, logits, 0.0), axis=-1, keepdims=True)

        coarse = lane < N_GROUPS
        cmax = jnp.max(jnp.where(coarse, logits, -jnp.inf), axis=-1, keepdims=True)
        csum = jnp.sum(jnp.where(coarse, jnp.exp(logits - cmax), 0.0), axis=-1, keepdims=True)
        gp = jnp.exp(pick(grp) - cmax) / csum
        fa = pick(N_GROUPS + ea)
        fb = pick(N_GROUPS + eb)
        fmax = jnp.maximum(fa, fb)
        pa = jnp.exp(fa - fmax)
        pb = jnp.exp(fb - fmax)
        st["wa"] = gp * pa / (pa + pb)
        st["wb"] = gp * pb / (pa + pb)

    def expert_phase(k, st):
        wga, wua, wda, wgb, wub, wdb = rest[6 * k:6 * k + 6]
        h = st["h"]

        def expert(wg_ref, wu_ref, wd_ref):
            g = _dot(h, wg_ref[0])
            u = _dot(h, wu_ref[0])
            return _dot((jax.nn.silu(g) * u).astype(_bf16), wd_ref[0])

        y = st["wa"] * expert(wga, wua, wda) + st["wb"] * expert(wgb, wub, wdb)
        _store_row_tiles(out_ref, st["x"] + y, MOE_ROWS, k * MOE_ROWS)

    def run(n_live):
        states = [{} for _ in range(n_live)]
        for phase in (load_phase, weight_phase, expert_phase):
            for k in range(n_live):
                phase(k, states[k])

    first = step * MOE_PAIR
    for n_live in range(MOE_PAIR, 0, -1):
        cond = (first + n_live <= nb) if n_live == MOE_PAIR else (first + n_live == nb)
        pl.when(cond)(functools.partial(run, n_live))


def _moe_call(xs_buf, tables, wts, n_blocks):
    d_model = wts["w_gate_e"].shape[1]
    d_exp = wts["w_gate_e"].shape[2]
    assert n_blocks % MOE_PAIR == 0

    def used_step(s, ea, eb, nb):
        return (jnp.minimum(s, (nb[0] - 1) // MOE_PAIR), 0)

    def expert_of(which, k):
        def index_map(s, ea, eb, nb):
            blk = jnp.minimum(s * MOE_PAIR + k, nb[0] - 1)
            return ((ea, eb)[which][blk], 0, 0)
        return index_map

    weight_specs = []
    weight_args = []
    for k in range(MOE_PAIR):
        for which in range(2):
            weight_specs += [pl.BlockSpec((1, d_model, d_exp), expert_of(which, k)),
                             pl.BlockSpec((1, d_model, d_exp), expert_of(which, k)),
                             pl.BlockSpec((1, d_exp, d_model), expert_of(which, k))]
            weight_args += [wts["w_gate_e"], wts["w_up_e"], wts["w_down_e"]]

    grid_spec = pltpu.PrefetchScalarGridSpec(
        num_scalar_prefetch=3,
        grid=(n_blocks // MOE_PAIR,),
        in_specs=[
            pl.BlockSpec((MOE_PAIR * ROW_TILE, LANES), used_step),
            pl.BlockSpec((1, d_model), lambda s, *_: (0, 0)),
            pl.BlockSpec((d_model, ROUTE_LANES), lambda s, *_: (0, 0)),
            pl.BlockSpec((1, ROUTE_LANES), lambda s, *_: (0, 0)),
        ] + weight_specs,
        out_specs=pl.BlockSpec((MOE_PAIR * ROW_TILE, LANES), used_step),
    )
    return pl.pallas_call(
        _moe_kernel,
        grid_spec=grid_spec,
        out_shape=jax.ShapeDtypeStruct(xs_buf.shape, _f32),
        compiler_params=pltpu.CompilerParams(
            dimension_semantics=("arbitrary",), vmem_limit_bytes=VMEM_LIMIT_BYTES),
        name="moe",
    )(tables["ea"], tables["eb"], tables["nb"], xs_buf, wts["g_ffn"], wts["w_route_hi"], wts["b_route"],
      *weight_args)


def _ple_kernel(idx_ref, idx_nxt_ref, p_ref, gple_ref, wgate_ref, wproj_ref, x2s_hbm, out_ref, buf0, buf1, gsem):
    i = pl.program_id(0)
    n = pl.num_programs(0)

    def row_copy(idx_ref, r, off, buf, s):
        return pltpu.make_async_copy(
            x2s_hbm.at[pl.ds(idx_ref[0, 0, off + r] * SUBLANES, SUBLANES), :],
            buf.at[pl.ds(r * SUBLANES, SUBLANES), :],
            gsem.at[s])

    def start_gather(idx_ref, off, buf, s):
        for r in range(MOE_ROWS):
            row_copy(idx_ref, r, off, buf, s).start()

    def wait_gather(buf, s):
        pltpu.make_async_copy(x2s_hbm.at[pl.ds(0, ROW_TILE), :], buf, gsem.at[s]).wait()

    def tile(buf, half):
        rows = pl.ds(half * MOE_ROWS, MOE_ROWS)
        x2 = _load_row_tiles(buf, MOE_ROWS)
        hn = (x2 * _rms_scale(x2) * gple_ref[...]).astype(_bf16)
        gate = jax.nn.sigmoid(_dot(hn, wgate_ref[...]))
        out_ref[rows, :] = x2 + _dot(p_ref[rows, :].astype(_bf16), wproj_ref[...]) * gate

    @pl.when(i == 0)
    def _():
        def body(r, carry):
            row_copy(idx_ref, r, 0, buf0, 0).start()
            return carry
        lax.fori_loop(0, MOE_ROWS, body, 0, unroll=8)

    start_gather(idx_ref, MOE_ROWS, buf1, 1)
    wait_gather(buf0, 0)
    tile(buf0, 0)
    start_gather(idx_nxt_ref, 0, buf0, 0)
    wait_gather(buf1, 1)
    tile(buf1, 1)

    @pl.when(i == n - 1)
    def _():
        wait_gather(buf0, 0)


def _ple_call(x2s_buf, dest, p, wts, *, row_off, name):
    n_rows, d_ple = p.shape
    d_model = wts["w_ple_gate"].shape[0]
    step_rows = 2 * MOE_ROWS
    n_steps = n_rows // step_rows
    blk_off = row_off // step_rows
    last = blk_off + n_steps - 1
    dest = dest.reshape(-1, 1, step_rows)
    idx_spec = functools.partial(pl.BlockSpec, (1, 1, step_rows), memory_space=pltpu.SMEM)
    return pl.pallas_call(
        _ple_kernel,
        grid=(n_steps,),
        in_specs=[
            idx_spec(index_map=lambda i: (blk_off + i, 0, 0)),
            idx_spec(index_map=lambda i: (jnp.minimum(blk_off + i + 1, last), 0, 0)),
            pl.BlockSpec((step_rows, d_ple), lambda i: (i, 0)),
            _const_spec((1, d_model)),
            _const_spec((d_model, d_model)),
            _const_spec((d_ple, d_model)),
            pl.BlockSpec(memory_space=pl.ANY),
        ],
        out_specs=pl.BlockSpec((step_rows, d_model), lambda i: (i, 0)),
        out_shape=jax.ShapeDtypeStruct((n_rows, d_model), _f32),
        scratch_shapes=[pltpu.VMEM((ROW_TILE, LANES), _f32), pltpu.VMEM((ROW_TILE, LANES), _f32),
                        pltpu.SemaphoreType.DMA((2,))],
        compiler_params=pltpu.CompilerParams(
            dimension_semantics=("arbitrary",), vmem_limit_bytes=VMEM_LIMIT_BYTES),
        name=name,
    )(dest, dest, p, wts["g_ple"], wts["w_ple_gate"], wts["w_ple_proj"], x2s_buf)


def _band_bias(table, tq):
    band = BAND_PREV + tq
    n_heads, n_rel = table.shape
    n_far = band - 1 - REL_CLIP
    length = band + tq
    n_near = length - n_far - n_rel
    tab = table.astype(_f32)
    f = jnp.concatenate([jnp.broadcast_to(tab[:, n_rel - 1:], (n_heads, n_far)), tab[:, ::-1],
                         jnp.broadcast_to(tab[:, :1], (n_heads, n_near))], axis=1)
    skew = jnp.tile(f, (1, tq))[:, :tq * (length - 1)].reshape(n_heads, tq, length - 1)
    bias = skew[:, :, tq - 1:tq - 1 + band]
    kc = lax.broadcasted_iota(jnp.int32, (tq, band), 1) // CHUNK
    qc = lax.broadcasted_iota(jnp.int32, (tq, band), 0) // CHUNK
    in_band = (kc >= qc) & (kc <= qc + N_PREV_CHUNKS)
    return jnp.where(in_band[None], bias, MASKED)


def _layer_weights(l, tq, g_mix, w_in, w_pool, pool_scale, g_q, g_k, rel_table, w_br_pool, w_br_att, w_out,
                   g_ffn, w_coarse, b_coarse, w_fine, b_fine, w_gate_e, w_up_e, w_down_e, g_ple, w_ple_gate,
                   w_ple_proj):
    d_model = w_in.shape[1]
    pad = ROUTE_LANES - N_GROUPS - N_EXPERTS
    head_of = jnp.arange(ATT_WIDTH, dtype=jnp.int32) // HEAD_DIM
    head_sum = jnp.where(head_of[:, None] == head_of[None, :], 1.0 / HEAD_DIM, 0.0)
    w_route = jnp.concatenate([w_coarse[l], w_fine[l], jnp.zeros((d_model, pad), _f32)], axis=1)
    w_route_hi = w_route.astype(_bf16)
    w_route_lo = (w_route - w_route_hi.astype(_f32)).astype(_bf16)
    b_route = jnp.concatenate([b_coarse[l], b_fine[l], jnp.zeros((pad,), _f32)])
    return {
        "g_mix": g_mix[l].reshape(1, d_model),
        "w_in": w_in[l].astype(_bf16),
        "w_pool": w_pool[l].astype(_bf16),
        "pool_scale": pool_scale[l].reshape(1, POOL_WIDTH),
        "g_q": (jnp.tile(g_q[l], N_HEADS) * (HEAD_DIM ** -0.5 * LOG2E)).reshape(1, ATT_WIDTH),
        "g_k": jnp.tile(g_k[l], N_HEADS).reshape(1, ATT_WIDTH),
        "bd": head_sum.astype(_bf16),
        "bias": _band_bias(rel_table[l] * LOG2E, tq),
        "w_br_pool": w_br_pool[l].astype(_bf16),
        "w_br_att": w_br_att[l].astype(_bf16),
        "w_out": w_out[l].astype(_bf16),
        "g_ffn": g_ffn[l].reshape(1, d_model),
        "w_route": jnp.concatenate([w_route_hi, w_route_lo], axis=1),
        "b_route": b_route.reshape(1, ROUTE_LANES),
        "w_route_hi": w_route_hi,
        "w_gate_e": w_gate_e[l].astype(_bf16),
        "w_up_e": w_up_e[l].astype(_bf16),
        "w_down_e": w_down_e[l].astype(_bf16),
        "g_ple": g_ple[l].reshape(1, d_model),
        "w_ple_gate": w_ple_gate[l].astype(_bf16),
        "w_ple_proj": w_ple_proj[l].astype(_bf16),
    }


def _pick_tile(t_len, want):
    tq = min(want, t_len)
    assert t_len % tq == 0 and tq % CHUNK == 0 and BAND_PREV % tq == 0
    return tq


def kernel(x_prompt, x_sample, cache_k, cache_v, state_pool, p_prompt, p_sample, g_mix, w_in, w_pool, pool_scale, g_q, g_k, rel_table, w_br_pool, w_br_att, w_out, g_ffn, w_coarse, b_coarse, w_fine, b_fine, w_gate_e, w_up_e, w_down_e, g_ple, w_ple_gate, w_ple_proj):
    depth = w_in.shape[0]
    bp, tp, d_model = x_prompt.shape
    bs, ts, _ = x_sample.shape
    assert d_model == SUBLANES * LANES
    tq_p = _pick_tile(tp, 512)
    tq_s = _pick_tile(ts, 256)
    sub_p = 2 if tp % (2 * tq_p) == 0 and BAND_PREV % (2 * tq_p) == 0 else 1
    rows_p, rows_s = bp * tp, bs * ts
    total = rows_p + rows_s
    assert rows_p % (2 * MOE_ROWS) == 0 and rows_s % (2 * MOE_ROWS) == 0 and total % RANK_ROWS[-1] == 0
    n_blocks = -(-(total + N_CLASSES * (MOE_ROWS - 1)) // MOE_ROWS)
    n_blocks = -(-n_blocks // MOE_PAIR) * MOE_PAIR

    xp, xs = x_prompt, x_sample
    outs = [[] for _ in range(6)]
    for l in range(depth):
        wts = _layer_weights(l, min(ATT_ROWS, max(tq_p, tq_s)), g_mix, w_in, w_pool, pool_scale, g_q, g_k, rel_table, w_br_pool,
                             w_br_att, w_out, g_ffn, w_coarse, b_coarse, w_fine, b_fine, w_gate_e, w_up_e,
                             w_down_e, g_ple, w_ple_gate, w_ple_proj)
        zeros_kv = jnp.zeros((bp, BAND_PREV, ATT_WIDTH), _f32)
        zeros_pool = jnp.zeros((bp, POOL_HIST, POOL_WIDTH), _f32)
        x1_buf, route_buf, kp, vp, pp = _mixer_call(
            xp, zeros_kv, zeros_kv, zeros_pool, wts, None,
            total_rows=total, tq=tq_p, n_sub=sub_p, pos0=0, row_off=0, name="mixer_prompt")
        pool_hist = jnp.pad(state_pool[l], ((0, 0), (POOL_HIST - POOL_STATE, 0), (0, 0)))
        x1_buf, route_buf, kn, vn, pn = _mixer_call(
            xs, cache_k[l].reshape(bs, BAND_PREV, ATT_WIDTH), cache_v[l].reshape(bs, BAND_PREV, ATT_WIDTH),
            pool_hist, wts, (x1_buf, route_buf),
            total_rows=total, tq=tq_s, n_sub=1, pos0=PAST_LEN, row_off=rows_p, name="mixer_sample")

        dest, counts = _rank_call(route_buf)
        dest = dest.reshape(total // MOE_ROWS, 1, MOE_ROWS)
        tables = _block_tables(counts, n_blocks)
        xs_buf = _dispatch_call(x1_buf, dest, tables["zero_end"], n_blocks)
        x2s_buf = _moe_call(xs_buf, tables, wts, n_blocks)

        xp = _ple_call(x2s_buf, dest, p_prompt[l].reshape(rows_p, -1), wts, row_off=0,
                       name="ple_prompt").reshape(bp, tp, d_model)
        xs = _ple_call(x2s_buf, dest, p_sample[l].reshape(rows_s, -1), wts, row_off=rows_p,
                       name="ple_sample").reshape(bs, ts, d_model)

        tail_p = min(BAND_PREV, tp)
        if tail_p < BAND_PREV:
            kp = jnp.pad(kp, ((0, 0), (BAND_PREV - tail_p, 0), (0, 0)))
            vp = jnp.pad(vp, ((0, 0), (BAND_PREV - tail_p, 0), (0, 0)))
        outs[0].append(kp.reshape(bp, BAND_PREV, N_HEADS, HEAD_DIM))
        outs[1].append(vp.reshape(bp, BAND_PREV, N_HEADS, HEAD_DIM))
        outs[2].append(pp)
        outs[3].append(kn.reshape(bs, ts, N_HEADS, HEAD_DIM))
        outs[4].append(vn.reshape(bs, ts, N_HEADS, HEAD_DIM))
        outs[5].append(pn)
    return (xp, xs) + tuple(jnp.stack(o) for o in outs)
```

```python
import functools

import jax
import jax.numpy as jnp
from jax import lax
from jax.experimental import pallas as pl
from jax.experimental.pallas import tpu as pltpu

CHUNK = 64
N_HEADS = 8
HEAD_DIM = 64
ATT_WIDTH = N_HEADS * HEAD_DIM
POOL_WINDOWS = (2, 4, 8, 16)
POOL_GROUP = 128
POOL_WIDTH = POOL_GROUP * len(POOL_WINDOWS)
POOL_STATE = max(POOL_WINDOWS) - 1
N_PREV_CHUNKS = 8
BAND_PREV = N_PREV_CHUNKS * CHUNK
REL_CLIP = 256
N_GROUPS = 4
EXPERTS_PER_GROUP = 8
N_EXPERTS = N_GROUPS * EXPERTS_PER_GROUP
PAST_LEN = 2048
EPS = 1e-6
MASKED = -1e30
LOG2E = 1.4426950408889634

LANES = 128
SUBLANES = 8
VMEM_LIMIT_BYTES = 56 * 1024 * 1024

POOL_HIST = 16
POOL_PAD = SUBLANES
POOL_BASE = POOL_PAD + POOL_HIST
POOL_LEVELS = 3
ATT_ROWS = 256
PAIRS_PER_GROUP = EXPERTS_PER_GROUP * (EXPERTS_PER_GROUP - 1) // 2
N_CLASSES = N_GROUPS * PAIRS_PER_GROUP
MOE_ROWS = 256
RANK_ROWS = (1536, 1024, 512)
ROUTE_LANES = 128
ROW_TILE = MOE_ROWS * SUBLANES

_f32 = jnp.float32
_bf16 = jnp.bfloat16


def _dot(a, b):
    return jnp.dot(a, b, preferred_element_type=_f32)


def _rms_scale(x):
    return lax.rsqrt(jnp.mean(x * x, axis=-1, keepdims=True) + EPS)


def _load_row_tiles(ref, rows, row0=0):
    return jnp.concatenate(
        [ref[pl.ds(row0 * SUBLANES + s, rows, stride=SUBLANES), :] for s in range(SUBLANES)], axis=-1)


def _store_row_tiles(ref, val, rows, row0=0):
    for s in range(SUBLANES):
        ref[pl.ds(row0 * SUBLANES + s, rows, stride=SUBLANES), :] = val[:, s * LANES:(s + 1) * LANES]


def _const_spec(shape):
    return pl.BlockSpec(shape, lambda *_: (0,) * len(shape), pipeline_mode=pl.Buffered(1))


def _route_rows(logits):
    lane = lax.broadcasted_iota(jnp.int32, (1, ROUTE_LANES), 1)
    lane_f = lane.astype(_f32)
    neg = jnp.float32(-jnp.inf)
    far = jnp.float32(ROUTE_LANES)
    cl = jnp.where(lane < N_GROUPS, logits, neg)
    cmax = jnp.max(cl, axis=-1, keepdims=True)
    grp = jnp.min(jnp.where(cl == cmax, lane_f, far), axis=-1, keepdims=True)
    fine_grp = ((lane - N_GROUPS) >> 3).astype(_f32)
    fl = jnp.where(fine_grp == grp, logits, neg)
    m1 = jnp.max(fl, axis=-1, keepdims=True)
    i1 = jnp.min(jnp.where(fl == m1, lane_f, far), axis=-1, keepdims=True)
    fl2 = jnp.where(lane_f == i1, neg, fl)
    m2 = jnp.max(fl2, axis=-1, keepdims=True)
    i2 = jnp.min(jnp.where(fl2 == m2, lane_f, far), axis=-1, keepdims=True)
    first_lane = N_GROUPS + EXPERTS_PER_GROUP * grp
    la = jnp.minimum(i1, i2) - first_lane
    lb = jnp.maximum(i1, i2) - first_lane
    return PAIRS_PER_GROUP * grp + la * (2 * EXPERTS_PER_GROUP - 1 - la) * 0.5 + (lb - la - 1.0)


def _mixer_kernel(x_ref, k0_ref, v0_ref, p0_ref, gmix_ref, win_ref, wpool_ref, pscale_ref, gq_ref, gk_ref,
                  bd_ref, bias_ref, wbp_ref, wba_ref, wout_ref, gffn_ref, wroute_ref, broute_ref,
                  *rest, tq, n_sub, n_steps, pos0):
    x1_ref, route_ref, kout_ref, vout_ref, pout_ref, cnt_ref, kt_buf, v_buf, u_buf, s_buf = rest[-10:]
    t = pl.program_id(1)

    @pl.when(jnp.logical_and(pl.program_id(0) == 0, t == 0))
    def _init_counts():
        cnt_ref[...] = jnp.zeros_like(cnt_ref)

    att_rows = bias_ref.shape[1]
    band = BAND_PREV + att_rows
    step_rows = n_sub * tq
    d_model = x_ref.shape[-1]
    lane = lax.broadcasted_iota(jnp.int32, (1, LANES), 1)
    even = lane < HEAD_DIM
    col = lax.broadcasted_iota(jnp.int32, (1, band), 1)
    row = lax.broadcasted_iota(jnp.int32, (tq, 1), 0)

    @pl.when(t == 0)
    def _init_history():
        kt_buf[:, 0:BAND_PREV] = k0_ref[0].T.astype(_bf16)
        v_buf[0:BAND_PREV, :] = v0_ref[0].astype(_bf16)
        u_buf[0:POOL_PAD, :] = jnp.zeros((POOL_PAD, POOL_WIDTH), _f32)
        u_buf[POOL_PAD:POOL_BASE, :] = p0_ref[0]
        s_buf[:, :, 0:POOL_PAD, :] = jnp.zeros(s_buf.shape[:2] + (POOL_PAD, POOL_WIDTH), _f32)

    def pool_phase(j, st):
        r0 = j * tq
        x = x_ref[0, r0:r0 + tq, :]
        h = (x * _rms_scale(x) * gmix_ref[...]).astype(_bf16)
        u = _dot(h, win_ref[:, 0:POOL_WIDTH])
        u_buf[POOL_BASE + r0:POOL_BASE + r0 + tq, :] = u
        pos1 = pos0 + t * step_rows + r0 + row + 1
        w0 = POOL_PAD + r0
        span = POOL_HIST + tq
        diffs = []
        for g, w in enumerate(POOL_WINDOWS):
            sl = slice(g * POOL_GROUP, (g + 1) * POOL_GROUP)
            acc = u_buf[w0:w0 + span, sl] + u_buf[w0 - 1:w0 - 1 + span, sl]
            shift = 2
            while shift < w:
                level = s_buf.at[j, shift.bit_length() - 2]
                level[POOL_PAD:POOL_PAD + span, sl] = acc
                acc = acc + level[POOL_PAD - shift:POOL_PAD - shift + span, sl]
                shift *= 2
            cnt = jnp.minimum(pos1, w).astype(_f32)
            diffs.append((acc[POOL_HIST:, :] / cnt - u[:, sl]).astype(_bf16))
        pooled = [_dot(jnp.concatenate(diffs[2 * i:2 * i + 2], axis=-1), wpool_ref[i])
                  for i in range(len(POOL_WINDOWS) // 2)]
        st["x"], st["h"] = x, h
        st["a"] = (jnp.concatenate(pooled, axis=-1) * pscale_ref[...]).astype(_bf16)

    def qkv_phase(j, st):
        r0 = j * tq
        h = st["h"]
        q = _dot(h, win_ref[:, POOL_WIDTH:POOL_WIDTH + ATT_WIDTH])
        st["qn"] = (q * lax.rsqrt(_dot((q * q).astype(_bf16), bd_ref[...]) + EPS) * gq_ref[...]).astype(_bf16)
        k = _dot(h, win_ref[:, POOL_WIDTH + ATT_WIDTH:POOL_WIDTH + 2 * ATT_WIDTH])
        kn = k * lax.rsqrt(_dot((k * k).astype(_bf16), bd_ref[...]) + EPS) * gk_ref[...]
        v = _dot(h, win_ref[:, POOL_WIDTH + 2 * ATT_WIDTH:POOL_WIDTH + 3 * ATT_WIDTH])
        kout_ref[0, r0:r0 + tq, :] = kn
        vout_ref[0, r0:r0 + tq, :] = v
        kt_buf[:, BAND_PREV + r0:BAND_PREV + r0 + tq] = kn.T.astype(_bf16)
        v_buf[BAND_PREV + r0:BAND_PREV + r0 + tq, :] = v.astype(_bf16)

    def attention_phase(j, st):
        qn = st["qn"]
        heads = []
        for p in range(N_HEADS // 2):
            sl = slice(p * LANES, (p + 1) * LANES)
            blocks = []
            for r in range(tq // att_rows):
                k0 = j * tq + r * att_rows
                qp = qn[r * att_rows:(r + 1) * att_rows, sl]
                ktp = kt_buf[sl, k0:k0 + band]
                vp = v_buf[k0:k0 + band, sl]
                started = col >= BAND_PREV - (pos0 + t * step_rows + k0)
                acc = None
                inv = []
                for half in range(2):
                    keep = even if half == 0 else jnp.logical_not(even)
                    qh = jnp.where(keep, qp, jnp.zeros_like(qp))
                    vh = jnp.where(keep, vp, jnp.zeros_like(vp))
                    s = _dot(qh, ktp) + bias_ref[2 * p + half]
                    s = jnp.where(started, s, MASKED)
                    e = jnp.exp2(s - jnp.max(s, axis=-1, keepdims=True))
                    inv.append(1.0 / jnp.sum(e, axis=-1, keepdims=True))
                    part = _dot(e.astype(_bf16), vh)
                    acc = part if acc is None else acc + part
                blocks.append(acc * jnp.where(even, inv[0], inv[1]))
            heads.append(jnp.concatenate(blocks, axis=0))
        st["o"] = jnp.concatenate(heads, axis=-1).astype(_bf16)

    def merge_phase(j, st):
        r0 = j * tq
        x, h = st["x"], st["h"]
        gate_off = POOL_WIDTH + 3 * ATT_WIDTH
        ga = _dot(h, win_ref[:, gate_off:gate_off + d_model])
        gb = _dot(h, win_ref[:, gate_off + d_model:gate_off + 2 * d_model])
        m = (jax.nn.sigmoid(ga) * _dot(st["a"], wbp_ref[...])
             + jax.nn.sigmoid(gb) * _dot(st["o"], wba_ref[...]))
        x1 = x + _dot(m.astype(_bf16), wout_ref[...])
        _store_row_tiles(x1_ref, x1, tq, r0)
        h2 = x1 * _rms_scale(x1) * gffn_ref[...]
        h2_hi = h2.astype(_bf16)
        h2_lo = (h2 - h2_hi.astype(_f32)).astype(_bf16)
        hi_both = _dot(h2_hi, wroute_ref[...])
        logits = (hi_both[:, 0:ROUTE_LANES] + hi_both[:, ROUTE_LANES:]
                  + _dot(h2_lo, wroute_ref[:, 0:ROUTE_LANES]) + broute_ref[...])
        cls = jnp.broadcast_to(_route_rows(logits), (tq, ROUTE_LANES))
        route_ref[r0:r0 + tq, :] = cls
        cnt_ref[...] += jnp.sum(jnp.where(cls == lane.astype(_f32), 1.0, 0.0), axis=0, keepdims=True)

    states = [{} for _ in range(n_sub)]
    for phase in (pool_phase, qkv_phase, attention_phase, merge_phase):
        for j in range(n_sub):
            phase(j, states[j])

    pout_ref[0] = u_buf[POOL_BASE + step_rows - POOL_STATE:POOL_BASE + step_rows, :]
    if n_steps > 1:
        chunk = min(step_rows, BAND_PREV)
        for c in range(BAND_PREV // chunk):
            dst = slice(c * chunk, (c + 1) * chunk)
            src = slice(step_rows + c * chunk, step_rows + (c + 1) * chunk)
            kt_buf[:, dst] = kt_buf[:, src]
            v_buf[dst, :] = v_buf[src, :]
        u_buf[POOL_PAD:POOL_BASE, :] = u_buf[POOL_PAD + step_rows:POOL_BASE + step_rows, :]


def _mixer_call(x, k0, v0, p0, wts, shared, *, total_rows, tq, n_sub, pos0, row_off, name):
    bsz, t_len, d_model = x.shape
    step_rows = n_sub * tq
    n_steps = t_len // step_rows
    tail = min(BAND_PREV, t_len)
    assert t_len % step_rows == 0 and tail % step_rows == 0 and row_off % step_rows == 0
    tail_steps = tail // step_rows
    att_rows = min(ATT_ROWS, tq)
    assert tq % att_rows == 0
    band = BAND_PREV + att_rows
    blk_off = row_off // step_rows

    def tail_map(b, t):
        return (b, jnp.maximum(t - (n_steps - tail_steps), 0), 0)

    in_specs = [
        pl.BlockSpec((1, step_rows, d_model), lambda b, t: (b, t, 0)),
        pl.BlockSpec((1, BAND_PREV, ATT_WIDTH), lambda b, t: (b, 0, 0), pipeline_mode=pl.Buffered(1)),
        pl.BlockSpec((1, BAND_PREV, ATT_WIDTH), lambda b, t: (b, 0, 0), pipeline_mode=pl.Buffered(1)),
        pl.BlockSpec((1, POOL_HIST, POOL_WIDTH), lambda b, t: (b, 0, 0)),
        _const_spec((1, d_model)),
        _const_spec(wts["w_in"].shape),
        _const_spec(wts["w_pool"].shape),
        _const_spec((1, POOL_WIDTH)),
        _const_spec((1, ATT_WIDTH)),
        _const_spec((1, ATT_WIDTH)),
        _const_spec((ATT_WIDTH, ATT_WIDTH)),
        _const_spec((N_HEADS, att_rows, band)),
        _const_spec(wts["w_br_pool"].shape),
        _const_spec(wts["w_br_att"].shape),
        _const_spec(wts["w_out"].shape),
        _const_spec((1, d_model)),
        _const_spec((d_model, 2 * ROUTE_LANES)),
        _const_spec((1, ROUTE_LANES)),
    ]
    operands = [x, k0, v0, p0, wts["g_mix"], wts["w_in"], wts["w_pool"], wts["pool_scale"], wts["g_q"],
                wts["g_k"], wts["bd"], wts["bias"][:, :att_rows, :band], wts["w_br_pool"], wts["w_br_att"],
                wts["w_out"], wts["g_ffn"], wts["w_route"], wts["b_route"]]
    aliases = {}
    if shared is not None:
        aliases = {len(operands): 0, len(operands) + 1: 1}
        in_specs += [pl.BlockSpec(memory_space=pl.ANY)] * 2
        operands += list(shared)
    out_specs = [
        pl.BlockSpec((step_rows * SUBLANES, LANES), lambda b, t: (blk_off + b * n_steps + t, 0)),
        pl.BlockSpec((step_rows, ROUTE_LANES), lambda b, t: (blk_off + b * n_steps + t, 0)),
        pl.BlockSpec((1, step_rows, ATT_WIDTH), tail_map),
        pl.BlockSpec((1, step_rows, ATT_WIDTH), tail_map),
        pl.BlockSpec((1, POOL_STATE, POOL_WIDTH), lambda b, t: (b, 0, 0)),
        pl.BlockSpec((SUBLANES, LANES), lambda b, t: (0, 0)),
    ]
    out_shape = [
        jax.ShapeDtypeStruct((total_rows * SUBLANES, LANES), _f32),
        jax.ShapeDtypeStruct((total_rows, ROUTE_LANES), _f32),
        jax.ShapeDtypeStruct((bsz, tail, ATT_WIDTH), _f32),
        jax.ShapeDtypeStruct((bsz, tail, ATT_WIDTH), _f32),
        jax.ShapeDtypeStruct((bsz, POOL_STATE, POOL_WIDTH), _f32),
        jax.ShapeDtypeStruct((SUBLANES, LANES), _f32),
    ]
    kern = functools.partial(_mixer_kernel, tq=tq, n_sub=n_sub, n_steps=n_steps, pos0=pos0)
    return pl.pallas_call(
        kern,
        grid=(bsz, n_steps),
        in_specs=in_specs,
        out_specs=out_specs,
        out_shape=out_shape,
        scratch_shapes=[
            pltpu.VMEM((ATT_WIDTH, BAND_PREV + step_rows), _bf16),
            pltpu.VMEM((BAND_PREV + step_rows, ATT_WIDTH), _bf16),
            pltpu.VMEM((POOL_BASE + step_rows, POOL_WIDTH), _f32),
            pltpu.VMEM((n_sub, POOL_LEVELS, POOL_BASE + tq, POOL_WIDTH), _f32),
        ],
        input_output_aliases=aliases,
        compiler_params=pltpu.CompilerParams(
            dimension_semantics=("arbitrary", "arbitrary"), vmem_limit_bytes=VMEM_LIMIT_BYTES),
        name=name,
    )(*operands)


def _lane_cumsum(x):
    lane = lax.broadcasted_iota(jnp.int32, x.shape, 1)
    shift = 1
    while shift < LANES:
        x = x + jnp.where(lane >= shift, pltpu.roll(x, shift, axis=1), 0.0)
        shift *= 2
    return x


def _rank_kernel(route_ref, counts_ref, dest_ref, base, before):
    i = pl.program_id(0)
    rows = route_ref.shape[0]
    lane = lax.broadcasted_iota(jnp.int32, (1, LANES), 1).astype(_f32)
    oh = jnp.where(route_ref[...] == lane, 1.0, 0.0)

    @pl.when(i == 0)
    def _():
        ri = lax.broadcasted_iota(jnp.int32, (rows, rows), 0)
        ci = lax.broadcasted_iota(jnp.int32, (rows, rows), 1)
        before[...] = jnp.where(ri < ci, 1.0, 0.0).astype(_bf16)
        cnt = counts_ref[...]
        padded = jnp.floor((cnt + (MOE_ROWS - 1)) * (1.0 / MOE_ROWS)) * MOE_ROWS
        first = _lane_cumsum(padded) - padded
        ri = lax.broadcasted_iota(jnp.int32, (LANES, LANES), 0)
        ci = lax.broadcasted_iota(jnp.int32, (LANES, LANES), 1)
        first_col = jnp.sum(jnp.where(ri == ci, first[0:1, :], 0.0), axis=-1, keepdims=True)
        base[...] = jnp.broadcast_to(first_col, base.shape)

    oh_t = oh.T
    earlier = _dot(oh_t.astype(_bf16), before[...])
    slot_row = jnp.sum(oh_t * (base[:, 0:1] + earlier), axis=0, keepdims=True)
    base[...] += jnp.sum(oh_t, axis=1, keepdims=True)
    dest_ref[0] = slot_row.astype(jnp.int32)


def _rank_call(route_buf, counts):
    total = route_buf.shape[0]
    rows = next(r for r in RANK_ROWS if total % r == 0)
    n_tiles = total // rows
    return pl.pallas_call(
        _rank_kernel,
        grid=(n_tiles,),
        in_specs=[pl.BlockSpec((rows, ROUTE_LANES), lambda i: (i, 0)),
                  pl.BlockSpec((SUBLANES, LANES), lambda i: (0, 0))],
        out_specs=pl.BlockSpec((1, 1, rows), lambda i: (i, 0, 0)),
        out_shape=jax.ShapeDtypeStruct((n_tiles, 1, rows), jnp.int32),
        scratch_shapes=[pltpu.VMEM((LANES, LANES), _f32), pltpu.VMEM((rows, rows), _bf16)],
        compiler_params=pltpu.CompilerParams(dimension_semantics=("arbitrary",)),
        name="rank",
    )(route_buf, counts)


def _block_tables(counts, n_blocks):
    cnt = counts[0, :N_CLASSES].astype(jnp.int32)
    padded = (cnt + MOE_ROWS - 1) // MOE_ROWS * MOE_ROWS
    pend = jnp.cumsum(padded)
    nb = pend[-1] // MOE_ROWS
    blk = jnp.minimum(jnp.arange(n_blocks, dtype=jnp.int32), nb - 1)
    blk_cls = jnp.sum(pend[None, :] <= (blk * MOE_ROWS)[:, None], axis=1, dtype=jnp.int32)
    blk_cls = jnp.minimum(blk_cls, N_CLASSES - 1)
    grp = blk_cls // PAIRS_PER_GROUP
    pair = blk_cls % PAIRS_PER_GROUP
    firsts = jnp.arange(1, EXPERTS_PER_GROUP, dtype=jnp.int32)
    pair_start = firsts * (2 * EXPERTS_PER_GROUP - 1 - firsts) // 2
    la = jnp.sum(pair[:, None] >= pair_start[None, :], axis=1, dtype=jnp.int32)
    lb = pair - la * (2 * EXPERTS_PER_GROUP - 1 - la) // 2 + la + 1
    zero_end = jnp.where(cnt > 0, pend, 0).astype(jnp.int32)
    return {
        "ea": grp * EXPERTS_PER_GROUP + la, "eb": grp * EXPERTS_PER_GROUP + lb,
        "nb": nb.reshape(1).astype(jnp.int32), "zero_end": zero_end,
    }


N_DISPATCH_BUFS = 3


def _dispatch_kernel(zero_end_ref, idx_ref, x1_hbm, xs_hbm, tbuf, zbuf, lsem, ssem, zsem):
    i = pl.program_id(0)
    n = pl.num_programs(0)
    slot = i % N_DISPATCH_BUFS

    def load_copy(tile, s):
        return pltpu.make_async_copy(x1_hbm.at[pl.ds(tile * ROW_TILE, ROW_TILE), :], tbuf.at[s], lsem.at[s])

    def wait_scatter(s):
        pltpu.make_async_copy(tbuf.at[s], xs_hbm.at[pl.ds(0, ROW_TILE), :], ssem.at[s]).wait()

    def zero_copy(c):
        end = zero_end_ref[c]
        return pltpu.make_async_copy(zbuf, xs_hbm.at[pl.ds((end - MOE_ROWS) * SUBLANES, ROW_TILE), :], zsem.at[0])

    @pl.when(i == 0)
    def _():
        zbuf[...] = jnp.zeros_like(zbuf)

        def start(c, carry):
            @pl.when(zero_end_ref[c] > 0)
            def _():
                zero_copy(c).start()
            return carry

        def wait(c, carry):
            @pl.when(zero_end_ref[c] > 0)
            def _():
                zero_copy(c).wait()
            return carry

        lax.fori_loop(0, N_CLASSES, start, 0)
        lax.fori_loop(0, N_CLASSES, wait, 0)
        load_copy(0, 0).start()

    @pl.when(i >= N_DISPATCH_BUFS - 1)
    def _():
        wait_scatter((i + 1) % N_DISPATCH_BUFS)

    @pl.when(i + 1 < n)
    def _():
        load_copy(i + 1, (i + 1) % N_DISPATCH_BUFS).start()

    load_copy(i, slot).wait()

    def body(r, carry):
        pltpu.make_async_copy(
            tbuf.at[slot, pl.ds(r * SUBLANES, SUBLANES), :],
            xs_hbm.at[pl.ds(idx_ref[0, 0, r] * SUBLANES, SUBLANES), :],
            ssem.at[slot]).start()
        return carry
    lax.fori_loop(0, MOE_ROWS, body, 0, unroll=8)

    @pl.when(i == n - 1)
    def _():
        @pl.when(n > 1)
        def _():
            wait_scatter((i + N_DISPATCH_BUFS - 1) % N_DISPATCH_BUFS)
        wait_scatter(slot)


def _dispatch_call(x1_buf, dest, zero_end, n_blocks):
    n_tiles = dest.shape[0]
    grid_spec = pltpu.PrefetchScalarGridSpec(
        num_scalar_prefetch=1,
        grid=(n_tiles,),
        in_specs=[
            pl.BlockSpec((1, 1, MOE_ROWS), lambda i, *_: (i, 0, 0), memory_space=pltpu.SMEM),
            pl.BlockSpec(memory_space=pl.ANY),
        ],
        out_specs=pl.BlockSpec(memory_space=pl.ANY),
        scratch_shapes=[
            pltpu.VMEM((N_DISPATCH_BUFS, ROW_TILE, LANES), _f32),
            pltpu.VMEM((ROW_TILE, LANES), _f32),
            pltpu.SemaphoreType.DMA((N_DISPATCH_BUFS,)),
            pltpu.SemaphoreType.DMA((N_DISPATCH_BUFS,)),
            pltpu.SemaphoreType.DMA((1,)),
        ],
    )
    return pl.pallas_call(
        _dispatch_kernel,
        grid_spec=grid_spec,
        out_shape=jax.ShapeDtypeStruct((n_blocks * ROW_TILE, LANES), _f32),
        compiler_params=pltpu.CompilerParams(dimension_semantics=("arbitrary",)),
        name="dispatch",
    )(zero_end, dest, x1_buf)


MOE_PAIR = 2


def _moe_kernel(ea_ref, eb_ref, nb_ref, xs_ref, gffn_ref, wr_ref, br_ref, *rest):
    out_ref = rest[-1]
    step = pl.program_id(0)
    nb = nb_ref[0]
    lane = lax.broadcasted_iota(jnp.int32, (1, ROUTE_LANES), 1)

    def load_phase(k, st):
        x = _load_row_tiles(xs_ref, MOE_ROWS, k * MOE_ROWS)
        st["x"] = x
        st["h"] = (x * _rms_scale(x) * gffn_ref[...]).astype(_bf16)

    def weight_phase(k, st):
        ea = ea_ref[step * MOE_PAIR + k]
        eb = eb_ref[step * MOE_PAIR + k]
        grp = ea // EXPERTS_PER_GROUP
        logits = _dot(st["h"], wr_ref[...]) + br_ref[...]

        def pick(col):
            return jnp.sum(jnp.where(lane == col, logits, 0.0), axis=-1, keepdims=True)

        coarse = lane < N_GROUPS
        cmax = jnp.max(jnp.where(coarse, logits, -jnp.inf), axis=-1, keepdims=True)
        csum = jnp.sum(jnp.where(coarse, jnp.exp(logits - cmax), 0.0), axis=-1, keepdims=True)
        gp = jnp.exp(pick(grp) - cmax) / csum
        fa = pick(N_GROUPS + ea)
        fb = pick(N_GROUPS + eb)
        fmax = jnp.maximum(fa, fb)
        pa = jnp.exp(fa - fmax)
        pb = jnp.exp(fb - fmax)
        st["wa"] = gp * pa / (pa + pb)
        st["wb"] = gp * pb / (pa + pb)

    def expert_phase(k, st):
        wga, wua, wda, wgb, wub, wdb = rest[6 * k:6 * k + 6]
        h = st["h"]

        def expert(wg_ref, wu_ref, wd_ref):
            g = _dot(h, wg_ref[0])
            u = _dot(h, wu_ref[0])
            return _dot((jax.nn.silu(g) * u).astype(_bf16), wd_ref[0])

        y = st["wa"] * expert(wga, wua, wda) + st["wb"] * expert(wgb, wub, wdb)
        _store_row_tiles(out_ref, st["x"] + y, MOE_ROWS, k * MOE_ROWS)

    def run(n_live):
        states = [{} for _ in range(n_live)]
        for phase in (load_phase, weight_phase, expert_phase):
            for k in range(n_live):
                phase(k, states[k])

    first = step * MOE_PAIR
    for n_live in range(MOE_PAIR, 0, -1):
        cond = (first + n_live <= nb) if n_live == MOE_PAIR else (first + n_live == nb)
        pl.when(cond)(functools.partial(run, n_live))


def _moe_call(xs_buf, tables, wts, n_blocks):
    d_model = wts["w_gate_e"].shape[1]
    d_exp = wts["w_gate_e"].shape[2]
    assert n_blocks % MOE_PAIR == 0

    def used_step(s, ea, eb, nb):
        return (jnp.minimum(s, (nb[0] - 1) // MOE_PAIR), 0)

    def expert_of(which, k):
        def index_map(s, ea, eb, nb):
            blk = jnp.minimum(s * MOE_PAIR + k, nb[0] - 1)
            return ((ea, eb)[which][blk], 0, 0)
        return index_map

    weight_specs = []
    weight_args = []
    for k in range(MOE_PAIR):
        for which in range(2):
            weight_specs += [pl.BlockSpec((1, d_model, d_exp), expert_of(which, k)),
                             pl.BlockSpec((1, d_model, d_exp), expert_of(which, k)),
                             pl.BlockSpec((1, d_exp, d_model), expert_of(which, k))]
            weight_args += [wts["w_gate_e"], wts["w_up_e"], wts["w_down_e"]]

    grid_spec = pltpu.PrefetchScalarGridSpec(
        num_scalar_prefetch=3,
        grid=(n_blocks // MOE_PAIR,),
        in_specs=[
            pl.BlockSpec((MOE_PAIR * ROW_TILE, LANES), used_step),
            pl.BlockSpec((1, d_model), lambda s, *_: (0, 0)),
            pl.BlockSpec((d_model, ROUTE_LANES), lambda s, *_: (0, 0)),
            pl.BlockSpec((1, ROUTE_LANES), lambda s, *_: (0, 0)),
        ] + weight_specs,
        out_specs=pl.BlockSpec((MOE_PAIR * ROW_TILE, LANES), used_step),
    )
    return pl.pallas_call(
        _moe_kernel,
        grid_spec=grid_spec,
        out_shape=jax.ShapeDtypeStruct(xs_buf.shape, _f32),
        compiler_params=pltpu.CompilerParams(
            dimension_semantics=("arbitrary",), vmem_limit_bytes=VMEM_LIMIT_BYTES),
        name="moe",
    )(tables["ea"], tables["eb"], tables["nb"], xs_buf, wts["g_ffn"], wts["w_route_hi"], wts["b_route"],
      *weight_args)


def _ple_kernel(idx_ref, idx_nxt_ref, p_ref, gple_ref, wgate_ref, wproj_ref, x2s_hbm, out_ref, buf0, buf1, gsem):
    i = pl.program_id(0)
    n = pl.num_programs(0)

    def row_copy(idx_ref, r, off, buf, s):
        return pltpu.make_async_copy(
            x2s_hbm.at[pl.ds(idx_ref[0, 0, off + r] * SUBLANES, SUBLANES), :],
            buf.at[pl.ds(r * SUBLANES, SUBLANES), :],
            gsem.at[s])

    def start_gather(idx_ref, off, buf, s):
        for r in range(MOE_ROWS):
            row_copy(idx_ref, r, off, buf, s).start()

    def wait_gather(buf, s):
        pltpu.make_async_copy(x2s_hbm.at[pl.ds(0, ROW_TILE), :], buf, gsem.at[s]).wait()

    def tile(buf, half):
        rows = pl.ds(half * MOE_ROWS, MOE_ROWS)
        x2 = _load_row_tiles(buf, MOE_ROWS)
        hn = (x2 * _rms_scale(x2) * gple_ref[...]).astype(_bf16)
        gate = jax.nn.sigmoid(_dot(hn, wgate_ref[...]))
        out_ref[rows, :] = x2 + _dot(p_ref[rows, :].astype(_bf16), wproj_ref[...]) * gate

    @pl.when(i == 0)
    def _():
        def body(r, carry):
            row_copy(idx_ref, r, 0, buf0, 0).start()
            return carry
        lax.fori_loop(0, MOE_ROWS, body, 0, unroll=8)

    start_gather(idx_ref, MOE_ROWS, buf1, 1)
    wait_gather(buf0, 0)
    tile(buf0, 0)
    start_gather(idx_nxt_ref, 0, buf0, 0)
    wait_gather(buf1, 1)
    tile(buf1, 1)

    @pl.when(i == n - 1)
    def _():
        wait_gather(buf0, 0)


def _ple_call(x2s_buf, dest, p, wts, *, row_off, name):
    n_rows, d_ple = p.shape
    d_model = wts["w_ple_gate"].shape[0]
    step_rows = 2 * MOE_ROWS
    n_steps = n_rows // step_rows
    blk_off = row_off // step_rows
    last = blk_off + n_steps - 1
    dest = dest.reshape(-1, 1, step_rows)
    idx_spec = functools.partial(pl.BlockSpec, (1, 1, step_rows), memory_space=pltpu.SMEM)
    return pl.pallas_call(
        _ple_kernel,
        grid=(n_steps,),
        in_specs=[
            idx_spec(index_map=lambda i: (blk_off + i, 0, 0)),
            idx_spec(index_map=lambda i: (jnp.minimum(blk_off + i + 1, last), 0, 0)),
            pl.BlockSpec((step_rows, d_ple), lambda i: (i, 0)),
            _const_spec((1, d_model)),
            _const_spec((d_model, d_model)),
            _const_spec((d_ple, d_model)),
            pl.BlockSpec(memory_space=pl.ANY),
        ],
        out_specs=pl.BlockSpec((step_rows, d_model), lambda i: (i, 0)),
        out_shape=jax.ShapeDtypeStruct((n_rows, d_model), _f32),
        scratch_shapes=[pltpu.VMEM((ROW_TILE, LANES), _f32), pltpu.VMEM((ROW_TILE, LANES), _f32),
                        pltpu.SemaphoreType.DMA((2,))],
        compiler_params=pltpu.CompilerParams(
            dimension_semantics=("arbitrary",), vmem_limit_bytes=VMEM_LIMIT_BYTES),
        name=name,
    )(dest, dest, p, wts["g_ple"], wts["w_ple_gate"], wts["w_ple_proj"], x2s_buf)


def _band_bias(table, tq):
    band = BAND_PREV + tq
    n_heads, n_rel = table.shape
    n_far = band - 1 - REL_CLIP
    length = band + tq
    n_near = length - n_far - n_rel
    tab = table.astype(_f32)
    f = jnp.concatenate([jnp.broadcast_to(tab[:, n_rel - 1:], (n_heads, n_far)), tab[:, ::-1],
                         jnp.broadcast_to(tab[:, :1], (n_heads, n_near))], axis=1)
    skew = jnp.tile(f, (1, tq))[:, :tq * (length - 1)].reshape(n_heads, tq, length - 1)
    bias = skew[:, :, tq - 1:tq - 1 + band]
    kc = lax.broadcasted_iota(jnp.int32, (tq, band), 1) // CHUNK
    qc = lax.broadcasted_iota(jnp.int32, (tq, band), 0) // CHUNK
    in_band = (kc >= qc) & (kc <= qc + N_PREV_CHUNKS)
    return jnp.where(in_band[None], bias, MASKED)


def _pair_block_diag(w):
    n2, c, _ = w.shape
    w = w.reshape(n2 // 2, 2, c, c)
    zero = jnp.zeros_like(w[:, 0])
    top = jnp.concatenate([w[:, 0], zero], axis=2)
    bottom = jnp.concatenate([zero, w[:, 1]], axis=2)
    return jnp.concatenate([top, bottom], axis=1)


def _layer_weights(l, tq, g_mix, w_in, w_pool, pool_scale, g_q, g_k, rel_table, w_br_pool, w_br_att, w_out,
                   g_ffn, w_coarse, b_coarse, w_fine, b_fine, w_gate_e, w_up_e, w_down_e, g_ple, w_ple_gate,
                   w_ple_proj):
    d_model = w_in.shape[1]
    pad = ROUTE_LANES - N_GROUPS - N_EXPERTS
    head_of = jnp.arange(ATT_WIDTH, dtype=jnp.int32) // HEAD_DIM
    head_sum = jnp.where(head_of[:, None] == head_of[None, :], 1.0 / HEAD_DIM, 0.0)
    w_route = jnp.concatenate([w_coarse[l], w_fine[l], jnp.zeros((d_model, pad), _f32)], axis=1)
    w_route_hi = w_route.astype(_bf16)
    w_route_lo = (w_route - w_route_hi.astype(_f32)).astype(_bf16)
    b_route = jnp.concatenate([b_coarse[l], b_fine[l], jnp.zeros((pad,), _f32)])
    return {
        "g_mix": g_mix[l].reshape(1, d_model),
        "w_in": w_in[l].astype(_bf16),
        "w_pool": _pair_block_diag(w_pool[l].astype(_bf16)),
        "pool_scale": pool_scale[l].reshape(1, POOL_WIDTH),
        "g_q": (jnp.tile(g_q[l], N_HEADS) * (HEAD_DIM ** -0.5 * LOG2E)).reshape(1, ATT_WIDTH),
        "g_k": jnp.tile(g_k[l], N_HEADS).reshape(1, ATT_WIDTH),
        "bd": head_sum.astype(_bf16),
        "bias": _band_bias(rel_table[l] * LOG2E, tq),
        "w_br_pool": w_br_pool[l].astype(_bf16),
        "w_br_att": w_br_att[l].astype(_bf16),
        "w_out": w_out[l].astype(_bf16),
        "g_ffn": g_ffn[l].reshape(1, d_model),
        "w_route": jnp.concatenate([w_route_hi, w_route_lo], axis=1),
        "b_route": b_route.reshape(1, ROUTE_LANES),
        "w_route_hi": w_route_hi,
        "w_gate_e": w_gate_e[l].astype(_bf16),
        "w_up_e": w_up_e[l].astype(_bf16),
        "w_down_e": w_down_e[l].astype(_bf16),
        "g_ple": g_ple[l].reshape(1, d_model),
        "w_ple_gate": w_ple_gate[l].astype(_bf16),
        "w_ple_proj": w_ple_proj[l].astype(_bf16),
    }


def _pick_tile(t_len, want):
    tq = min(want, t_len)
    assert t_len % tq == 0 and tq % CHUNK == 0 and BAND_PREV % tq == 0
    return tq


def kernel(x_prompt, x_sample, cache_k, cache_v, state_pool, p_prompt, p_sample, g_mix, w_in, w_pool, pool_scale, g_q, g_k, rel_table, w_br_pool, w_br_att, w_out, g_ffn, w_coarse, b_coarse, w_fine, b_fine, w_gate_e, w_up_e, w_down_e, g_ple, w_ple_gate, w_ple_proj):
    depth = w_in.shape[0]
    bp, tp, d_model = x_prompt.shape
    bs, ts, _ = x_sample.shape
    assert d_model == SUBLANES * LANES
    tq_p = _pick_tile(tp, 512)
    tq_s = _pick_tile(ts, 256)
    sub_p = 2 if tp % (2 * tq_p) == 0 and BAND_PREV % (2 * tq_p) == 0 else 1
    rows_p, rows_s = bp * tp, bs * ts
    total = rows_p + rows_s
    assert rows_p % (2 * MOE_ROWS) == 0 and rows_s % (2 * MOE_ROWS) == 0 and total % RANK_ROWS[-1] == 0
    n_blocks = -(-(total + N_CLASSES * (MOE_ROWS - 1)) // MOE_ROWS)
    n_blocks = -(-n_blocks // MOE_PAIR) * MOE_PAIR

    xp, xs = x_prompt, x_sample
    outs = [[] for _ in range(6)]
    for l in range(depth):
        wts = _layer_weights(l, min(ATT_ROWS, max(tq_p, tq_s)), g_mix, w_in, w_pool, pool_scale, g_q, g_k, rel_table, w_br_pool,
                             w_br_att, w_out, g_ffn, w_coarse, b_coarse, w_fine, b_fine, w_gate_e, w_up_e,
                             w_down_e, g_ple, w_ple_gate, w_ple_proj)
        zeros_kv = jnp.zeros((bp, BAND_PREV, ATT_WIDTH), _f32)
        zeros_pool = jnp.zeros((bp, POOL_HIST, POOL_WIDTH), _f32)
        x1_buf, route_buf, kp, vp, pp, counts_p = _mixer_call(
            xp, zeros_kv, zeros_kv, zeros_pool, wts, None,
            total_rows=total, tq=tq_p, n_sub=sub_p, pos0=0, row_off=0, name="mixer_prompt")
        pool_hist = jnp.pad(state_pool[l], ((0, 0), (POOL_HIST - POOL_STATE, 0), (0, 0)))
        x1_buf, route_buf, kn, vn, pn, counts_s = _mixer_call(
            xs, cache_k[l].reshape(bs, BAND_PREV, ATT_WIDTH), cache_v[l].reshape(bs, BAND_PREV, ATT_WIDTH),
            pool_hist, wts, (x1_buf, route_buf),
            total_rows=total, tq=tq_s, n_sub=1, pos0=PAST_LEN, row_off=rows_p, name="mixer_sample")

        counts = counts_p + counts_s
        dest = _rank_call(route_buf, counts).reshape(total // MOE_ROWS, 1, MOE_ROWS)
        tables = _block_tables(counts, n_blocks)
        xs_buf = _dispatch_call(x1_buf, dest, tables["zero_end"], n_blocks)
        x2s_buf = _moe_call(xs_buf, tables, wts, n_blocks)

        xp = _ple_call(x2s_buf, dest, p_prompt[l].reshape(rows_p, -1), wts, row_off=0,
                       name="ple_prompt").reshape(bp, tp, d_model)
        xs = _ple_call(x2s_buf, dest, p_sample[l].reshape(rows_s, -1), wts, row_off=rows_p,
                       name="ple_sample").reshape(bs, ts, d_model)

        tail_p = min(BAND_PREV, tp)
        if tail_p < BAND_PREV:
            kp = jnp.pad(kp, ((0, 0), (BAND_PREV - tail_p, 0), (0, 0)))
            vp = jnp.pad(vp, ((0, 0), (BAND_PREV - tail_p, 0), (0, 0)))
        outs[0].append(kp.reshape(bp, BAND_PREV, N_HEADS, HEAD_DIM))
        outs[1].append(vp.reshape(bp, BAND_PREV, N_HEADS, HEAD_DIM))
        outs[2].append(pp)
        outs[3].append(kn.reshape(bs, ts, N_HEADS, HEAD_DIM))
        outs[4].append(vn.reshape(bs, ts, N_HEADS, HEAD_DIM))
        outs[5].append(pn)
    return (xp, xs) + tuple(jnp.stack(o) for o in outs)
```

```python
import functools

import jax
import jax.numpy as jnp
from jax import lax
from jax.experimental import pallas as pl
from jax.experimental.pallas import tpu as pltpu

CHUNK = 64
N_HEADS = 8
HEAD_DIM = 64
ATT_WIDTH = N_HEADS * HEAD_DIM
POOL_WINDOWS = (2, 4, 8, 16)
POOL_GROUP = 128
POOL_WIDTH = POOL_GROUP * len(POOL_WINDOWS)
POOL_STATE = max(POOL_WINDOWS) - 1
N_PREV_CHUNKS = 8
BAND_PREV = N_PREV_CHUNKS * CHUNK
REL_CLIP = 256
N_GROUPS = 4
EXPERTS_PER_GROUP = 8
N_EXPERTS = N_GROUPS * EXPERTS_PER_GROUP
PAST_LEN = 2048
EPS = 1e-6
MASKED = -1e30
LOG2E = 1.4426950408889634

LANES = 128
SUBLANES = 8
VMEM_LIMIT_BYTES = 56 * 1024 * 1024

POOL_HIST = 16
POOL_PAD = SUBLANES
POOL_BASE = POOL_PAD + POOL_HIST
POOL_LEVELS = 3
ATT_ROWS = 256
PAIRS_PER_GROUP = EXPERTS_PER_GROUP * (EXPERTS_PER_GROUP - 1) // 2
N_CLASSES = N_GROUPS * PAIRS_PER_GROUP
MOE_ROWS = 256
RANK_ROWS = (1536, 1024, 512)
ROUTE_LANES = 128
ROW_TILE = MOE_ROWS * SUBLANES
SPARE_ROWS = 1024

_f32 = jnp.float32
_bf16 = jnp.bfloat16


def _dot(a, b):
    return jnp.dot(a, b, preferred_element_type=_f32)


def _rms_scale(x):
    return lax.rsqrt(jnp.mean(x * x, axis=-1, keepdims=True) + EPS)


def _load_row_tiles(ref, rows, row0=0):
    return jnp.concatenate(
        [ref[pl.ds(row0 * SUBLANES + s, rows, stride=SUBLANES), :] for s in range(SUBLANES)], axis=-1)


def _store_row_tiles(ref, val, rows, row0=0):
    for s in range(SUBLANES):
        ref[pl.ds(row0 * SUBLANES + s, rows, stride=SUBLANES), :] = val[:, s * LANES:(s + 1) * LANES]


def _const_spec(shape):
    return pl.BlockSpec(shape, lambda *_: (0,) * len(shape), pipeline_mode=pl.Buffered(1))


def _route_rows(logits):
    lane = lax.broadcasted_iota(jnp.int32, (1, ROUTE_LANES), 1)
    lane_f = lane.astype(_f32)
    neg = jnp.float32(-jnp.inf)
    far = jnp.float32(ROUTE_LANES)
    cl = jnp.where(lane < N_GROUPS, logits, neg)
    cmax = jnp.max(cl, axis=-1, keepdims=True)
    grp = jnp.min(jnp.where(cl == cmax, lane_f, far), axis=-1, keepdims=True)
    fine_grp = ((lane - N_GROUPS) >> 3).astype(_f32)
    fl = jnp.where(fine_grp == grp, logits, neg)
    m1 = jnp.max(fl, axis=-1, keepdims=True)
    i1 = jnp.min(jnp.where(fl == m1, lane_f, far), axis=-1, keepdims=True)
    fl2 = jnp.where(lane_f == i1, neg, fl)
    m2 = jnp.max(fl2, axis=-1, keepdims=True)
    i2 = jnp.min(jnp.where(fl2 == m2, lane_f, far), axis=-1, keepdims=True)
    first_lane = N_GROUPS + EXPERTS_PER_GROUP * grp
    la = jnp.minimum(i1, i2) - first_lane
    lb = jnp.maximum(i1, i2) - first_lane
    return PAIRS_PER_GROUP * grp + la * (2 * EXPERTS_PER_GROUP - 1 - la) * 0.5 + (lb - la - 1.0)


def _mixer_kernel(x_ref, k0_ref, v0_ref, p0_ref, gmix_ref, win_ref, wpool_ref, pscale_ref, gq_ref, gk_ref,
                  bd_ref, bias_ref, wbp_ref, wba_ref, wout_ref, gffn_ref, wroute_ref, broute_ref,
                  *rest, tq, n_sub, n_steps, pos0):
    x1_ref, route_ref, kout_ref, vout_ref, pout_ref, cnt_ref, kt_buf, v_buf, u_buf, s_buf = rest[-10:]
    t = pl.program_id(1)

    @pl.when(jnp.logical_and(pl.program_id(0) == 0, t == 0))
    def _init_counts():
        cnt_ref[...] = jnp.zeros_like(cnt_ref)

    att_rows = bias_ref.shape[1]
    band = BAND_PREV + att_rows
    step_rows = n_sub * tq
    d_model = x_ref.shape[-1]
    lane = lax.broadcasted_iota(jnp.int32, (1, LANES), 1)
    even = lane < HEAD_DIM
    col = lax.broadcasted_iota(jnp.int32, (1, band), 1)
    row = lax.broadcasted_iota(jnp.int32, (tq, 1), 0)

    @pl.when(t == 0)
    def _init_history():
        kt_buf[:, 0:BAND_PREV] = k0_ref[0].T.astype(_bf16)
        v_buf[0:BAND_PREV, :] = v0_ref[0].astype(_bf16)
        u_buf[0:POOL_PAD, :] = jnp.zeros((POOL_PAD, POOL_WIDTH), _f32)
        u_buf[POOL_PAD:POOL_BASE, :] = p0_ref[0]
        s_buf[:, :, 0:POOL_PAD, :] = jnp.zeros(s_buf.shape[:2] + (POOL_PAD, POOL_WIDTH), _f32)

    def pool_phase(j, st):
        r0 = j * tq
        x = x_ref[0, r0:r0 + tq, :]
        h = (x * _rms_scale(x) * gmix_ref[...]).astype(_bf16)
        u = _dot(h, win_ref[:, 0:POOL_WIDTH])
        u_buf[POOL_BASE + r0:POOL_BASE + r0 + tq, :] = u
        pos1 = pos0 + t * step_rows + r0 + row + 1
        w0 = POOL_PAD + r0
        span = POOL_HIST + tq
        diffs = []
        for g, w in enumerate(POOL_WINDOWS):
            sl = slice(g * POOL_GROUP, (g + 1) * POOL_GROUP)
            acc = u_buf[w0:w0 + span, sl] + u_buf[w0 - 1:w0 - 1 + span, sl]
            shift = 2
            while shift < w:
                level = s_buf.at[j, shift.bit_length() - 2]
                level[POOL_PAD:POOL_PAD + span, sl] = acc
                acc = acc + level[POOL_PAD - shift:POOL_PAD - shift + span, sl]
                shift *= 2
            cnt = jnp.minimum(pos1, w).astype(_f32)
            diffs.append((acc[POOL_HIST:, :] / cnt - u[:, sl]).astype(_bf16))
        pooled = [_dot(jnp.concatenate(diffs[2 * i:2 * i + 2], axis=-1), wpool_ref[i])
                  for i in range(len(POOL_WINDOWS) // 2)]
        st["x"], st["h"] = x, h
        st["a"] = (jnp.concatenate(pooled, axis=-1) * pscale_ref[...]).astype(_bf16)

    def qkv_phase(j, st):
        r0 = j * tq
        h = st["h"]
        q = _dot(h, win_ref[:, POOL_WIDTH:POOL_WIDTH + ATT_WIDTH])
        st["qn"] = (q * lax.rsqrt(_dot((q * q).astype(_bf16), bd_ref[...]) + EPS) * gq_ref[...]).astype(_bf16)
        k = _dot(h, win_ref[:, POOL_WIDTH + ATT_WIDTH:POOL_WIDTH + 2 * ATT_WIDTH])
        kn = k * lax.rsqrt(_dot((k * k).astype(_bf16), bd_ref[...]) + EPS) * gk_ref[...]
        v = _dot(h, win_ref[:, POOL_WIDTH + 2 * ATT_WIDTH:POOL_WIDTH + 3 * ATT_WIDTH])
        kout_ref[0, r0:r0 + tq, :] = kn
        vout_ref[0, r0:r0 + tq, :] = v
        kt_buf[:, BAND_PREV + r0:BAND_PREV + r0 + tq] = kn.T.astype(_bf16)
        v_buf[BAND_PREV + r0:BAND_PREV + r0 + tq, :] = v.astype(_bf16)

    def attention_phase(j, st):
        qn = st["qn"]
        heads = []
        for p in range(N_HEADS // 2):
            sl = slice(p * LANES, (p + 1) * LANES)
            blocks = []
            for r in range(tq // att_rows):
                k0 = j * tq + r * att_rows
                qp = qn[r * att_rows:(r + 1) * att_rows, sl]
                ktp = kt_buf[sl, k0:k0 + band]
                vp = v_buf[k0:k0 + band, sl]
                started = col >= BAND_PREV - (pos0 + t * step_rows + k0)
                acc = None
                inv = []
                for half in range(2):
                    keep = even if half == 0 else jnp.logical_not(even)
                    qh = jnp.where(keep, qp, jnp.zeros_like(qp))
                    vh = jnp.where(keep, vp, jnp.zeros_like(vp))
                    s = _dot(qh, ktp) + bias_ref[2 * p + half]
                    s = jnp.where(started, s, MASKED)
                    e = jnp.exp2(s - jnp.max(s, axis=-1, keepdims=True))
                    inv.append(1.0 / jnp.sum(e, axis=-1, keepdims=True))
                    part = _dot(e.astype(_bf16), vh)
                    acc = part if acc is None else acc + part
                blocks.append(acc * jnp.where(even, inv[0], inv[1]))
            heads.append(jnp.concatenate(blocks, axis=0))
        st["o"] = jnp.concatenate(heads, axis=-1).astype(_bf16)

    def merge_phase(j, st):
        r0 = j * tq
        x, h = st["x"], st["h"]
        gate_off = POOL_WIDTH + 3 * ATT_WIDTH
        ga = _dot(h, win_ref[:, gate_off:gate_off + d_model])
        gb = _dot(h, win_ref[:, gate_off + d_model:gate_off + 2 * d_model])
        m = (jax.nn.sigmoid(ga) * _dot(st["a"], wbp_ref[...])
             + jax.nn.sigmoid(gb) * _dot(st["o"], wba_ref[...]))
        x1 = x + _dot(m.astype(_bf16), wout_ref[...])
        _store_row_tiles(x1_ref, x1, tq, r0)
        h2 = x1 * _rms_scale(x1) * gffn_ref[...]
        h2_hi = h2.astype(_bf16)
        h2_lo = (h2 - h2_hi.astype(_f32)).astype(_bf16)
        hi_both = _dot(h2_hi, wroute_ref[...])
        logits = (hi_both[:, 0:ROUTE_LANES] + hi_both[:, ROUTE_LANES:]
                  + _dot(h2_lo, wroute_ref[:, 0:ROUTE_LANES]) + broute_ref[...])
        cls = jnp.broadcast_to(_route_rows(logits), (tq, ROUTE_LANES))
        route_ref[r0:r0 + tq, :] = cls
        cnt_ref[...] += jnp.sum(jnp.where(cls == lane.astype(_f32), 1.0, 0.0), axis=0, keepdims=True)

    states = [{} for _ in range(n_sub)]
    for phase in (pool_phase, qkv_phase, attention_phase, merge_phase):
        for j in range(n_sub):
            phase(j, states[j])

    pout_ref[0] = u_buf[POOL_BASE + step_rows - POOL_STATE:POOL_BASE + step_rows, :]
    if n_steps > 1:
        chunk = min(step_rows, BAND_PREV)
        for c in range(BAND_PREV // chunk):
            dst = slice(c * chunk, (c + 1) * chunk)
            src = slice(step_rows + c * chunk, step_rows + (c + 1) * chunk)
            kt_buf[:, dst] = kt_buf[:, src]
            v_buf[dst, :] = v_buf[src, :]
        u_buf[POOL_PAD:POOL_BASE, :] = u_buf[POOL_PAD + step_rows:POOL_BASE + step_rows, :]


def _mixer_call(x, k0, v0, p0, wts, shared, *, total_rows, tq, n_sub, pos0, row_off, name):
    bsz, t_len, d_model = x.shape
    step_rows = n_sub * tq
    n_steps = t_len // step_rows
    tail = min(BAND_PREV, t_len)
    assert t_len % step_rows == 0 and tail % step_rows == 0 and row_off % step_rows == 0
    tail_steps = tail // step_rows
    att_rows = min(ATT_ROWS, tq)
    assert tq % att_rows == 0
    band = BAND_PREV + att_rows
    blk_off = row_off // step_rows

    def tail_map(b, t):
        return (b, jnp.maximum(t - (n_steps - tail_steps), 0), 0)

    in_specs = [
        pl.BlockSpec((1, step_rows, d_model), lambda b, t: (b, t, 0)),
        pl.BlockSpec((1, BAND_PREV, ATT_WIDTH), lambda b, t: (b, 0, 0), pipeline_mode=pl.Buffered(1)),
        pl.BlockSpec((1, BAND_PREV, ATT_WIDTH), lambda b, t: (b, 0, 0), pipeline_mode=pl.Buffered(1)),
        pl.BlockSpec((1, POOL_HIST, POOL_WIDTH), lambda b, t: (b, 0, 0)),
        _const_spec((1, d_model)),
        _const_spec(wts["w_in"].shape),
        _const_spec(wts["w_pool"].shape),
        _const_spec((1, POOL_WIDTH)),
        _const_spec((1, ATT_WIDTH)),
        _const_spec((1, ATT_WIDTH)),
        _const_spec((ATT_WIDTH, ATT_WIDTH)),
        _const_spec((N_HEADS, att_rows, band)),
        _const_spec(wts["w_br_pool"].shape),
        _const_spec(wts["w_br_att"].shape),
        _const_spec(wts["w_out"].shape),
        _const_spec((1, d_model)),
        _const_spec((d_model, 2 * ROUTE_LANES)),
        _const_spec((1, ROUTE_LANES)),
    ]
    operands = [x, k0, v0, p0, wts["g_mix"], wts["w_in"], wts["w_pool"], wts["pool_scale"], wts["g_q"],
                wts["g_k"], wts["bd"], wts["bias"][:, :att_rows, :band], wts["w_br_pool"], wts["w_br_att"],
                wts["w_out"], wts["g_ffn"], wts["w_route"], wts["b_route"]]
    aliases = {}
    if shared is not None:
        aliases = {len(operands): 0, len(operands) + 1: 1}
        in_specs += [pl.BlockSpec(memory_space=pl.ANY)] * 2
        operands += list(shared)
    out_specs = [
        pl.BlockSpec((step_rows * SUBLANES, LANES), lambda b, t: (blk_off + b * n_steps + t, 0)),
        pl.BlockSpec((step_rows, ROUTE_LANES), lambda b, t: (blk_off + b * n_steps + t, 0)),
        pl.BlockSpec((1, step_rows, ATT_WIDTH), tail_map),
        pl.BlockSpec((1, step_rows, ATT_WIDTH), tail_map),
        pl.BlockSpec((1, POOL_STATE, POOL_WIDTH), lambda b, t: (b, 0, 0)),
        pl.BlockSpec((SUBLANES, LANES), lambda b, t: (0, 0)),
    ]
    out_shape = [
        jax.ShapeDtypeStruct((total_rows * SUBLANES, LANES), _f32),
        jax.ShapeDtypeStruct((total_rows, ROUTE_LANES), _f32),
        jax.ShapeDtypeStruct((bsz, tail, ATT_WIDTH), _f32),
        jax.ShapeDtypeStruct((bsz, tail, ATT_WIDTH), _f32),
        jax.ShapeDtypeStruct((bsz, POOL_STATE, POOL_WIDTH), _f32),
        jax.ShapeDtypeStruct((SUBLANES, LANES), _f32),
    ]
    kern = functools.partial(_mixer_kernel, tq=tq, n_sub=n_sub, n_steps=n_steps, pos0=pos0)
    return pl.pallas_call(
        kern,
        grid=(bsz, n_steps),
        in_specs=in_specs,
        out_specs=out_specs,
        out_shape=out_shape,
        scratch_shapes=[
            pltpu.VMEM((ATT_WIDTH, BAND_PREV + step_rows), _bf16),
            pltpu.VMEM((BAND_PREV + step_rows, ATT_WIDTH), _bf16),
            pltpu.VMEM((POOL_BASE + step_rows, POOL_WIDTH), _f32),
            pltpu.VMEM((n_sub, POOL_LEVELS, POOL_BASE + tq, POOL_WIDTH), _f32),
        ],
        input_output_aliases=aliases,
        compiler_params=pltpu.CompilerParams(
            dimension_semantics=("arbitrary", "arbitrary"), vmem_limit_bytes=VMEM_LIMIT_BYTES),
        name=name,
    )(*operands)


def _lane_cumsum(x):
    lane = lax.broadcasted_iota(jnp.int32, x.shape, 1)
    shift = 1
    while shift < LANES:
        x = x + jnp.where(lane >= shift, pltpu.roll(x, shift, axis=1), 0.0)
        shift *= 2
    return x


def _rank_kernel(route_ref, counts_ref, dest_ref, base, before):
    i = pl.program_id(0)
    rows = route_ref.shape[0]
    lane = lax.broadcasted_iota(jnp.int32, (1, LANES), 1).astype(_f32)
    oh = jnp.where(route_ref[...] == lane, 1.0, 0.0)

    @pl.when(i == 0)
    def _():
        ri = lax.broadcasted_iota(jnp.int32, (rows, rows), 0)
        ci = lax.broadcasted_iota(jnp.int32, (rows, rows), 1)
        before[...] = jnp.where(ri < ci, 1.0, 0.0).astype(_bf16)
        cnt = counts_ref[...]
        padded = jnp.floor((cnt + (MOE_ROWS - 1)) * (1.0 / MOE_ROWS)) * MOE_ROWS
        first = _lane_cumsum(padded) - padded
        ri = lax.broadcasted_iota(jnp.int32, (LANES, LANES), 0)
        ci = lax.broadcasted_iota(jnp.int32, (LANES, LANES), 1)
        first_col = jnp.sum(jnp.where(ri == ci, first[0:1, :], 0.0), axis=-1, keepdims=True)
        base[...] = jnp.broadcast_to(first_col, base.shape)

    oh_t = oh.T
    earlier = _dot(oh_t.astype(_bf16), before[...])
    slot_row = jnp.sum(oh_t * (base[:, 0:1] + earlier), axis=0, keepdims=True)
    base[...] += jnp.sum(oh_t, axis=1, keepdims=True)
    dest_ref[0] = slot_row.astype(jnp.int32)


def _rank_call(route_buf, counts):
    total = route_buf.shape[0]
    rows = next(r for r in RANK_ROWS if total % r == 0)
    n_tiles = total // rows
    return pl.pallas_call(
        _rank_kernel,
        grid=(n_tiles,),
        in_specs=[pl.BlockSpec((rows, ROUTE_LANES), lambda i: (i, 0)),
                  pl.BlockSpec((SUBLANES, LANES), lambda i: (0, 0))],
        out_specs=pl.BlockSpec((1, 1, rows), lambda i: (i, 0, 0)),
        out_shape=jax.ShapeDtypeStruct((n_tiles, 1, rows), jnp.int32),
        scratch_shapes=[pltpu.VMEM((LANES, LANES), _f32), pltpu.VMEM((rows, rows), _bf16)],
        compiler_params=pltpu.CompilerParams(dimension_semantics=("arbitrary",)),
        name="rank",
    )(route_buf, counts)


def _block_tables(counts, n_blocks):
    cnt = counts[0, :N_CLASSES].astype(jnp.int32)
    padded = (cnt + MOE_ROWS - 1) // MOE_ROWS * MOE_ROWS
    pend = jnp.cumsum(padded)
    nb = pend[-1] // MOE_ROWS
    blk = jnp.minimum(jnp.arange(n_blocks, dtype=jnp.int32), nb - 1)
    blk_cls = jnp.sum(pend[None, :] <= (blk * MOE_ROWS)[:, None], axis=1, dtype=jnp.int32)
    blk_cls = jnp.minimum(blk_cls, N_CLASSES - 1)
    grp = blk_cls // PAIRS_PER_GROUP
    pair = blk_cls % PAIRS_PER_GROUP
    firsts = jnp.arange(1, EXPERTS_PER_GROUP, dtype=jnp.int32)
    pair_start = firsts * (2 * EXPERTS_PER_GROUP - 1 - firsts) // 2
    la = jnp.sum(pair[:, None] >= pair_start[None, :], axis=1, dtype=jnp.int32)
    lb = pair - la * (2 * EXPERTS_PER_GROUP - 1 - la) // 2 + la + 1
    zero_end = jnp.where(cnt > 0, pend, 0).astype(jnp.int32)
    pad_start = jnp.where(cnt > 0, pend - padded + cnt, 0).astype(jnp.int32)
    return {
        "ea": grp * EXPERTS_PER_GROUP + la, "eb": grp * EXPERTS_PER_GROUP + lb,
        "nb": nb.reshape(1).astype(jnp.int32), "zero_end": zero_end, "pad_start": pad_start,
    }


N_DISPATCH_BUFS = 3


def _dispatch_kernel(zero_end_ref, pad_start_ref, idx_ref, x1_hbm, xs_hbm, inv_ref, tbuf, zbuf, lsem, ssem, zsem):
    i = pl.program_id(0)
    n = pl.num_programs(0)
    slot = i % N_DISPATCH_BUFS
    n_rows = n * MOE_ROWS

    def load_copy(tile, s):
        return pltpu.make_async_copy(x1_hbm.at[pl.ds(tile * ROW_TILE, ROW_TILE), :], tbuf.at[s], lsem.at[s])

    def wait_scatter(s):
        pltpu.make_async_copy(tbuf.at[s], xs_hbm.at[pl.ds(0, ROW_TILE), :], ssem.at[s]).wait()

    def zero_copy(c):
        end = zero_end_ref[c]
        return pltpu.make_async_copy(zbuf, xs_hbm.at[pl.ds((end - MOE_ROWS) * SUBLANES, ROW_TILE), :], zsem.at[0])

    @pl.when(i == 0)
    def _():
        zbuf[...] = jnp.zeros_like(zbuf)

        def start(c, carry):
            @pl.when(zero_end_ref[c] > 0)
            def _():
                zero_copy(c).start()
            return carry

        def wait(c, carry):
            @pl.when(zero_end_ref[c] > 0)
            def _():
                zero_copy(c).wait()
            return carry

        lax.fori_loop(0, N_CLASSES, start, 0)
        load_copy(0, 0).start()

        def pad_class(c, carry):
            def pad_slot(s, inner):
                inv_ref[s] = n_rows + (s & (SPARE_ROWS - 1))
                return inner
            return lax.fori_loop(pad_start_ref[c], zero_end_ref[c], pad_slot, carry)

        lax.fori_loop(0, N_CLASSES, pad_class, 0)
        lax.fori_loop(0, N_CLASSES, wait, 0)

    @pl.when(i >= N_DISPATCH_BUFS - 1)
    def _():
        wait_scatter((i + 1) % N_DISPATCH_BUFS)

    @pl.when(i + 1 < n)
    def _():
        load_copy(i + 1, (i + 1) % N_DISPATCH_BUFS).start()

    load_copy(i, slot).wait()

    def body(r, carry):
        dst = idx_ref[0, 0, r]
        pltpu.make_async_copy(
            tbuf.at[slot, pl.ds(r * SUBLANES, SUBLANES), :],
            xs_hbm.at[pl.ds(dst * SUBLANES, SUBLANES), :],
            ssem.at[slot]).start()
        inv_ref[dst] = i * MOE_ROWS + r
        return carry
    lax.fori_loop(0, MOE_ROWS, body, 0, unroll=8)

    @pl.when(i == n - 1)
    def _():
        @pl.when(n > 1)
        def _():
            wait_scatter((i + N_DISPATCH_BUFS - 1) % N_DISPATCH_BUFS)
        wait_scatter(slot)


def _dispatch_call(x1_buf, dest, tables, n_blocks):
    n_tiles = dest.shape[0]
    grid_spec = pltpu.PrefetchScalarGridSpec(
        num_scalar_prefetch=2,
        grid=(n_tiles,),
        in_specs=[
            pl.BlockSpec((1, 1, MOE_ROWS), lambda i, *_: (i, 0, 0), memory_space=pltpu.SMEM),
            pl.BlockSpec(memory_space=pl.ANY),
        ],
        out_specs=[pl.BlockSpec(memory_space=pl.ANY), pl.BlockSpec(memory_space=pltpu.SMEM)],
        scratch_shapes=[
            pltpu.VMEM((N_DISPATCH_BUFS, ROW_TILE, LANES), _f32),
            pltpu.VMEM((ROW_TILE, LANES), _f32),
            pltpu.SemaphoreType.DMA((N_DISPATCH_BUFS,)),
            pltpu.SemaphoreType.DMA((N_DISPATCH_BUFS,)),
            pltpu.SemaphoreType.DMA((1,)),
        ],
    )
    return pl.pallas_call(
        _dispatch_kernel,
        grid_spec=grid_spec,
        out_shape=[jax.ShapeDtypeStruct((n_blocks * ROW_TILE, LANES), _f32),
                   jax.ShapeDtypeStruct((n_blocks * MOE_ROWS,), jnp.int32)],
        compiler_params=pltpu.CompilerParams(dimension_semantics=("arbitrary",)),
        name="dispatch",
    )(tables["zero_end"], tables["pad_start"], dest, x1_buf)


MOE_PAIR = 2


def _moe_kernel(ea_ref, eb_ref, nb_ref, inv_ref, xs_ref, gffn_ref, wr_ref, br_ref, *rest):
    x2_hbm, obuf, osem = rest[-3:]
    step = pl.program_id(0)
    nb = nb_ref[0]
    par = step % 2
    lane = lax.broadcasted_iota(jnp.int32, (1, ROUTE_LANES), 1)

    def wait_rows(parity, n_rows):
        pltpu.make_async_copy(obuf.at[parity, pl.ds(0, n_rows * SUBLANES), :],
                              x2_hbm.at[pl.ds(0, n_rows * SUBLANES), :], osem.at[parity]).wait()

    def load_phase(k, st):
        x = _load_row_tiles(xs_ref, MOE_ROWS, k * MOE_ROWS)
        st["x"] = x
        st["h"] = (x * _rms_scale(x) * gffn_ref[...]).astype(_bf16)

    def weight_phase(k, st):
        ea = ea_ref[step * MOE_PAIR + k]
        eb = eb_ref[step * MOE_PAIR + k]
        grp = ea // EXPERTS_PER_GROUP
        logits = _dot(st["h"], wr_ref[...]) + br_ref[...]

        def pick(col):
            return jnp.sum(jnp.where(lane == col, logits, 0.0), axis=-1, keepdims=True)

        coarse = lane < N_GROUPS
        cmax = jnp.max(jnp.where(coarse, logits, -jnp.inf), axis=-1, keepdims=True)
        csum = jnp.sum(jnp.where(coarse, jnp.exp(logits - cmax), 0.0), axis=-1, keepdims=True)
        gp = jnp.exp(pick(grp) - cmax) / csum
        fa = pick(N_GROUPS + ea)
        fb = pick(N_GROUPS + eb)
        fmax = jnp.maximum(fa, fb)
        pa = jnp.exp(fa - fmax)
        pb = jnp.exp(fb - fmax)
        st["wa"] = gp * pa / (pa + pb)
        st["wb"] = gp * pb / (pa + pb)

    def expert_phase(k, st):
        wga, wua, wda, wgb, wub, wdb = rest[6 * k:6 * k + 6]
        h = st["h"]

        def expert(wg_ref, wu_ref, wd_ref):
            g = _dot(h, wg_ref[0])
            u = _dot(h, wu_ref[0])
            return _dot((jax.nn.silu(g) * u).astype(_bf16), wd_ref[0])

        y = st["wa"] * expert(wga, wua, wda) + st["wb"] * expert(wgb, wub, wdb)
        stage = obuf.at[par]
        _store_row_tiles(stage, st["x"] + y, MOE_ROWS, k * MOE_ROWS)
        for r in range(k * MOE_ROWS, (k + 1) * MOE_ROWS):
            pltpu.make_async_copy(stage.at[pl.ds(r * SUBLANES, SUBLANES), :],
                                  x2_hbm.at[pl.ds(inv_ref[0, 0, r] * SUBLANES, SUBLANES), :],
                                  osem.at[par]).start()

    def run(n_live):
        @pl.when(step >= 2)
        def _():
            wait_rows(par, MOE_PAIR * MOE_ROWS)

        states = [{} for _ in range(n_live)]
        for phase in (load_phase, weight_phase, expert_phase):
            for k in range(n_live):
                phase(k, states[k])

        @pl.when((step * MOE_PAIR + n_live) == nb)
        def _drain():
            @pl.when(step >= 1)
            def _():
                wait_rows(1 - par, MOE_PAIR * MOE_ROWS)
            wait_rows(par, n_live * MOE_ROWS)

    first = step * MOE_PAIR
    for n_live in range(MOE_PAIR, 0, -1):
        cond = (first + n_live <= nb) if n_live == MOE_PAIR else (first + n_live == nb)
        pl.when(cond)(functools.partial(run, n_live))


def _moe_call(xs_buf, inv, tables, wts, n_blocks, n_rows):
    d_model = wts["w_gate_e"].shape[1]
    d_exp = wts["w_gate_e"].shape[2]
    assert n_blocks % MOE_PAIR == 0 and SPARE_ROWS >= 2 * MOE_PAIR * MOE_ROWS
    step_rows = MOE_PAIR * MOE_ROWS

    def used_step(s, ea, eb, nb):
        return (jnp.minimum(s, (nb[0] - 1) // MOE_PAIR), 0)

    def expert_of(which, k):
        def index_map(s, ea, eb, nb):
            blk = jnp.minimum(s * MOE_PAIR + k, nb[0] - 1)
            return ((ea, eb)[which][blk], 0, 0)
        return index_map

    weight_specs = []
    weight_args = []
    for k in range(MOE_PAIR):
        for which in range(2):
            weight_specs += [pl.BlockSpec((1, d_model, d_exp), expert_of(which, k)),
                             pl.BlockSpec((1, d_model, d_exp), expert_of(which, k)),
                             pl.BlockSpec((1, d_exp, d_model), expert_of(which, k))]
            weight_args += [wts["w_gate_e"], wts["w_up_e"], wts["w_down_e"]]

    grid_spec = pltpu.PrefetchScalarGridSpec(
        num_scalar_prefetch=3,
        grid=(n_blocks // MOE_PAIR,),
        in_specs=[
            pl.BlockSpec((1, 1, step_rows), lambda s, ea, eb, nb: used_step(s, ea, eb, nb) + (0,),
                         memory_space=pltpu.SMEM),
            pl.BlockSpec((MOE_PAIR * ROW_TILE, LANES), used_step),
            pl.BlockSpec((1, d_model), lambda s, *_: (0, 0)),
            pl.BlockSpec((d_model, ROUTE_LANES), lambda s, *_: (0, 0)),
            pl.BlockSpec((1, ROUTE_LANES), lambda s, *_: (0, 0)),
        ] + weight_specs,
        out_specs=pl.BlockSpec(memory_space=pl.ANY),
        scratch_shapes=[
            pltpu.VMEM((2, MOE_PAIR * ROW_TILE, LANES), _f32),
            pltpu.SemaphoreType.DMA((2,)),
        ],
    )
    return pl.pallas_call(
        _moe_kernel,
        grid_spec=grid_spec,
        out_shape=jax.ShapeDtypeStruct(((n_rows + SPARE_ROWS) * SUBLANES, LANES), _f32),
        compiler_params=pltpu.CompilerParams(
            dimension_semantics=("arbitrary",), vmem_limit_bytes=VMEM_LIMIT_BYTES),
        name="moe",
    )(tables["ea"], tables["eb"], tables["nb"], inv.reshape(n_blocks // MOE_PAIR, 1, step_rows), xs_buf,
      wts["g_ffn"], wts["w_route_hi"], wts["b_route"], *weight_args)


PLE_ROWS = 512


def _ple_kernel(x2_ref, p_ref, gple_ref, wgate_ref, wproj_ref, out_ref):
    x2 = _load_row_tiles(x2_ref, PLE_ROWS)
    hn = (x2 * _rms_scale(x2) * gple_ref[...]).astype(_bf16)
    gate = jax.nn.sigmoid(_dot(hn, wgate_ref[...]))
    out_ref[...] = x2 + _dot(p_ref[...].astype(_bf16), wproj_ref[...]) * gate


def _ple_call(x2_buf, p, wts, *, row_off, name):
    n_rows, d_ple = p.shape
    d_model = wts["w_ple_gate"].shape[0]
    assert n_rows % PLE_ROWS == 0 and row_off % PLE_ROWS == 0
    blk_off = row_off // PLE_ROWS
    return pl.pallas_call(
        _ple_kernel,
        grid=(n_rows // PLE_ROWS,),
        in_specs=[
            pl.BlockSpec((PLE_ROWS * SUBLANES, LANES), lambda i: (blk_off + i, 0)),
            pl.BlockSpec((PLE_ROWS, d_ple), lambda i: (i, 0)),
            _const_spec((1, d_model)),
            _const_spec((d_model, d_model)),
            _const_spec((d_ple, d_model)),
        ],
        out_specs=pl.BlockSpec((PLE_ROWS, d_model), lambda i: (i, 0)),
        out_shape=jax.ShapeDtypeStruct((n_rows, d_model), _f32),
        compiler_params=pltpu.CompilerParams(
            dimension_semantics=("arbitrary",), vmem_limit_bytes=VMEM_LIMIT_BYTES),
        name=name,
    )(x2_buf, p, wts["g_ple"], wts["w_ple_gate"], wts["w_ple_proj"])


def _band_bias(table, tq):
    band = BAND_PREV + tq
    n_heads, n_rel = table.shape
    n_far = band - 1 - REL_CLIP
    length = band + tq
    n_near = length - n_far - n_rel
    tab = table.astype(_f32)
    f = jnp.concatenate([jnp.broadcast_to(tab[:, n_rel - 1:], (n_heads, n_far)), tab[:, ::-1],
                         jnp.broadcast_to(tab[:, :1], (n_heads, n_near))], axis=1)
    skew = jnp.tile(f, (1, tq))[:, :tq * (length - 1)].reshape(n_heads, tq, length - 1)
    bias = skew[:, :, tq - 1:tq - 1 + band]
    kc = lax.broadcasted_iota(jnp.int32, (tq, band), 1) // CHUNK
    qc = lax.broadcasted_iota(jnp.int32, (tq, band), 0) // CHUNK
    in_band = (kc >= qc) & (kc <= qc + N_PREV_CHUNKS)
    return jnp.where(in_band[None], bias, MASKED)


def _pair_block_diag(w):
    n2, c, _ = w.shape
    w = w.reshape(n2 // 2, 2, c, c)
    zero = jnp.zeros_like(w[:, 0])
    top = jnp.concatenate([w[:, 0], zero], axis=2)
    bottom = jnp.concatenate([zero, w[:, 1]], axis=2)
    return jnp.concatenate([top, bottom], axis=1)


def _layer_weights(l, tq, g_mix, w_in, w_pool, pool_scale, g_q, g_k, rel_table, w_br_pool, w_br_att, w_out,
                   g_ffn, w_coarse, b_coarse, w_fine, b_fine, w_gate_e, w_up_e, w_down_e, g_ple, w_ple_gate,
                   w_ple_proj):
    d_model = w_in.shape[1]
    pad = ROUTE_LANES - N_GROUPS - N_EXPERTS
    head_of = jnp.arange(ATT_WIDTH, dtype=jnp.int32) // HEAD_DIM
    head_sum = jnp.where(head_of[:, None] == head_of[None, :], 1.0 / HEAD_DIM, 0.0)
    w_route = jnp.concatenate([w_coarse[l], w_fine[l], jnp.zeros((d_model, pad), _f32)], axis=1)
    w_route_hi = w_route.astype(_bf16)
    w_route_lo = (w_route - w_route_hi.astype(_f32)).astype(_bf16)
    b_route = jnp.concatenate([b_coarse[l], b_fine[l], jnp.zeros((pad,), _f32)])
    return {
        "g_mix": g_mix[l].reshape(1, d_model),
        "w_in": w_in[l].astype(_bf16),
        "w_pool": _pair_block_diag(w_pool[l].astype(_bf16)),
        "pool_scale": pool_scale[l].reshape(1, POOL_WIDTH),
        "g_q": (jnp.tile(g_q[l], N_HEADS) * (HEAD_DIM ** -0.5 * LOG2E)).reshape(1, ATT_WIDTH),
        "g_k": jnp.tile(g_k[l], N_HEADS).reshape(1, ATT_WIDTH),
        "bd": head_sum.astype(_bf16),
        "bias": _band_bias(rel_table[l] * LOG2E, tq),
        "w_br_pool": w_br_pool[l].astype(_bf16),
        "w_br_att": w_br_att[l].astype(_bf16),
        "w_out": w_out[l].astype(_bf16),
        "g_ffn": g_ffn[l].reshape(1, d_model),
        "w_route": jnp.concatenate([w_route_hi, w_route_lo], axis=1),
        "b_route": b_route.reshape(1, ROUTE_LANES),
        "w_route_hi": w_route_hi,
        "w_gate_e": w_gate_e[l].astype(_bf16),
        "w_up_e": w_up_e[l].astype(_bf16),
        "w_down_e": w_down_e[l].astype(_bf16),
        "g_ple": g_ple[l].reshape(1, d_model),
        "w_ple_gate": w_ple_gate[l].astype(_bf16),
        "w_ple_proj": w_ple_proj[l].astype(_bf16),
    }


def _pick_tile(t_len, want):
    tq = min(want, t_len)
    assert t_len % tq == 0 and tq % CHUNK == 0 and BAND_PREV % tq == 0
    return tq


def kernel(x_prompt, x_sample, cache_k, cache_v, state_pool, p_prompt, p_sample, g_mix, w_in, w_pool, pool_scale, g_q, g_k, rel_table, w_br_pool, w_br_att, w_out, g_ffn, w_coarse, b_coarse, w_fine, b_fine, w_gate_e, w_up_e, w_down_e, g_ple, w_ple_gate, w_ple_proj):
    depth = w_in.shape[0]
    bp, tp, d_model = x_prompt.shape
    bs, ts, _ = x_sample.shape
    assert d_model == SUBLANES * LANES
    tq_p = _pick_tile(tp, 512)
    tq_s = _pick_tile(ts, 256)
    sub_p = 2 if tp % (2 * tq_p) == 0 and BAND_PREV % (2 * tq_p) == 0 else 1
    rows_p, rows_s = bp * tp, bs * ts
    total = rows_p + rows_s
    assert rows_p % (2 * MOE_ROWS) == 0 and rows_s % (2 * MOE_ROWS) == 0 and total % RANK_ROWS[-1] == 0
    n_blocks = -(-(total + N_CLASSES * (MOE_ROWS - 1)) // MOE_ROWS)
    n_blocks = -(-n_blocks // MOE_PAIR) * MOE_PAIR

    xp, xs = x_prompt, x_sample
    outs = [[] for _ in range(6)]
    for l in range(depth):
        wts = _layer_weights(l, min(ATT_ROWS, max(tq_p, tq_s)), g_mix, w_in, w_pool, pool_scale, g_q, g_k, rel_table, w_br_pool,
                             w_br_att, w_out, g_ffn, w_coarse, b_coarse, w_fine, b_fine, w_gate_e, w_up_e,
                             w_down_e, g_ple, w_ple_gate, w_ple_proj)
        zeros_kv = jnp.zeros((bp, BAND_PREV, ATT_WIDTH), _f32)
        zeros_pool = jnp.zeros((bp, POOL_HIST, POOL_WIDTH), _f32)
        x1_buf, route_buf, kp, vp, pp, counts_p = _mixer_call(
            xp, zeros_kv, zeros_kv, zeros_pool, wts, None,
            total_rows=total, tq=tq_p, n_sub=sub_p, pos0=0, row_off=0, name="mixer_prompt")
        pool_hist = jnp.pad(state_pool[l], ((0, 0), (POOL_HIST - POOL_STATE, 0), (0, 0)))
        x1_buf, route_buf, kn, vn, pn, counts_s = _mixer_call(
            xs, cache_k[l].reshape(bs, BAND_PREV, ATT_WIDTH), cache_v[l].reshape(bs, BAND_PREV, ATT_WIDTH),
            pool_hist, wts, (x1_buf, route_buf),
            total_rows=total, tq=tq_s, n_sub=1, pos0=PAST_LEN, row_off=rows_p, name="mixer_sample")

        counts = counts_p + counts_s
        dest = _rank_call(route_buf, counts).reshape(total // MOE_ROWS, 1, MOE_ROWS)
        tables = _block_tables(counts, n_blocks)
        xs_buf, inv = _dispatch_call(x1_buf, dest, tables, n_blocks)
        x2_buf = _moe_call(xs_buf, inv, tables, wts, n_blocks, total)

        xp = _ple_call(x2_buf, p_prompt[l].reshape(rows_p, -1), wts, row_off=0,
                       name="ple_prompt").reshape(bp, tp, d_model)
        xs = _ple_call(x2_buf, p_sample[l].reshape(rows_s, -1), wts, row_off=rows_p,
                       name="ple_sample").reshape(bs, ts, d_model)

        tail_p = min(BAND_PREV, tp)
        if tail_p < BAND_PREV:
            kp = jnp.pad(kp, ((0, 0), (BAND_PREV - tail_p, 0), (0, 0)))
            vp = jnp.pad(vp, ((0, 0), (BAND_PREV - tail_p, 0), (0, 0)))
        outs[0].append(kp.reshape(bp, BAND_PREV, N_HEADS, HEAD_DIM))
        outs[1].append(vp.reshape(bp, BAND_PREV, N_HEADS, HEAD_DIM))
        outs[2].append(pp)
        outs[3].append(kn.reshape(bs, ts, N_HEADS, HEAD_DIM))
        outs[4].append(vn.reshape(bs, ts, N_HEADS, HEAD_DIM))
        outs[5].append(pn)
    return (xp, xs) + tuple(jnp.stack(o) for o in outs)
```

```python
import functools

import jax
import jax.numpy as jnp
from jax import lax
from jax.experimental import pallas as pl
from jax.experimental.pallas import tpu as pltpu

CHUNK = 64
N_HEADS = 8
HEAD_DIM = 64
ATT_WIDTH = N_HEADS * HEAD_DIM
POOL_WINDOWS = (2, 4, 8, 16)
POOL_GROUP = 128
POOL_WIDTH = POOL_GROUP * len(POOL_WINDOWS)
POOL_STATE = max(POOL_WINDOWS) - 1
N_PREV_CHUNKS = 8
BAND_PREV = N_PREV_CHUNKS * CHUNK
REL_CLIP = 256
N_GROUPS = 4
EXPERTS_PER_GROUP = 8
N_EXPERTS = N_GROUPS * EXPERTS_PER_GROUP
PAST_LEN = 2048
EPS = 1e-6
MASKED = -1e30
LOG2E = 1.4426950408889634

LANES = 128
SUBLANES = 8
VMEM_LIMIT_BYTES = 56 * 1024 * 1024

POOL_HIST = 16
POOL_PAD = SUBLANES
POOL_BASE = POOL_PAD + POOL_HIST
POOL_LEVELS = 3
ATT_ROWS = 256
PAIRS_PER_GROUP = EXPERTS_PER_GROUP * (EXPERTS_PER_GROUP - 1) // 2
N_CLASSES = N_GROUPS * PAIRS_PER_GROUP
MOE_ROWS = 256
RANK_ROWS = (1536, 1024, 512)
ROUTE_LANES = 128
ROW_TILE = MOE_ROWS * SUBLANES

_f32 = jnp.float32
_bf16 = jnp.bfloat16


def _dot(a, b):
    return jnp.dot(a, b, preferred_element_type=_f32)


def _rms_scale(x):
    return lax.rsqrt(jnp.mean(x * x, axis=-1, keepdims=True) + EPS)


def _load_row_tiles(ref, rows, row0=0):
    return jnp.concatenate(
        [ref[pl.ds(row0 * SUBLANES + s, rows, stride=SUBLANES), :] for s in range(SUBLANES)], axis=-1)


def _store_row_tiles(ref, val, rows, row0=0):
    for s in range(SUBLANES):
        ref[pl.ds(row0 * SUBLANES + s, rows, stride=SUBLANES), :] = val[:, s * LANES:(s + 1) * LANES]


def _const_spec(shape):
    return pl.BlockSpec(shape, lambda *_: (0,) * len(shape), pipeline_mode=pl.Buffered(1))


def _route_rows(logits):
    lane = lax.broadcasted_iota(jnp.int32, (1, ROUTE_LANES), 1)
    lane_f = lane.astype(_f32)
    neg = jnp.float32(-jnp.inf)
    far = jnp.float32(ROUTE_LANES)
    cl = jnp.where(lane < N_GROUPS, logits, neg)
    cmax = jnp.max(cl, axis=-1, keepdims=True)
    grp = jnp.min(jnp.where(cl == cmax, lane_f, far), axis=-1, keepdims=True)
    fine_grp = ((lane - N_GROUPS) >> 3).astype(_f32)
    fl = jnp.where(fine_grp == grp, logits, neg)
    m1 = jnp.max(fl, axis=-1, keepdims=True)
    i1 = jnp.min(jnp.where(fl == m1, lane_f, far), axis=-1, keepdims=True)
    fl2 = jnp.where(lane_f == i1, neg, fl)
    m2 = jnp.max(fl2, axis=-1, keepdims=True)
    i2 = jnp.min(jnp.where(fl2 == m2, lane_f, far), axis=-1, keepdims=True)
    first_lane = N_GROUPS + EXPERTS_PER_GROUP * grp
    la = jnp.minimum(i1, i2) - first_lane
    lb = jnp.maximum(i1, i2) - first_lane
    return PAIRS_PER_GROUP * grp + la * (2 * EXPERTS_PER_GROUP - 1 - la) * 0.5 + (lb - la - 1.0)


def _mixer_kernel(x_ref, k0_ref, v0_ref, p0_ref, gmix_ref, win_ref, wpool_ref, pscale_ref, gq_ref, gk_ref,
                  bd_ref, bias_ref, wbp_ref, wba_ref, wout_ref, gffn_ref, wroute_ref, broute_ref,
                  *rest, tq, n_sub, n_steps, pos0):
    x1_ref, route_ref, kout_ref, vout_ref, pout_ref, cnt_ref, kt_buf, v_buf, u_buf, s_buf = rest[-10:]
    t = pl.program_id(1)

    @pl.when(jnp.logical_and(pl.program_id(0) == 0, t == 0))
    def _init_counts():
        cnt_ref[...] = jnp.zeros_like(cnt_ref)

    att_rows = bias_ref.shape[1]
    band = BAND_PREV + att_rows
    step_rows = n_sub * tq
    d_model = x_ref.shape[-1]
    lane = lax.broadcasted_iota(jnp.int32, (1, LANES), 1)
    even = lane < HEAD_DIM
    col = lax.broadcasted_iota(jnp.int32, (1, band), 1)
    row = lax.broadcasted_iota(jnp.int32, (tq, 1), 0)

    @pl.when(t == 0)
    def _init_history():
        kt_buf[:, 0:BAND_PREV] = k0_ref[0].T.astype(_bf16)
        v_buf[0:BAND_PREV, :] = v0_ref[0].astype(_bf16)
        u_buf[0:POOL_PAD, :] = jnp.zeros((POOL_PAD, POOL_WIDTH), _f32)
        u_buf[POOL_PAD:POOL_BASE, :] = p0_ref[0]
        s_buf[:, :, 0:POOL_PAD, :] = jnp.zeros(s_buf.shape[:2] + (POOL_PAD, POOL_WIDTH), _f32)

    def pool_phase(j, st):
        r0 = j * tq
        x = x_ref[0, r0:r0 + tq, :]
        h = (x * _rms_scale(x) * gmix_ref[...]).astype(_bf16)
        u = _dot(h, win_ref[:, 0:POOL_WIDTH])
        u_buf[POOL_BASE + r0:POOL_BASE + r0 + tq, :] = u
        pos1 = pos0 + t * step_rows + r0 + row + 1
        w0 = POOL_PAD + r0
        span = POOL_HIST + tq
        diffs = []
        for g, w in enumerate(POOL_WINDOWS):
            sl = slice(g * POOL_GROUP, (g + 1) * POOL_GROUP)
            acc = u_buf[w0:w0 + span, sl] + u_buf[w0 - 1:w0 - 1 + span, sl]
            shift = 2
            while shift < w:
                level = s_buf.at[j, shift.bit_length() - 2]
                level[POOL_PAD:POOL_PAD + span, sl] = acc
                acc = acc + level[POOL_PAD - shift:POOL_PAD - shift + span, sl]
                shift *= 2
            cnt = jnp.minimum(pos1, w).astype(_f32)
            diffs.append((acc[POOL_HIST:, :] / cnt - u[:, sl]).astype(_bf16))
        pooled = [_dot(jnp.concatenate(diffs[2 * i:2 * i + 2], axis=-1), wpool_ref[i])
                  for i in range(len(POOL_WINDOWS) // 2)]
        st["x"], st["h"] = x, h
        st["a"] = (jnp.concatenate(pooled, axis=-1) * pscale_ref[...]).astype(_bf16)

    def qkv_phase(j, st):
        r0 = j * tq
        h = st["h"]
        q = _dot(h, win_ref[:, POOL_WIDTH:POOL_WIDTH + ATT_WIDTH])
        st["qn"] = (q * lax.rsqrt(_dot((q * q).astype(_bf16), bd_ref[...]) + EPS) * gq_ref[...]).astype(_bf16)
        k = _dot(h, win_ref[:, POOL_WIDTH + ATT_WIDTH:POOL_WIDTH + 2 * ATT_WIDTH])
        kn = k * lax.rsqrt(_dot((k * k).astype(_bf16), bd_ref[...]) + EPS) * gk_ref[...]
        v = _dot(h, win_ref[:, POOL_WIDTH + 2 * ATT_WIDTH:POOL_WIDTH + 3 * ATT_WIDTH])
        kout_ref[0, r0:r0 + tq, :] = kn
        vout_ref[0, r0:r0 + tq, :] = v
        kt_buf[:, BAND_PREV + r0:BAND_PREV + r0 + tq] = kn.T.astype(_bf16)
        v_buf[BAND_PREV + r0:BAND_PREV + r0 + tq, :] = v.astype(_bf16)

    def attention_phase(j, st):
        qn = st["qn"]
        heads = []
        for p in range(N_HEADS // 2):
            sl = slice(p * LANES, (p + 1) * LANES)
            blocks = []
            for r in range(tq // att_rows):
                k0 = j * tq + r * att_rows
                qp = qn[r * att_rows:(r + 1) * att_rows, sl]
                ktp = kt_buf[sl, k0:k0 + band]
                vp = v_buf[k0:k0 + band, sl]
                started = col >= BAND_PREV - (pos0 + t * step_rows + k0)
                acc = None
                inv = []
                for half in range(2):
                    keep = even if half == 0 else jnp.logical_not(even)
                    qh = jnp.where(keep, qp, jnp.zeros_like(qp))
                    vh = jnp.where(keep, vp, jnp.zeros_like(vp))
                    s = _dot(qh, ktp) + bias_ref[2 * p + half]
                    s = jnp.where(started, s, MASKED)
                    e = jnp.exp2(s - jnp.max(s, axis=-1, keepdims=True))
                    inv.append(1.0 / jnp.sum(e, axis=-1, keepdims=True))
                    part = _dot(e.astype(_bf16), vh)
                    acc = part if acc is None else acc + part
                blocks.append(acc * jnp.where(even, inv[0], inv[1]))
            heads.append(jnp.concatenate(blocks, axis=0))
        st["o"] = jnp.concatenate(heads, axis=-1).astype(_bf16)

    def merge_phase(j, st):
        r0 = j * tq
        x, h = st["x"], st["h"]
        gate_off = POOL_WIDTH + 3 * ATT_WIDTH
        ga = _dot(h, win_ref[:, gate_off:gate_off + d_model])
        gb = _dot(h, win_ref[:, gate_off + d_model:gate_off + 2 * d_model])
        m = (jax.nn.sigmoid(ga) * _dot(st["a"], wbp_ref[...])
             + jax.nn.sigmoid(gb) * _dot(st["o"], wba_ref[...]))
        x1 = x + _dot(m.astype(_bf16), wout_ref[...])
        _store_row_tiles(x1_ref, x1, tq, r0)
        h2 = x1 * _rms_scale(x1) * gffn_ref[...]
        h2_hi = h2.astype(_bf16)
        h2_lo = (h2 - h2_hi.astype(_f32)).astype(_bf16)
        hi_both = _dot(h2_hi, wroute_ref[...])
        logits = (hi_both[:, 0:ROUTE_LANES] + hi_both[:, ROUTE_LANES:]
                  + _dot(h2_lo, wroute_ref[:, 0:ROUTE_LANES]) + broute_ref[...])
        cls = jnp.broadcast_to(_route_rows(logits), (tq, ROUTE_LANES))
        route_ref[r0:r0 + tq, :] = cls
        cnt_ref[...] += jnp.sum(jnp.where(cls == lane.astype(_f32), 1.0, 0.0), axis=0, keepdims=True)

    states = [{} for _ in range(n_sub)]
    for phase in (pool_phase, qkv_phase, attention_phase, merge_phase):
        for j in range(n_sub):
            phase(j, states[j])

    pout_ref[0] = u_buf[POOL_BASE + step_rows - POOL_STATE:POOL_BASE + step_rows, :]
    if n_steps > 1:
        chunk = min(step_rows, BAND_PREV)
        for c in range(BAND_PREV // chunk):
            dst = slice(c * chunk, (c + 1) * chunk)
            src = slice(step_rows + c * chunk, step_rows + (c + 1) * chunk)
            kt_buf[:, dst] = kt_buf[:, src]
            v_buf[dst, :] = v_buf[src, :]
        u_buf[POOL_PAD:POOL_BASE, :] = u_buf[POOL_PAD + step_rows:POOL_BASE + step_rows, :]


def _mixer_call(x, k0, v0, p0, wts, shared, *, total_rows, tq, n_sub, pos0, row_off, name):
    bsz, t_len, d_model = x.shape
    step_rows = n_sub * tq
    n_steps = t_len // step_rows
    tail = min(BAND_PREV, t_len)
    assert t_len % step_rows == 0 and tail % step_rows == 0 and row_off % step_rows == 0
    tail_steps = tail // step_rows
    att_rows = min(ATT_ROWS, tq)
    assert tq % att_rows == 0
    band = BAND_PREV + att_rows
    blk_off = row_off // step_rows

    def tail_map(b, t):
        return (b, jnp.maximum(t - (n_steps - tail_steps), 0), 0)

    in_specs = [
        pl.BlockSpec((1, step_rows, d_model), lambda b, t: (b, t, 0)),
        pl.BlockSpec((1, BAND_PREV, ATT_WIDTH), lambda b, t: (b, 0, 0), pipeline_mode=pl.Buffered(1)),
        pl.BlockSpec((1, BAND_PREV, ATT_WIDTH), lambda b, t: (b, 0, 0), pipeline_mode=pl.Buffered(1)),
        pl.BlockSpec((1, POOL_HIST, POOL_WIDTH), lambda b, t: (b, 0, 0)),
        _const_spec((1, d_model)),
        _const_spec(wts["w_in"].shape),
        _const_spec(wts["w_pool"].shape),
        _const_spec((1, POOL_WIDTH)),
        _const_spec((1, ATT_WIDTH)),
        _const_spec((1, ATT_WIDTH)),
        _const_spec((ATT_WIDTH, ATT_WIDTH)),
        _const_spec((N_HEADS, att_rows, band)),
        _const_spec(wts["w_br_pool"].shape),
        _const_spec(wts["w_br_att"].shape),
        _const_spec(wts["w_out"].shape),
        _const_spec((1, d_model)),
        _const_spec((d_model, 2 * ROUTE_LANES)),
        _const_spec((1, ROUTE_LANES)),
    ]
    operands = [x, k0, v0, p0, wts["g_mix"], wts["w_in"], wts["w_pool"], wts["pool_scale"], wts["g_q"],
                wts["g_k"], wts["bd"], wts["bias"][:, :att_rows, :band], wts["w_br_pool"], wts["w_br_att"],
                wts["w_out"], wts["g_ffn"], wts["w_route"], wts["b_route"]]
    aliases = {}
    if shared is not None:
        aliases = {len(operands): 0, len(operands) + 1: 1}
        in_specs += [pl.BlockSpec(memory_space=pl.ANY)] * 2
        operands += list(shared)
    out_specs = [
        pl.BlockSpec((step_rows * SUBLANES, LANES), lambda b, t: (blk_off + b * n_steps + t, 0)),
        pl.BlockSpec((step_rows, ROUTE_LANES), lambda b, t: (blk_off + b * n_steps + t, 0)),
        pl.BlockSpec((1, step_rows, ATT_WIDTH), tail_map),
        pl.BlockSpec((1, step_rows, ATT_WIDTH), tail_map),
        pl.BlockSpec((1, POOL_STATE, POOL_WIDTH), lambda b, t: (b, 0, 0)),
        pl.BlockSpec((SUBLANES, LANES), lambda b, t: (0, 0)),
    ]
    out_shape = [
        jax.ShapeDtypeStruct((total_rows * SUBLANES, LANES), _f32),
        jax.ShapeDtypeStruct((total_rows, ROUTE_LANES), _f32),
        jax.ShapeDtypeStruct((bsz, tail, ATT_WIDTH), _f32),
        jax.ShapeDtypeStruct((bsz, tail, ATT_WIDTH), _f32),
        jax.ShapeDtypeStruct((bsz, POOL_STATE, POOL_WIDTH), _f32),
        jax.ShapeDtypeStruct((SUBLANES, LANES), _f32),
    ]
    kern = functools.partial(_mixer_kernel, tq=tq, n_sub=n_sub, n_steps=n_steps, pos0=pos0)
    return pl.pallas_call(
        kern,
        grid=(bsz, n_steps),
        in_specs=in_specs,
        out_specs=out_specs,
        out_shape=out_shape,
        scratch_shapes=[
            pltpu.VMEM((ATT_WIDTH, BAND_PREV + step_rows), _bf16),
            pltpu.VMEM((BAND_PREV + step_rows, ATT_WIDTH), _bf16),
            pltpu.VMEM((POOL_BASE + step_rows, POOL_WIDTH), _f32),
            pltpu.VMEM((n_sub, POOL_LEVELS, POOL_BASE + tq, POOL_WIDTH), _f32),
        ],
        input_output_aliases=aliases,
        compiler_params=pltpu.CompilerParams(
            dimension_semantics=("arbitrary", "arbitrary"), vmem_limit_bytes=VMEM_LIMIT_BYTES),
        name=name,
    )(*operands)


def _lane_cumsum(x):
    lane = lax.broadcasted_iota(jnp.int32, x.shape, 1)
    shift = 1
    while shift < LANES:
        x = x + jnp.where(lane >= shift, pltpu.roll(x, shift, axis=1), 0.0)
        shift *= 2
    return x


def _rank_kernel(route_ref, counts_ref, dest_ref, base, before):
    i = pl.program_id(0)
    rows = route_ref.shape[0]
    lane = lax.broadcasted_iota(jnp.int32, (1, LANES), 1).astype(_f32)
    oh = jnp.where(route_ref[...] == lane, 1.0, 0.0)

    @pl.when(i == 0)
    def _():
        ri = lax.broadcasted_iota(jnp.int32, (rows, rows), 0)
        ci = lax.broadcasted_iota(jnp.int32, (rows, rows), 1)
        before[...] = jnp.where(ri < ci, 1.0, 0.0).astype(_bf16)
        cnt = counts_ref[...]
        padded = jnp.floor((cnt + (MOE_ROWS - 1)) * (1.0 / MOE_ROWS)) * MOE_ROWS
        first = _lane_cumsum(padded) - padded
        ri = lax.broadcasted_iota(jnp.int32, (LANES, LANES), 0)
        ci = lax.broadcasted_iota(jnp.int32, (LANES, LANES), 1)
        first_col = jnp.sum(jnp.where(ri == ci, first[0:1, :], 0.0), axis=-1, keepdims=True)
        base[...] = jnp.broadcast_to(first_col, base.shape)

    oh_t = oh.T
    earlier = _dot(oh_t.astype(_bf16), before[...])
    slot_row = jnp.sum(oh_t * (base[:, 0:1] + earlier), axis=0, keepdims=True)
    base[...] += jnp.sum(oh_t, axis=1, keepdims=True)
    dest_ref[0] = slot_row.astype(jnp.int32)


def _rank_call(route_buf, counts):
    total = route_buf.shape[0]
    rows = next(r for r in RANK_ROWS if total % r == 0)
    n_tiles = total // rows
    return pl.pallas_call(
        _rank_kernel,
        grid=(n_tiles,),
        in_specs=[pl.BlockSpec((rows, ROUTE_LANES), lambda i: (i, 0)),
                  pl.BlockSpec((SUBLANES, LANES), lambda i: (0, 0))],
        out_specs=pl.BlockSpec((1, 1, rows), lambda i: (i, 0, 0)),
        out_shape=jax.ShapeDtypeStruct((n_tiles, 1, rows), jnp.int32),
        scratch_shapes=[pltpu.VMEM((LANES, LANES), _f32), pltpu.VMEM((rows, rows), _bf16)],
        compiler_params=pltpu.CompilerParams(dimension_semantics=("arbitrary",)),
        name="rank",
    )(route_buf, counts)


def _block_tables(counts, n_blocks):
    cnt = counts[0, :N_CLASSES].astype(jnp.int32)
    padded = (cnt + MOE_ROWS - 1) // MOE_ROWS * MOE_ROWS
    pend = jnp.cumsum(padded)
    nb = pend[-1] // MOE_ROWS
    blk = jnp.minimum(jnp.arange(n_blocks, dtype=jnp.int32), nb - 1)
    blk_cls = jnp.sum(pend[None, :] <= (blk * MOE_ROWS)[:, None], axis=1, dtype=jnp.int32)
    blk_cls = jnp.minimum(blk_cls, N_CLASSES - 1)
    grp = blk_cls // PAIRS_PER_GROUP
    pair = blk_cls % PAIRS_PER_GROUP
    firsts = jnp.arange(1, EXPERTS_PER_GROUP, dtype=jnp.int32)
    pair_start = firsts * (2 * EXPERTS_PER_GROUP - 1 - firsts) // 2
    la = jnp.sum(pair[:, None] >= pair_start[None, :], axis=1, dtype=jnp.int32)
    lb = pair - la * (2 * EXPERTS_PER_GROUP - 1 - la) // 2 + la + 1
    zero_end = jnp.where(cnt > 0, pend, 0).astype(jnp.int32)
    return {
        "ea": grp * EXPERTS_PER_GROUP + la, "eb": grp * EXPERTS_PER_GROUP + lb,
        "nb": nb.reshape(1).astype(jnp.int32), "zero_end": zero_end,
    }


N_DISPATCH_BUFS = 3
N_WEIGHT_SLABS = 256


def _dispatch_kernel(zero_end_ref, idx_ref, wg_ref, wu_ref, wd_ref, x1_hbm, xs_hbm, wg_out, wu_out, wd_out,
                     tbuf, zbuf, lsem, ssem, zsem, *, n_slabs):
    i = pl.program_id(0)
    n = pl.num_programs(0)
    slot = i % N_DISPATCH_BUFS

    @pl.when(i < n_slabs)
    def _round_weights():
        wg_out[...] = wg_ref[...].astype(_bf16)
        wu_out[...] = wu_ref[...].astype(_bf16)
        wd_out[...] = wd_ref[...].astype(_bf16)

    def load_copy(tile, s):
        return pltpu.make_async_copy(x1_hbm.at[pl.ds(tile * ROW_TILE, ROW_TILE), :], tbuf.at[s], lsem.at[s])

    def wait_scatter(s):
        pltpu.make_async_copy(tbuf.at[s], xs_hbm.at[pl.ds(0, ROW_TILE), :], ssem.at[s]).wait()

    def zero_copy(c):
        end = zero_end_ref[c]
        return pltpu.make_async_copy(zbuf, xs_hbm.at[pl.ds((end - MOE_ROWS) * SUBLANES, ROW_TILE), :], zsem.at[0])

    @pl.when(i == 0)
    def _():
        zbuf[...] = jnp.zeros_like(zbuf)

        def start(c, carry):
            @pl.when(zero_end_ref[c] > 0)
            def _():
                zero_copy(c).start()
            return carry

        def wait(c, carry):
            @pl.when(zero_end_ref[c] > 0)
            def _():
                zero_copy(c).wait()
            return carry

        lax.fori_loop(0, N_CLASSES, start, 0)
        lax.fori_loop(0, N_CLASSES, wait, 0)
        load_copy(0, 0).start()

    @pl.when(i >= N_DISPATCH_BUFS - 1)
    def _():
        wait_scatter((i + 1) % N_DISPATCH_BUFS)

    @pl.when(i + 1 < n)
    def _():
        load_copy(i + 1, (i + 1) % N_DISPATCH_BUFS).start()

    load_copy(i, slot).wait()

    def body(r, carry):
        pltpu.make_async_copy(
            tbuf.at[slot, pl.ds(r * SUBLANES, SUBLANES), :],
            xs_hbm.at[pl.ds(idx_ref[0, 0, r] * SUBLANES, SUBLANES), :],
            ssem.at[slot]).start()
        return carry
    lax.fori_loop(0, MOE_ROWS, body, 0, unroll=8)

    @pl.when(i == n - 1)
    def _():
        @pl.when(n > 1)
        def _():
            wait_scatter((i + N_DISPATCH_BUFS - 1) % N_DISPATCH_BUFS)
        wait_scatter(slot)


def _dispatch_call(x1_buf, dest, zero_end, n_blocks, expert_weights):
    n_tiles = dest.shape[0]
    n_slabs = min(N_WEIGHT_SLABS, 1 << (n_tiles.bit_length() - 1))
    flat = [w.reshape(-1, w.shape[-1]) for w in expert_weights]
    slab_specs = []
    for w in flat:
        assert w.shape[0] % n_slabs == 0
        slab_specs.append(pl.BlockSpec((w.shape[0] // n_slabs, w.shape[1]),
                                       lambda i, *_: (jnp.minimum(i, n_slabs - 1), 0)))
    grid_spec = pltpu.PrefetchScalarGridSpec(
        num_scalar_prefetch=1,
        grid=(n_tiles,),
        in_specs=[
            pl.BlockSpec((1, 1, MOE_ROWS), lambda i, *_: (i, 0, 0), memory_space=pltpu.SMEM),
        ] + slab_specs + [
            pl.BlockSpec(memory_space=pl.ANY),
        ],
        out_specs=[pl.BlockSpec(memory_space=pl.ANY)] + slab_specs,
        scratch_shapes=[
            pltpu.VMEM((N_DISPATCH_BUFS, ROW_TILE, LANES), _f32),
            pltpu.VMEM((ROW_TILE, LANES), _f32),
            pltpu.SemaphoreType.DMA((N_DISPATCH_BUFS,)),
            pltpu.SemaphoreType.DMA((N_DISPATCH_BUFS,)),
            pltpu.SemaphoreType.DMA((1,)),
        ],
    )
    outs = pl.pallas_call(
        functools.partial(_dispatch_kernel, n_slabs=n_slabs),
        grid_spec=grid_spec,
        out_shape=[jax.ShapeDtypeStruct((n_blocks * ROW_TILE, LANES), _f32)]
        + [jax.ShapeDtypeStruct(w.shape, _bf16) for w in flat],
        compiler_params=pltpu.CompilerParams(
            dimension_semantics=("arbitrary",), vmem_limit_bytes=VMEM_LIMIT_BYTES),
        name="dispatch",
    )(zero_end, dest, *flat, x1_buf)
    return outs[0], [o.reshape(w.shape) for o, w in zip(outs[1:], expert_weights)]


MOE_PAIR = 2


def _moe_kernel(ea_ref, eb_ref, nb_ref, xs_ref, gffn_ref, wr_ref, br_ref, *rest):
    out_ref = rest[-1]
    step = pl.program_id(0)
    nb = nb_ref[0]
    lane = lax.broadcasted_iota(jnp.int32, (1, ROUTE_LANES), 1)

    def load_phase(k, st):
        x = _load_row_tiles(xs_ref, MOE_ROWS, k * MOE_ROWS)
        st["x"] = x
        st["h"] = (x * _rms_scale(x) * gffn_ref[...]).astype(_bf16)

    def weight_phase(k, st):
        ea = ea_ref[step * MOE_PAIR + k]
        eb = eb_ref[step * MOE_PAIR + k]
        grp = ea // EXPERTS_PER_GROUP
        logits = _dot(st["h"], wr_ref[...]) + br_ref[...]

        def pick(col):
            return jnp.sum(jnp.where(lane == col, logits, 0.0), axis=-1, keepdims=True)

        coarse = lane < N_GROUPS
        cmax = jnp.max(jnp.where(coarse, logits, -jnp.inf), axis=-1, keepdims=True)
        csum = jnp.sum(jnp.where(coarse, jnp.exp(logits - cmax), 0.0), axis=-1, keepdims=True)
        gp = jnp.exp(pick(grp) - cmax) / csum
        fa = pick(N_GROUPS + ea)
        fb = pick(N_GROUPS + eb)
        fmax = jnp.maximum(fa, fb)
        pa = jnp.exp(fa - fmax)
        pb = jnp.exp(fb - fmax)
        st["wa"] = gp * pa / (pa + pb)
        st["wb"] = gp * pb / (pa + pb)

    def expert_phase(k, st):
        wga, wua, wda, wgb, wub, wdb = rest[6 * k:6 * k + 6]
        h = st["h"]

        def expert(wg_ref, wu_ref, wd_ref):
            g = _dot(h, wg_ref[0])
            u = _dot(h, wu_ref[0])
            return _dot((jax.nn.silu(g) * u).astype(_bf16), wd_ref[0])

        y = st["wa"] * expert(wga, wua, wda) + st["wb"] * expert(wgb, wub, wdb)
        _store_row_tiles(out_ref, st["x"] + y, MOE_ROWS, k * MOE_ROWS)

    def run(n_live):
        states = [{} for _ in range(n_live)]
        for phase in (load_phase, weight_phase, expert_phase):
            for k in range(n_live):
                phase(k, states[k])

    first = step * MOE_PAIR
    for n_live in range(MOE_PAIR, 0, -1):
        cond = (first + n_live <= nb) if n_live == MOE_PAIR else (first + n_live == nb)
        pl.when(cond)(functools.partial(run, n_live))


def _moe_call(xs_buf, tables, wts, expert_weights, n_blocks):
    w_gate, w_up, w_down = expert_weights
    d_model = w_gate.shape[1]
    d_exp = w_gate.shape[2]
    assert n_blocks % MOE_PAIR == 0

    def used_step(s, ea, eb, nb):
        return (jnp.minimum(s, (nb[0] - 1) // MOE_PAIR), 0)

    def expert_of(which, k):
        def index_map(s, ea, eb, nb):
            blk = jnp.minimum(s * MOE_PAIR + k, nb[0] - 1)
            return ((ea, eb)[which][blk], 0, 0)
        return index_map

    weight_specs = []
    weight_args = []
    for k in range(MOE_PAIR):
        for which in range(2):
            weight_specs += [pl.BlockSpec((1, d_model, d_exp), expert_of(which, k)),
                             pl.BlockSpec((1, d_model, d_exp), expert_of(which, k)),
                             pl.BlockSpec((1, d_exp, d_model), expert_of(which, k))]
            weight_args += [w_gate, w_up, w_down]

    grid_spec = pltpu.PrefetchScalarGridSpec(
        num_scalar_prefetch=3,
        grid=(n_blocks // MOE_PAIR,),
        in_specs=[
            pl.BlockSpec((MOE_PAIR * ROW_TILE, LANES), used_step),
            pl.BlockSpec((1, d_model), lambda s, *_: (0, 0)),
            pl.BlockSpec((d_model, ROUTE_LANES), lambda s, *_: (0, 0)),
            pl.BlockSpec((1, ROUTE_LANES), lambda s, *_: (0, 0)),
        ] + weight_specs,
        out_specs=pl.BlockSpec((MOE_PAIR * ROW_TILE, LANES), used_step),
    )
    return pl.pallas_call(
        _moe_kernel,
        grid_spec=grid_spec,
        out_shape=jax.ShapeDtypeStruct(xs_buf.shape, _f32),
        compiler_params=pltpu.CompilerParams(
            dimension_semantics=("arbitrary",), vmem_limit_bytes=VMEM_LIMIT_BYTES),
        name="moe",
    )(tables["ea"], tables["eb"], tables["nb"], xs_buf, wts["g_ffn"], wts["w_route_hi"], wts["b_route"],
      *weight_args)


def _ple_kernel(idx_ref, idx_nxt_ref, p_ref, gple_ref, wgate_ref, wproj_ref, x2s_hbm, out_ref, buf0, buf1, gsem):
    i = pl.program_id(0)
    n = pl.num_programs(0)

    def row_copy(idx_ref, r, off, buf, s):
        return pltpu.make_async_copy(
            x2s_hbm.at[pl.ds(idx_ref[0, 0, off + r] * SUBLANES, SUBLANES), :],
            buf.at[pl.ds(r * SUBLANES, SUBLANES), :],
            gsem.at[s])

    def start_gather(idx_ref, off, buf, s):
        for r in range(MOE_ROWS):
            row_copy(idx_ref, r, off, buf, s).start()

    def wait_gather(buf, s):
        pltpu.make_async_copy(x2s_hbm.at[pl.ds(0, ROW_TILE), :], buf, gsem.at[s]).wait()

    def tile(buf, half):
        rows = pl.ds(half * MOE_ROWS, MOE_ROWS)
        x2 = _load_row_tiles(buf, MOE_ROWS)
        hn = (x2 * _rms_scale(x2) * gple_ref[...]).astype(_bf16)
        gate = jax.nn.sigmoid(_dot(hn, wgate_ref[...]))
        out_ref[rows, :] = x2 + _dot(p_ref[rows, :].astype(_bf16), wproj_ref[...]) * gate

    @pl.when(i == 0)
    def _():
        def body(r, carry):
            row_copy(idx_ref, r, 0, buf0, 0).start()
            return carry
        lax.fori_loop(0, MOE_ROWS, body, 0, unroll=8)

    start_gather(idx_ref, MOE_ROWS, buf1, 1)
    wait_gather(buf0, 0)
    tile(buf0, 0)
    start_gather(idx_nxt_ref, 0, buf0, 0)
    wait_gather(buf1, 1)
    tile(buf1, 1)

    @pl.when(i == n - 1)
    def _():
        wait_gather(buf0, 0)


def _ple_call(x2s_buf, dest, p, wts, *, row_off, name):
    n_rows, d_ple = p.shape
    d_model = wts["w_ple_gate"].shape[0]
    step_rows = 2 * MOE_ROWS
    n_steps = n_rows // step_rows
    blk_off = row_off // step_rows
    last = blk_off + n_steps - 1
    dest = dest.reshape(-1, 1, step_rows)
    idx_spec = functools.partial(pl.BlockSpec, (1, 1, step_rows), memory_space=pltpu.SMEM)
    return pl.pallas_call(
        _ple_kernel,
        grid=(n_steps,),
        in_specs=[
            idx_spec(index_map=lambda i: (blk_off + i, 0, 0)),
            idx_spec(index_map=lambda i: (jnp.minimum(blk_off + i + 1, last), 0, 0)),
            pl.BlockSpec((step_rows, d_ple), lambda i: (i, 0)),
            _const_spec((1, d_model)),
            _const_spec((d_model, d_model)),
            _const_spec((d_ple, d_model)),
            pl.BlockSpec(memory_space=pl.ANY),
        ],
        out_specs=pl.BlockSpec((step_rows, d_model), lambda i: (i, 0)),
        out_shape=jax.ShapeDtypeStruct((n_rows, d_model), _f32),
        scratch_shapes=[pltpu.VMEM((ROW_TILE, LANES), _f32), pltpu.VMEM((ROW_TILE, LANES), _f32),
                        pltpu.SemaphoreType.DMA((2,))],
        compiler_params=pltpu.CompilerParams(
            dimension_semantics=("arbitrary",), vmem_limit_bytes=VMEM_LIMIT_BYTES),
        name=name,
    )(dest, dest, p, wts["g_ple"], wts["w_ple_gate"], wts["w_ple_proj"], x2s_buf)


def _band_bias(table, tq):
    band = BAND_PREV + tq
    n_heads, n_rel = table.shape
    n_far = band - 1 - REL_CLIP
    length = band + tq
    n_near = length - n_far - n_rel
    tab = table.astype(_f32)
    f = jnp.concatenate([jnp.broadcast_to(tab[:, n_rel - 1:], (n_heads, n_far)), tab[:, ::-1],
                         jnp.broadcast_to(tab[:, :1], (n_heads, n_near))], axis=1)
    skew = jnp.tile(f, (1, tq))[:, :tq * (length - 1)].reshape(n_heads, tq, length - 1)
    bias = skew[:, :, tq - 1:tq - 1 + band]
    kc = lax.broadcasted_iota(jnp.int32, (tq, band), 1) // CHUNK
    qc = lax.broadcasted_iota(jnp.int32, (tq, band), 0) // CHUNK
    in_band = (kc >= qc) & (kc <= qc + N_PREV_CHUNKS)
    return jnp.where(in_band[None], bias, MASKED)


def _pair_block_diag(w):
    n2, c, _ = w.shape
    w = w.reshape(n2 // 2, 2, c, c)
    zero = jnp.zeros_like(w[:, 0])
    top = jnp.concatenate([w[:, 0], zero], axis=2)
    bottom = jnp.concatenate([zero, w[:, 1]], axis=2)
    return jnp.concatenate([top, bottom], axis=1)


def _layer_weights(l, tq, g_mix, w_in, w_pool, pool_scale, g_q, g_k, rel_table, w_br_pool, w_br_att, w_out,
                   g_ffn, w_coarse, b_coarse, w_fine, b_fine, w_gate_e, w_up_e, w_down_e, g_ple, w_ple_gate,
                   w_ple_proj):
    d_model = w_in.shape[1]
    pad = ROUTE_LANES - N_GROUPS - N_EXPERTS
    head_of = jnp.arange(ATT_WIDTH, dtype=jnp.int32) // HEAD_DIM
    head_sum = jnp.where(head_of[:, None] == head_of[None, :], 1.0 / HEAD_DIM, 0.0)
    w_route = jnp.concatenate([w_coarse[l], w_fine[l], jnp.zeros((d_model, pad), _f32)], axis=1)
    w_route_hi = w_route.astype(_bf16)
    w_route_lo = (w_route - w_route_hi.astype(_f32)).astype(_bf16)
    b_route = jnp.concatenate([b_coarse[l], b_fine[l], jnp.zeros((pad,), _f32)])
    return {
        "g_mix": g_mix[l].reshape(1, d_model),
        "w_in": w_in[l].astype(_bf16),
        "w_pool": _pair_block_diag(w_pool[l].astype(_bf16)),
        "pool_scale": pool_scale[l].reshape(1, POOL_WIDTH),
        "g_q": (jnp.tile(g_q[l], N_HEADS) * (HEAD_DIM ** -0.5 * LOG2E)).reshape(1, ATT_WIDTH),
        "g_k": jnp.tile(g_k[l], N_HEADS).reshape(1, ATT_WIDTH),
        "bd": head_sum.astype(_bf16),
        "bias": _band_bias(rel_table[l] * LOG2E, tq),
        "w_br_pool": w_br_pool[l].astype(_bf16),
        "w_br_att": w_br_att[l].astype(_bf16),
        "w_out": w_out[l].astype(_bf16),
        "g_ffn": g_ffn[l].reshape(1, d_model),
        "w_route": jnp.concatenate([w_route_hi, w_route_lo], axis=1),
        "b_route": b_route.reshape(1, ROUTE_LANES),
        "w_route_hi": w_route_hi,
        "experts_f32": (w_gate_e[l], w_up_e[l], w_down_e[l]),
        "g_ple": g_ple[l].reshape(1, d_model),
        "w_ple_gate": w_ple_gate[l].astype(_bf16),
        "w_ple_proj": w_ple_proj[l].astype(_bf16),
    }


def _pick_tile(t_len, want):
    tq = min(want, t_len)
    assert t_len % tq == 0 and tq % CHUNK == 0 and BAND_PREV % tq == 0
    return tq


def kernel(x_prompt, x_sample, cache_k, cache_v, state_pool, p_prompt, p_sample, g_mix, w_in, w_pool, pool_scale, g_q, g_k, rel_table, w_br_pool, w_br_att, w_out, g_ffn, w_coarse, b_coarse, w_fine, b_fine, w_gate_e, w_up_e, w_down_e, g_ple, w_ple_gate, w_ple_proj):
    depth = w_in.shape[0]
    bp, tp, d_model = x_prompt.shape
    bs, ts, _ = x_sample.shape
    assert d_model == SUBLANES * LANES
    tq_p = _pick_tile(tp, 512)
    tq_s = _pick_tile(ts, 256)
    sub_p = 2 if tp % (2 * tq_p) == 0 and BAND_PREV % (2 * tq_p) == 0 else 1
    rows_p, rows_s = bp * tp, bs * ts
    total = rows_p + rows_s
    assert rows_p % (2 * MOE_ROWS) == 0 and rows_s % (2 * MOE_ROWS) == 0 and total % RANK_ROWS[-1] == 0
    n_blocks = -(-(total + N_CLASSES * (MOE_ROWS - 1)) // MOE_ROWS)
    n_blocks = -(-n_blocks // MOE_PAIR) * MOE_PAIR

    xp, xs = x_prompt, x_sample
    outs = [[] for _ in range(6)]
    for l in range(depth):
        wts = _layer_weights(l, min(ATT_ROWS, max(tq_p, tq_s)), g_mix, w_in, w_pool, pool_scale, g_q, g_k, rel_table, w_br_pool,
                             w_br_att, w_out, g_ffn, w_coarse, b_coarse, w_fine, b_fine, w_gate_e, w_up_e,
                             w_down_e, g_ple, w_ple_gate, w_ple_proj)
        zeros_kv = jnp.zeros((bp, BAND_PREV, ATT_WIDTH), _f32)
        zeros_pool = jnp.zeros((bp, POOL_HIST, POOL_WIDTH), _f32)
        x1_buf, route_buf, kp, vp, pp, counts_p = _mixer_call(
            xp, zeros_kv, zeros_kv, zeros_pool, wts, None,
            total_rows=total, tq=tq_p, n_sub=sub_p, pos0=0, row_off=0, name="mixer_prompt")
        pool_hist = jnp.pad(state_pool[l], ((0, 0), (POOL_HIST - POOL_STATE, 0), (0, 0)))
        x1_buf, route_buf, kn, vn, pn, counts_s = _mixer_call(
            xs, cache_k[l].reshape(bs, BAND_PREV, ATT_WIDTH), cache_v[l].reshape(bs, BAND_PREV, ATT_WIDTH),
            pool_hist, wts, (x1_buf, route_buf),
            total_rows=total, tq=tq_s, n_sub=1, pos0=PAST_LEN, row_off=rows_p, name="mixer_sample")

        counts = counts_p + counts_s
        dest = _rank_call(route_buf, counts).reshape(total // MOE_ROWS, 1, MOE_ROWS)
        tables = _block_tables(counts, n_blocks)
        xs_buf, experts_bf16 = _dispatch_call(x1_buf, dest, tables["zero_end"], n_blocks, wts["experts_f32"])
        x2s_buf = _moe_call(xs_buf, tables, wts, experts_bf16, n_blocks)

        xp = _ple_call(x2s_buf, dest, p_prompt[l].reshape(rows_p, -1), wts, row_off=0,
                       name="ple_prompt").reshape(bp, tp, d_model)
        xs = _ple_call(x2s_buf, dest, p_sample[l].reshape(rows_s, -1), wts, row_off=rows_p,
                       name="ple_sample").reshape(bs, ts, d_model)

        tail_p = min(BAND_PREV, tp)
        if tail_p < BAND_PREV:
            kp = jnp.pad(kp, ((0, 0), (BAND_PREV - tail_p, 0), (0, 0)))
            vp = jnp.pad(vp, ((0, 0), (BAND_PREV - tail_p, 0), (0, 0)))
        outs[0].append(kp.reshape(bp, BAND_PREV, N_HEADS, HEAD_DIM))
        outs[1].append(vp.reshape(bp, BAND_PREV, N_HEADS, HEAD_DIM))
        outs[2].append(pp)
        outs[3].append(kn.reshape(bs, ts, N_HEADS, HEAD_DIM))
        outs[4].append(vn.reshape(bs, ts, N_HEADS, HEAD_DIM))
        outs[5].append(pn)
    return (xp, xs) + tuple(jnp.stack(o) for o in outs)
```

```python
import functools

import jax
import jax.numpy as jnp
from jax import lax
from jax.experimental import pallas as pl
from jax.experimental.pallas import tpu as pltpu

CHUNK = 64
N_HEADS = 8
HEAD_DIM = 64
ATT_WIDTH = N_HEADS * HEAD_DIM
POOL_WINDOWS = (2, 4, 8, 16)
POOL_GROUP = 128
POOL_WIDTH = POOL_GROUP * len(POOL_WINDOWS)
POOL_STATE = max(POOL_WINDOWS) - 1
N_PREV_CHUNKS = 8
BAND_PREV = N_PREV_CHUNKS * CHUNK
REL_CLIP = 256
N_GROUPS = 4
EXPERTS_PER_GROUP = 8
N_EXPERTS = N_GROUPS * EXPERTS_PER_GROUP
PAST_LEN = 2048
EPS = 1e-6
MASKED = -1e30
LOG2E = 1.4426950408889634

LANES = 128
SUBLANES = 8
VMEM_LIMIT_BYTES = 56 * 1024 * 1024

POOL_HIST = 16
POOL_PAD = SUBLANES
POOL_BASE = POOL_PAD + POOL_HIST
POOL_LEVELS = 3
ATT_ROWS = 256
PAIRS_PER_GROUP = EXPERTS_PER_GROUP * (EXPERTS_PER_GROUP - 1) // 2
N_CLASSES = N_GROUPS * PAIRS_PER_GROUP
MOE_ROWS = 256
RANK_ROWS = (1536, 1024, 512)
ROUTE_LANES = 128
ROW_TILE = MOE_ROWS * SUBLANES

_f32 = jnp.float32
_bf16 = jnp.bfloat16


def _dot(a, b):
    return jnp.dot(a, b, preferred_element_type=_f32)


def _rms_scale(x):
    return lax.rsqrt(jnp.mean(x * x, axis=-1, keepdims=True) + EPS)


def _load_row_tiles(ref, rows, row0=0):
    return jnp.concatenate(
        [ref[pl.ds(row0 * SUBLANES + s, rows, stride=SUBLANES), :] for s in range(SUBLANES)], axis=-1)


def _store_row_tiles(ref, val, rows, row0=0):
    for s in range(SUBLANES):
        ref[pl.ds(row0 * SUBLANES + s, rows, stride=SUBLANES), :] = val[:, s * LANES:(s + 1) * LANES]


def _const_spec(shape):
    return pl.BlockSpec(shape, lambda *_: (0,) * len(shape), pipeline_mode=pl.Buffered(1))


def _route_rows(logits):
    lane = lax.broadcasted_iota(jnp.int32, (1, ROUTE_LANES), 1)
    lane_f = lane.astype(_f32)
    neg = jnp.float32(-jnp.inf)
    far = jnp.float32(ROUTE_LANES)
    cl = jnp.where(lane < N_GROUPS, logits, neg)
    cmax = jnp.max(cl, axis=-1, keepdims=True)
    grp = jnp.min(jnp.where(cl == cmax, lane_f, far), axis=-1, keepdims=True)
    fine_grp = ((lane - N_GROUPS) >> 3).astype(_f32)
    fl = jnp.where(fine_grp == grp, logits, neg)
    m1 = jnp.max(fl, axis=-1, keepdims=True)
    i1 = jnp.min(jnp.where(fl == m1, lane_f, far), axis=-1, keepdims=True)
    fl2 = jnp.where(lane_f == i1, neg, fl)
    m2 = jnp.max(fl2, axis=-1, keepdims=True)
    i2 = jnp.min(jnp.where(fl2 == m2, lane_f, far), axis=-1, keepdims=True)
    first_lane = N_GROUPS + EXPERTS_PER_GROUP * grp
    la = jnp.minimum(i1, i2) - first_lane
    lb = jnp.maximum(i1, i2) - first_lane
    return PAIRS_PER_GROUP * grp + la * (2 * EXPERTS_PER_GROUP - 1 - la) * 0.5 + (lb - la - 1.0)


def _mixer_kernel(x_ref, k0_ref, v0_ref, p0_ref, gmix_ref, win_ref, wpool_ref, pscale_ref, gq_ref, gk_ref,
                  bd_ref, bias_ref, wbp_ref, wba_ref, wout_ref, gffn_ref, wroute_ref, broute_ref,
                  *rest, tq, n_sub, n_steps, pos0):
    x1_ref, route_ref, kout_ref, vout_ref, pout_ref, cnt_ref, kt_buf, v_buf, u_buf, s_buf = rest[-10:]
    t = pl.program_id(1)

    @pl.when(jnp.logical_and(pl.program_id(0) == 0, t == 0))
    def _init_counts():
        cnt_ref[...] = jnp.zeros_like(cnt_ref)

    att_rows = bias_ref.shape[1]
    band = BAND_PREV + att_rows
    step_rows = n_sub * tq
    d_model = x_ref.shape[-1]
    lane = lax.broadcasted_iota(jnp.int32, (1, LANES), 1)
    even = lane < HEAD_DIM
    col = lax.broadcasted_iota(jnp.int32, (1, band), 1)
    row = lax.broadcasted_iota(jnp.int32, (tq, 1), 0)

    @pl.when(t == 0)
    def _init_history():
        kt_buf[:, 0:BAND_PREV] = k0_ref[0].T.astype(_bf16)
        v_buf[0:BAND_PREV, :] = v0_ref[0].astype(_bf16)
        u_buf[0:POOL_PAD, :] = jnp.zeros((POOL_PAD, POOL_WIDTH), _f32)
        u_buf[POOL_PAD:POOL_BASE, :] = p0_ref[0]
        s_buf[:, :, 0:POOL_PAD, :] = jnp.zeros(s_buf.shape[:2] + (POOL_PAD, POOL_WIDTH), _f32)

    def pool_phase(j, st):
        r0 = j * tq
        x = x_ref[0, r0:r0 + tq, :]
        h = (x * _rms_scale(x) * gmix_ref[...]).astype(_bf16)
        u = _dot(h, win_ref[:, 0:POOL_WIDTH])
        u_buf[POOL_BASE + r0:POOL_BASE + r0 + tq, :] = u
        pos1 = pos0 + t * step_rows + r0 + row + 1
        w0 = POOL_PAD + r0
        span = POOL_HIST + tq
        diffs = []
        for g, w in enumerate(POOL_WINDOWS):
            sl = slice(g * POOL_GROUP, (g + 1) * POOL_GROUP)
            acc = u_buf[w0:w0 + span, sl] + u_buf[w0 - 1:w0 - 1 + span, sl]
            shift = 2
            while shift < w:
                level = s_buf.at[j, shift.bit_length() - 2]
                level[POOL_PAD:POOL_PAD + span, sl] = acc
                acc = acc + level[POOL_PAD - shift:POOL_PAD - shift + span, sl]
                shift *= 2
            cnt = jnp.minimum(pos1, w).astype(_f32)
            diffs.append((acc[POOL_HIST:, :] / cnt - u[:, sl]).astype(_bf16))
        pooled = [_dot(jnp.concatenate(diffs[2 * i:2 * i + 2], axis=-1), wpool_ref[i])
                  for i in range(len(POOL_WINDOWS) // 2)]
        st["x"], st["h"] = x, h
        st["a"] = (jnp.concatenate(pooled, axis=-1) * pscale_ref[...]).astype(_bf16)

    def qkv_phase(j, st):
        r0 = j * tq
        h = st["h"]
        def head_mean_sq(z):
            sq = (z * z).astype(_bf16)
            half = ATT_WIDTH // 2
            return jnp.concatenate([_dot(sq[:, :half], bd_ref[...]), _dot(sq[:, half:], bd_ref[...])], axis=-1)

        q = _dot(h, win_ref[:, POOL_WIDTH:POOL_WIDTH + ATT_WIDTH])
        st["qn"] = (q * lax.rsqrt(head_mean_sq(q) + EPS) * gq_ref[...]).astype(_bf16)
        k = _dot(h, win_ref[:, POOL_WIDTH + ATT_WIDTH:POOL_WIDTH + 2 * ATT_WIDTH])
        kn = k * lax.rsqrt(head_mean_sq(k) + EPS) * gk_ref[...]
        v = _dot(h, win_ref[:, POOL_WIDTH + 2 * ATT_WIDTH:POOL_WIDTH + 3 * ATT_WIDTH])
        kout_ref[0, r0:r0 + tq, :] = kn
        vout_ref[0, r0:r0 + tq, :] = v
        kt_buf[:, BAND_PREV + r0:BAND_PREV + r0 + tq] = kn.T.astype(_bf16)
        v_buf[BAND_PREV + r0:BAND_PREV + r0 + tq, :] = v.astype(_bf16)

    def attention_phase(j, st):
        qn = st["qn"]
        heads = []
        for p in range(N_HEADS // 2):
            sl = slice(p * LANES, (p + 1) * LANES)
            blocks = []
            for r in range(tq // att_rows):
                k0 = j * tq + r * att_rows
                qp = qn[r * att_rows:(r + 1) * att_rows, sl]
                ktp = kt_buf[sl, k0:k0 + band]
                vp = v_buf[k0:k0 + band, sl]
                started = col >= BAND_PREV - (pos0 + t * step_rows + k0)
                acc = None
                inv = []
                for half in range(2):
                    keep = even if half == 0 else jnp.logical_not(even)
                    qh = jnp.where(keep, qp, jnp.zeros_like(qp))
                    vh = jnp.where(keep, vp, jnp.zeros_like(vp))
                    s = _dot(qh, ktp) + bias_ref[2 * p + half]
                    s = jnp.where(started, s, MASKED)
                    e = jnp.exp2(s - jnp.max(s, axis=-1, keepdims=True))
                    inv.append(1.0 / jnp.sum(e, axis=-1, keepdims=True))
                    part = _dot(e.astype(_bf16), vh)
                    acc = part if acc is None else acc + part
                blocks.append(acc * jnp.where(even, inv[0], inv[1]))
            heads.append(jnp.concatenate(blocks, axis=0))
        st["o"] = jnp.concatenate(heads, axis=-1).astype(_bf16)

    def merge_phase(j, st):
        r0 = j * tq
        x, h = st["x"], st["h"]
        gate_off = POOL_WIDTH + 3 * ATT_WIDTH
        ga = _dot(h, win_ref[:, gate_off:gate_off + d_model])
        gb = _dot(h, win_ref[:, gate_off + d_model:gate_off + 2 * d_model])
        m = (jax.nn.sigmoid(ga) * _dot(st["a"], wbp_ref[...])
             + jax.nn.sigmoid(gb) * _dot(st["o"], wba_ref[...]))
        x1 = x + _dot(m.astype(_bf16), wout_ref[...])
        _store_row_tiles(x1_ref, x1, tq, r0)
        h2 = x1 * _rms_scale(x1) * gffn_ref[...]
        h2_hi = h2.astype(_bf16)
        h2_lo = (h2 - h2_hi.astype(_f32)).astype(_bf16)
        hi_both = _dot(h2_hi, wroute_ref[...])
        logits = (hi_both[:, 0:ROUTE_LANES] + hi_both[:, ROUTE_LANES:]
                  + _dot(h2_lo, wroute_ref[:, 0:ROUTE_LANES]) + broute_ref[...])
        cls = jnp.broadcast_to(_route_rows(logits), (tq, ROUTE_LANES))
        route_ref[r0:r0 + tq, :] = cls
        cnt_ref[...] += jnp.sum(jnp.where(cls == lane.astype(_f32), 1.0, 0.0), axis=0, keepdims=True)

    states = [{} for _ in range(n_sub)]
    for phase in (pool_phase, qkv_phase, attention_phase, merge_phase):
        for j in range(n_sub):
            phase(j, states[j])

    pout_ref[0] = u_buf[POOL_BASE + step_rows - POOL_STATE:POOL_BASE + step_rows, :]
    if n_steps > 1:
        chunk = min(step_rows, BAND_PREV)
        for c in range(BAND_PREV // chunk):
            dst = slice(c * chunk, (c + 1) * chunk)
            src = slice(step_rows + c * chunk, step_rows + (c + 1) * chunk)
            kt_buf[:, dst] = kt_buf[:, src]
            v_buf[dst, :] = v_buf[src, :]
        u_buf[POOL_PAD:POOL_BASE, :] = u_buf[POOL_PAD + step_rows:POOL_BASE + step_rows, :]


def _mixer_call(x, k0, v0, p0, wts, shared, *, total_rows, tq, n_sub, pos0, row_off, name):
    bsz, t_len, d_model = x.shape
    step_rows = n_sub * tq
    n_steps = t_len // step_rows
    tail = min(BAND_PREV, t_len)
    assert t_len % step_rows == 0 and tail % step_rows == 0 and row_off % step_rows == 0
    tail_steps = tail // step_rows
    att_rows = min(ATT_ROWS, tq)
    assert tq % att_rows == 0
    band = BAND_PREV + att_rows
    blk_off = row_off // step_rows

    def tail_map(b, t):
        return (b, jnp.maximum(t - (n_steps - tail_steps), 0), 0)

    in_specs = [
        pl.BlockSpec((1, step_rows, d_model), lambda b, t: (b, t, 0)),
        pl.BlockSpec((1, BAND_PREV, ATT_WIDTH), lambda b, t: (b, 0, 0), pipeline_mode=pl.Buffered(1)),
        pl.BlockSpec((1, BAND_PREV, ATT_WIDTH), lambda b, t: (b, 0, 0), pipeline_mode=pl.Buffered(1)),
        pl.BlockSpec((1, POOL_HIST, POOL_WIDTH), lambda b, t: (b, 0, 0)),
        _const_spec((1, d_model)),
        _const_spec(wts["w_in"].shape),
        _const_spec(wts["w_pool"].shape),
        _const_spec((1, POOL_WIDTH)),
        _const_spec((1, ATT_WIDTH)),
        _const_spec((1, ATT_WIDTH)),
        _const_spec((ATT_WIDTH // 2, ATT_WIDTH // 2)),
        _const_spec((N_HEADS, att_rows, band)),
        _const_spec(wts["w_br_pool"].shape),
        _const_spec(wts["w_br_att"].shape),
        _const_spec(wts["w_out"].shape),
        _const_spec((1, d_model)),
        _const_spec((d_model, 2 * ROUTE_LANES)),
        _const_spec((1, ROUTE_LANES)),
    ]
    operands = [x, k0, v0, p0, wts["g_mix"], wts["w_in"], wts["w_pool"], wts["pool_scale"], wts["g_q"],
                wts["g_k"], wts["bd"], wts["bias"][:, :att_rows, :band], wts["w_br_pool"], wts["w_br_att"],
                wts["w_out"], wts["g_ffn"], wts["w_route"], wts["b_route"]]
    aliases = {}
    if shared is not None:
        aliases = {len(operands): 0, len(operands) + 1: 1}
        in_specs += [pl.BlockSpec(memory_space=pl.ANY)] * 2
        operands += list(shared)
    out_specs = [
        pl.BlockSpec((step_rows * SUBLANES, LANES), lambda b, t: (blk_off + b * n_steps + t, 0)),
        pl.BlockSpec((step_rows, ROUTE_LANES), lambda b, t: (blk_off + b * n_steps + t, 0)),
        pl.BlockSpec((1, step_rows, ATT_WIDTH), tail_map),
        pl.BlockSpec((1, step_rows, ATT_WIDTH), tail_map),
        pl.BlockSpec((1, POOL_STATE, POOL_WIDTH), lambda b, t: (b, 0, 0)),
        pl.BlockSpec((SUBLANES, LANES), lambda b, t: (0, 0)),
    ]
    out_shape = [
        jax.ShapeDtypeStruct((total_rows * SUBLANES, LANES), _f32),
        jax.ShapeDtypeStruct((total_rows, ROUTE_LANES), _f32),
        jax.ShapeDtypeStruct((bsz, tail, ATT_WIDTH), _f32),
        jax.ShapeDtypeStruct((bsz, tail, ATT_WIDTH), _f32),
        jax.ShapeDtypeStruct((bsz, POOL_STATE, POOL_WIDTH), _f32),
        jax.ShapeDtypeStruct((SUBLANES, LANES), _f32),
    ]
    kern = functools.partial(_mixer_kernel, tq=tq, n_sub=n_sub, n_steps=n_steps, pos0=pos0)
    return pl.pallas_call(
        kern,
        grid=(bsz, n_steps),
        in_specs=in_specs,
        out_specs=out_specs,
        out_shape=out_shape,
        scratch_shapes=[
            pltpu.VMEM((ATT_WIDTH, BAND_PREV + step_rows), _bf16),
            pltpu.VMEM((BAND_PREV + step_rows, ATT_WIDTH), _bf16),
            pltpu.VMEM((POOL_BASE + step_rows, POOL_WIDTH), _f32),
            pltpu.VMEM((n_sub, POOL_LEVELS, POOL_BASE + tq, POOL_WIDTH), _f32),
        ],
        input_output_aliases=aliases,
        compiler_params=pltpu.CompilerParams(
            dimension_semantics=("arbitrary", "arbitrary"), vmem_limit_bytes=VMEM_LIMIT_BYTES),
        name=name,
    )(*operands)


def _lane_cumsum(x):
    lane = lax.broadcasted_iota(jnp.int32, x.shape, 1)
    shift = 1
    while shift < LANES:
        x = x + jnp.where(lane >= shift, pltpu.roll(x, shift, axis=1), 0.0)
        shift *= 2
    return x


def _rank_kernel(zero_end_ref, route_ref, counts_ref, dest_ref, xs_hbm, base, before, zbuf, zsem):
    i = pl.program_id(0)
    rows = route_ref.shape[0]
    lane = lax.broadcasted_iota(jnp.int32, (1, LANES), 1).astype(_f32)
    oh = jnp.where(route_ref[...] == lane, 1.0, 0.0)

    def zero_fill(c, carry, *, wait):
        end = zero_end_ref[c]

        @pl.when(end > 0)
        def _():
            copy = pltpu.make_async_copy(
                zbuf, xs_hbm.at[pl.ds((end - MOE_ROWS) * SUBLANES, ROW_TILE), :], zsem.at[0])
            copy.wait() if wait else copy.start()
        return carry

    @pl.when(i == 0)
    def _():
        zbuf[...] = jnp.zeros_like(zbuf)
        lax.fori_loop(0, N_CLASSES, functools.partial(zero_fill, wait=False), 0)
        ri = lax.broadcasted_iota(jnp.int32, (rows, rows), 0)
        ci = lax.broadcasted_iota(jnp.int32, (rows, rows), 1)
        before[...] = jnp.where(ri < ci, 1.0, 0.0).astype(_bf16)
        cnt = counts_ref[...]
        padded = jnp.floor((cnt + (MOE_ROWS - 1)) * (1.0 / MOE_ROWS)) * MOE_ROWS
        first = _lane_cumsum(padded) - padded
        ri = lax.broadcasted_iota(jnp.int32, (LANES, LANES), 0)
        ci = lax.broadcasted_iota(jnp.int32, (LANES, LANES), 1)
        first_col = jnp.sum(jnp.where(ri == ci, first[0:1, :], 0.0), axis=-1, keepdims=True)
        base[...] = jnp.broadcast_to(first_col, base.shape)

    oh_t = oh.T
    earlier = _dot(oh_t.astype(_bf16), before[...])
    slot_row = jnp.sum(oh_t * (base[:, 0:1] + earlier), axis=0, keepdims=True)
    base[...] += jnp.sum(oh_t, axis=1, keepdims=True)
    dest_ref[0] = slot_row.astype(jnp.int32)

    @pl.when(i == pl.num_programs(0) - 1)
    def _():
        lax.fori_loop(0, N_CLASSES, functools.partial(zero_fill, wait=True), 0)


def _rank_call(route_buf, counts, zero_end, n_blocks):
    total = route_buf.shape[0]
    rows = next(r for r in RANK_ROWS if total % r == 0)
    n_tiles = total // rows
    grid_spec = pltpu.PrefetchScalarGridSpec(
        num_scalar_prefetch=1,
        grid=(n_tiles,),
        in_specs=[pl.BlockSpec((rows, ROUTE_LANES), lambda i, *_: (i, 0)),
                  pl.BlockSpec((SUBLANES, LANES), lambda i, *_: (0, 0))],
        out_specs=[pl.BlockSpec((1, 1, rows), lambda i, *_: (i, 0, 0)),
                   pl.BlockSpec(memory_space=pl.ANY)],
        scratch_shapes=[pltpu.VMEM((LANES, LANES), _f32), pltpu.VMEM((rows, rows), _bf16),
                        pltpu.VMEM((ROW_TILE, LANES), _f32), pltpu.SemaphoreType.DMA((1,))],
    )
    return pl.pallas_call(
        _rank_kernel,
        grid_spec=grid_spec,
        out_shape=[jax.ShapeDtypeStruct((n_tiles, 1, rows), jnp.int32),
                   jax.ShapeDtypeStruct((n_blocks * ROW_TILE, LANES), _f32)],
        compiler_params=pltpu.CompilerParams(dimension_semantics=("arbitrary",)),
        name="rank",
    )(zero_end, route_buf, counts)


def _block_tables(counts, n_blocks):
    cnt = counts[0, :N_CLASSES].astype(jnp.int32)
    padded = (cnt + MOE_ROWS - 1) // MOE_ROWS * MOE_ROWS
    pend = jnp.cumsum(padded)
    nb = pend[-1] // MOE_ROWS
    blk = jnp.minimum(jnp.arange(n_blocks, dtype=jnp.int32), nb - 1)
    blk_cls = jnp.sum(pend[None, :] <= (blk * MOE_ROWS)[:, None], axis=1, dtype=jnp.int32)
    blk_cls = jnp.minimum(blk_cls, N_CLASSES - 1)
    grp = blk_cls // PAIRS_PER_GROUP
    pair = blk_cls % PAIRS_PER_GROUP
    firsts = jnp.arange(1, EXPERTS_PER_GROUP, dtype=jnp.int32)
    pair_start = firsts * (2 * EXPERTS_PER_GROUP - 1 - firsts) // 2
    la = jnp.sum(pair[:, None] >= pair_start[None, :], axis=1, dtype=jnp.int32)
    lb = pair - la * (2 * EXPERTS_PER_GROUP - 1 - la) // 2 + la + 1
    zero_end = jnp.where(cnt > 0, pend, 0).astype(jnp.int32)
    return {
        "ea": grp * EXPERTS_PER_GROUP + la, "eb": grp * EXPERTS_PER_GROUP + lb,
        "nb": nb.reshape(1).astype(jnp.int32), "zero_end": zero_end,
    }


N_DISPATCH_BUFS = 3
N_WEIGHT_SLABS = 256


def _dispatch_kernel(idx_ref, wg_ref, wu_ref, wd_ref, x1_hbm, xs_in_hbm, xs_hbm, wg_out, wu_out, wd_out,
                     tbuf, lsem, ssem, *, n_slabs):
    del xs_in_hbm
    i = pl.program_id(0)
    n = pl.num_programs(0)
    slot = i % N_DISPATCH_BUFS

    @pl.when(i < n_slabs)
    def _round_weights():
        wg_out[...] = wg_ref[...].astype(_bf16)
        wu_out[...] = wu_ref[...].astype(_bf16)
        wd_out[...] = wd_ref[...].astype(_bf16)

    def load_copy(tile, s):
        return pltpu.make_async_copy(x1_hbm.at[pl.ds(tile * ROW_TILE, ROW_TILE), :], tbuf.at[s], lsem.at[s])

    def wait_scatter(s):
        pltpu.make_async_copy(tbuf.at[s], xs_hbm.at[pl.ds(0, ROW_TILE), :], ssem.at[s]).wait()

    @pl.when(i == 0)
    def _():
        load_copy(0, 0).start()

    @pl.when(i >= N_DISPATCH_BUFS - 1)
    def _():
        wait_scatter((i + 1) % N_DISPATCH_BUFS)

    @pl.when(i + 1 < n)
    def _():
        load_copy(i + 1, (i + 1) % N_DISPATCH_BUFS).start()

    load_copy(i, slot).wait()

    def body(r, carry):
        pltpu.make_async_copy(
            tbuf.at[slot, pl.ds(r * SUBLANES, SUBLANES), :],
            xs_hbm.at[pl.ds(idx_ref[0, 0, r] * SUBLANES, SUBLANES), :],
            ssem.at[slot]).start()
        return carry
    lax.fori_loop(0, MOE_ROWS, body, 0, unroll=8)

    @pl.when(i == n - 1)
    def _():
        @pl.when(n > 1)
        def _():
            wait_scatter((i + N_DISPATCH_BUFS - 1) % N_DISPATCH_BUFS)
        wait_scatter(slot)


def _dispatch_call(x1_buf, dest, xs_buf, expert_weights):
    n_tiles = dest.shape[0]
    n_slabs = min(N_WEIGHT_SLABS, 1 << (n_tiles.bit_length() - 1))
    flat = [w.reshape(-1, w.shape[-1]) for w in expert_weights]
    slab_specs = []
    for w in flat:
        assert w.shape[0] % n_slabs == 0
        slab_specs.append(pl.BlockSpec((w.shape[0] // n_slabs, w.shape[1]),
                                       lambda i, *_: (jnp.minimum(i, n_slabs - 1), 0)))
    in_specs = ([pl.BlockSpec((1, 1, MOE_ROWS), lambda i: (i, 0, 0), memory_space=pltpu.SMEM)] + slab_specs
                + [pl.BlockSpec(memory_space=pl.ANY), pl.BlockSpec(memory_space=pl.ANY)])
    outs = pl.pallas_call(
        functools.partial(_dispatch_kernel, n_slabs=n_slabs),
        grid=(n_tiles,),
        in_specs=in_specs,
        out_specs=[pl.BlockSpec(memory_space=pl.ANY)] + slab_specs,
        out_shape=[jax.ShapeDtypeStruct(xs_buf.shape, _f32)]
        + [jax.ShapeDtypeStruct(w.shape, _bf16) for w in flat],
        scratch_shapes=[
            pltpu.VMEM((N_DISPATCH_BUFS, ROW_TILE, LANES), _f32),
            pltpu.SemaphoreType.DMA((N_DISPATCH_BUFS,)),
            pltpu.SemaphoreType.DMA((N_DISPATCH_BUFS,)),
        ],
        input_output_aliases={len(in_specs) - 1: 0},
        compiler_params=pltpu.CompilerParams(
            dimension_semantics=("arbitrary",), vmem_limit_bytes=VMEM_LIMIT_BYTES),
        name="dispatch",
    )(dest, *flat, x1_buf, xs_buf)
    return outs[0], [o.reshape(w.shape) for o, w in zip(outs[1:], expert_weights)]


MOE_PAIR = 2


def _moe_kernel(ea_ref, eb_ref, nb_ref, xs_ref, gffn_ref, wr_ref, br_ref, *rest):
    out_ref = rest[-1]
    step = pl.program_id(0)
    nb = nb_ref[0]
    lane = lax.broadcasted_iota(jnp.int32, (1, ROUTE_LANES), 1)

    def load_phase(k, st):
        x = _load_row_tiles(xs_ref, MOE_ROWS, k * MOE_ROWS)
        st["x"] = x
        st["h"] = (x * _rms_scale(x) * gffn_ref[...]).astype(_bf16)

    def weight_phase(k, st):
        ea = ea_ref[step * MOE_PAIR + k]
        eb = eb_ref[step * MOE_PAIR + k]
        grp = ea // EXPERTS_PER_GROUP
        logits = _dot(st["h"], wr_ref[...]) + br_ref[...]

        def pick(col):
            return jnp.sum(jnp.where(lane == col, logits, 0.0), axis=-1, keepdims=True)

        coarse = lane < N_GROUPS
        cmax = jnp.max(jnp.where(coarse, logits, -jnp.inf), axis=-1, keepdims=True)
        csum = jnp.sum(jnp.where(coarse, jnp.exp(logits - cmax), 0.0), axis=-1, keepdims=True)
        gp = jnp.exp(pick(grp) - cmax) / csum
        fa = pick(N_GROUPS + ea)
        fb = pick(N_GROUPS + eb)
        fmax = jnp.maximum(fa, fb)
        pa = jnp.exp(fa - fmax)
        pb = jnp.exp(fb - fmax)
        st["wa"] = gp * pa / (pa + pb)
        st["wb"] = gp * pb / (pa + pb)

    def expert_phase(k, st):
        wga, wua, wda, wgb, wub, wdb = rest[6 * k:6 * k + 6]
        h = st["h"]

        def expert(wg_ref, wu_ref, wd_ref):
            g = _dot(h, wg_ref[0])
            u = _dot(h, wu_ref[0])
            return _dot((jax.nn.silu(g) * u).astype(_bf16), wd_ref[0])

        y = st["wa"] * expert(wga, wua, wda) + st["wb"] * expert(wgb, wub, wdb)
        _store_row_tiles(out_ref, st["x"] + y, MOE_ROWS, k * MOE_ROWS)

    def run(n_live):
        states = [{} for _ in range(n_live)]
        for phase in (load_phase, weight_phase, expert_phase):
            for k in range(n_live):
                phase(k, states[k])

    first = step * MOE_PAIR
    for n_live in range(MOE_PAIR, 0, -1):
        cond = (first + n_live <= nb) if n_live == MOE_PAIR else (first + n_live == nb)
        pl.when(cond)(functools.partial(run, n_live))


def _moe_call(xs_buf, tables, wts, expert_weights, n_blocks):
    w_gate, w_up, w_down = expert_weights
    d_model = w_gate.shape[1]
    d_exp = w_gate.shape[2]
    assert n_blocks % MOE_PAIR == 0

    def used_step(s, ea, eb, nb):
        return (jnp.minimum(s, (nb[0] - 1) // MOE_PAIR), 0)

    def expert_of(which, k):
        def index_map(s, ea, eb, nb):
            blk = jnp.minimum(s * MOE_PAIR + k, nb[0] - 1)
            return ((ea, eb)[which][blk], 0, 0)
        return index_map

    weight_specs = []
    weight_args = []
    for k in range(MOE_PAIR):
        for which in range(2):
            weight_specs += [pl.BlockSpec((1, d_model, d_exp), expert_of(which, k)),
                             pl.BlockSpec((1, d_model, d_exp), expert_of(which, k)),
                             pl.BlockSpec((1, d_exp, d_model), expert_of(which, k))]
            weight_args += [w_gate, w_up, w_down]

    grid_spec = pltpu.PrefetchScalarGridSpec(
        num_scalar_prefetch=3,
        grid=(n_blocks // MOE_PAIR,),
        in_specs=[
            pl.BlockSpec((MOE_PAIR * ROW_TILE, LANES), used_step),
            pl.BlockSpec((1, d_model), lambda s, *_: (0, 0)),
            pl.BlockSpec((d_model, ROUTE_LANES), lambda s, *_: (0, 0)),
            pl.BlockSpec((1, ROUTE_LANES), lambda s, *_: (0, 0)),
        ] + weight_specs,
        out_specs=pl.BlockSpec((MOE_PAIR * ROW_TILE, LANES), used_step),
    )
    return pl.pallas_call(
        _moe_kernel,
        grid_spec=grid_spec,
        out_shape=jax.ShapeDtypeStruct(xs_buf.shape, _f32),
        compiler_params=pltpu.CompilerParams(
            dimension_semantics=("arbitrary",), vmem_limit_bytes=VMEM_LIMIT_BYTES),
        name="moe",
    )(tables["ea"], tables["eb"], tables["nb"], xs_buf, wts["g_ffn"], wts["w_route_hi"], wts["b_route"],
      *weight_args)


def _ple_kernel(idx_ref, idx_nxt_ref, p_ref, gple_ref, wgate_ref, wproj_ref, x2s_hbm, out_ref, buf0, buf1, gsem):
    i = pl.program_id(0)
    n = pl.num_programs(0)

    def row_copy(idx_ref, r, off, buf, s):
        return pltpu.make_async_copy(
            x2s_hbm.at[pl.ds(idx_ref[0, 0, off + r] * SUBLANES, SUBLANES), :],
            buf.at[pl.ds(r * SUBLANES, SUBLANES), :],
            gsem.at[s])

    def start_gather(idx_ref, off, buf, s):
        for r in range(MOE_ROWS):
            row_copy(idx_ref, r, off, buf, s).start()

    def wait_gather(buf, s):
        pltpu.make_async_copy(x2s_hbm.at[pl.ds(0, ROW_TILE), :], buf, gsem.at[s]).wait()

    def tile(buf, half):
        rows = pl.ds(half * MOE_ROWS, MOE_ROWS)
        x2 = _load_row_tiles(buf, MOE_ROWS)
        hn = (x2 * _rms_scale(x2) * gple_ref[...]).astype(_bf16)
        gate = jax.nn.sigmoid(_dot(hn, wgate_ref[...]))
        out_ref[rows, :] = x2 + _dot(p_ref[rows, :].astype(_bf16), wproj_ref[...]) * gate

    @pl.when(i == 0)
    def _():
        def body(r, carry):
            row_copy(idx_ref, r, 0, buf0, 0).start()
            return carry
        lax.fori_loop(0, MOE_ROWS, body, 0, unroll=8)

    start_gather(idx_ref, MOE_ROWS, buf1, 1)
    wait_gather(buf0, 0)
    tile(buf0, 0)
    start_gather(idx_nxt_ref, 0, buf0, 0)
    wait_gather(buf1, 1)
    tile(buf1, 1)

    @pl.when(i == n - 1)
    def _():
        wait_gather(buf0, 0)


def _ple_call(x2s_buf, dest, p, wts, *, row_off, name):
    n_rows, d_ple = p.shape
    d_model = wts["w_ple_gate"].shape[0]
    step_rows = 2 * MOE_ROWS
    n_steps = n_rows // step_rows
    blk_off = row_off // step_rows
    last = blk_off + n_steps - 1
    dest = dest.reshape(-1, 1, step_rows)
    idx_spec = functools.partial(pl.BlockSpec, (1, 1, step_rows), memory_space=pltpu.SMEM)
    return pl.pallas_call(
        _ple_kernel,
        grid=(n_steps,),
        in_specs=[
            idx_spec(index_map=lambda i: (blk_off + i, 0, 0)),
            idx_spec(index_map=lambda i: (jnp.minimum(blk_off + i + 1, last), 0, 0)),
            pl.BlockSpec((step_rows, d_ple), lambda i: (i, 0)),
            _const_spec((1, d_model)),
            _const_spec((d_model, d_model)),
            _const_spec((d_ple, d_model)),
            pl.BlockSpec(memory_space=pl.ANY),
        ],
        out_specs=pl.BlockSpec((step_rows, d_model), lambda i: (i, 0)),
        out_shape=jax.ShapeDtypeStruct((n_rows, d_model), _f32),
        scratch_shapes=[pltpu.VMEM((ROW_TILE, LANES), _f32), pltpu.VMEM((ROW_TILE, LANES), _f32),
                        pltpu.SemaphoreType.DMA((2,))],
        compiler_params=pltpu.CompilerParams(
            dimension_semantics=("arbitrary",), vmem_limit_bytes=VMEM_LIMIT_BYTES),
        name=name,
    )(dest, dest, p, wts["g_ple"], wts["w_ple_gate"], wts["w_ple_proj"], x2s_buf)


def _band_bias(table, tq):
    band = BAND_PREV + tq
    n_heads, n_rel = table.shape
    n_far = band - 1 - REL_CLIP
    length = band + tq
    n_near = length - n_far - n_rel
    tab = table.astype(_f32)
    f = jnp.concatenate([jnp.broadcast_to(tab[:, n_rel - 1:], (n_heads, n_far)), tab[:, ::-1],
                         jnp.broadcast_to(tab[:, :1], (n_heads, n_near))], axis=1)
    skew = jnp.tile(f, (1, tq))[:, :tq * (length - 1)].reshape(n_heads, tq, length - 1)
    bias = skew[:, :, tq - 1:tq - 1 + band]
    kc = lax.broadcasted_iota(jnp.int32, (tq, band), 1) // CHUNK
    qc = lax.broadcasted_iota(jnp.int32, (tq, band), 0) // CHUNK
    in_band = (kc >= qc) & (kc <= qc + N_PREV_CHUNKS)
    return jnp.where(in_band[None], bias, MASKED)


def _pair_block_diag(w):
    n2, c, _ = w.shape
    w = w.reshape(n2 // 2, 2, c, c)
    zero = jnp.zeros_like(w[:, 0])
    top = jnp.concatenate([w[:, 0], zero], axis=2)
    bottom = jnp.concatenate([zero, w[:, 1]], axis=2)
    return jnp.concatenate([top, bottom], axis=1)


def _layer_weights(l, tq, g_mix, w_in, w_pool, pool_scale, g_q, g_k, rel_table, w_br_pool, w_br_att, w_out,
                   g_ffn, w_coarse, b_coarse, w_fine, b_fine, w_gate_e, w_up_e, w_down_e, g_ple, w_ple_gate,
                   w_ple_proj):
    d_model = w_in.shape[1]
    pad = ROUTE_LANES - N_GROUPS - N_EXPERTS
    head_of = jnp.arange(ATT_WIDTH // 2, dtype=jnp.int32) // HEAD_DIM
    head_sum = jnp.where(head_of[:, None] == head_of[None, :], 1.0 / HEAD_DIM, 0.0)
    w_route = jnp.concatenate([w_coarse[l], w_fine[l], jnp.zeros((d_model, pad), _f32)], axis=1)
    w_route_hi = w_route.astype(_bf16)
    w_route_lo = (w_route - w_route_hi.astype(_f32)).astype(_bf16)
    b_route = jnp.concatenate([b_coarse[l], b_fine[l], jnp.zeros((pad,), _f32)])
    return {
        "g_mix": g_mix[l].reshape(1, d_model),
        "w_in": w_in[l].astype(_bf16),
        "w_pool": _pair_block_diag(w_pool[l].astype(_bf16)),
        "pool_scale": pool_scale[l].reshape(1, POOL_WIDTH),
        "g_q": (jnp.tile(g_q[l], N_HEADS) * (HEAD_DIM ** -0.5 * LOG2E)).reshape(1, ATT_WIDTH),
        "g_k": jnp.tile(g_k[l], N_HEADS).reshape(1, ATT_WIDTH),
        "bd": head_sum.astype(_bf16),
        "bias": _band_bias(rel_table[l] * LOG2E, tq),
        "w_br_pool": w_br_pool[l].astype(_bf16),
        "w_br_att": w_br_att[l].astype(_bf16),
        "w_out": w_out[l].astype(_bf16),
        "g_ffn": g_ffn[l].reshape(1, d_model),
        "w_route": jnp.concatenate([w_route_hi, w_route_lo], axis=1),
        "b_route": b_route.reshape(1, ROUTE_LANES),
        "w_route_hi": w_route_hi,
        "experts_f32": (w_gate_e[l], w_up_e[l], w_down_e[l]),
        "g_ple": g_ple[l].reshape(1, d_model),
        "w_ple_gate": w_ple_gate[l].astype(_bf16),
        "w_ple_proj": w_ple_proj[l].astype(_bf16),
    }


def _pick_tile(t_len, want):
    tq = min(want, t_len)
    assert t_len % tq == 0 and tq % CHUNK == 0 and BAND_PREV % tq == 0
    return tq


def kernel(x_prompt, x_sample, cache_k, cache_v, state_pool, p_prompt, p_sample, g_mix, w_in, w_pool, pool_scale, g_q, g_k, rel_table, w_br_pool, w_br_att, w_out, g_ffn, w_coarse, b_coarse, w_fine, b_fine, w_gate_e, w_up_e, w_down_e, g_ple, w_ple_gate, w_ple_proj):
    depth = w_in.shape[0]
    bp, tp, d_model = x_prompt.shape
    bs, ts, _ = x_sample.shape
    assert d_model == SUBLANES * LANES
    tq_p = _pick_tile(tp, 512)
    tq_s = _pick_tile(ts, 256)
    sub_p = 2 if tp % (2 * tq_p) == 0 and BAND_PREV % (2 * tq_p) == 0 else 1
    rows_p, rows_s = bp * tp, bs * ts
    total = rows_p + rows_s
    assert rows_p % (2 * MOE_ROWS) == 0 and rows_s % (2 * MOE_ROWS) == 0 and total % RANK_ROWS[-1] == 0
    n_blocks = -(-(total + N_CLASSES * (MOE_ROWS - 1)) // MOE_ROWS)
    n_blocks = -(-n_blocks // MOE_PAIR) * MOE_PAIR

    xp, xs = x_prompt, x_sample
    outs = [[] for _ in range(6)]
    for l in range(depth):
        wts = _layer_weights(l, min(ATT_ROWS, max(tq_p, tq_s)), g_mix, w_in, w_pool, pool_scale, g_q, g_k, rel_table, w_br_pool,
                             w_br_att, w_out, g_ffn, w_coarse, b_coarse, w_fine, b_fine, w_gate_e, w_up_e,
                             w_down_e, g_ple, w_ple_gate, w_ple_proj)
        zeros_kv = jnp.zeros((bp, BAND_PREV, ATT_WIDTH), _f32)
        zeros_pool = jnp.zeros((bp, POOL_HIST, POOL_WIDTH), _f32)
        x1_buf, route_buf, kp, vp, pp, counts_p = _mixer_call(
            xp, zeros_kv, zeros_kv, zeros_pool, wts, None,
            total_rows=total, tq=tq_p, n_sub=sub_p, pos0=0, row_off=0, name="mixer_prompt")
        pool_hist = jnp.pad(state_pool[l], ((0, 0), (POOL_HIST - POOL_STATE, 0), (0, 0)))
        x1_buf, route_buf, kn, vn, pn, counts_s = _mixer_call(
            xs, cache_k[l].reshape(bs, BAND_PREV, ATT_WIDTH), cache_v[l].reshape(bs, BAND_PREV, ATT_WIDTH),
            pool_hist, wts, (x1_buf, route_buf),
            total_rows=total, tq=tq_s, n_sub=1, pos0=PAST_LEN, row_off=rows_p, name="mixer_sample")

        counts = counts_p + counts_s
        tables = _block_tables(counts, n_blocks)
        dest, xs_buf = _rank_call(route_buf, counts, tables["zero_end"], n_blocks)
        dest = dest.reshape(total // MOE_ROWS, 1, MOE_ROWS)
        xs_buf, experts_bf16 = _dispatch_call(x1_buf, dest, xs_buf, wts["experts_f32"])
        x2s_buf = _moe_call(xs_buf, tables, wts, experts_bf16, n_blocks)

        xp = _ple_call(x2s_buf, dest, p_prompt[l].reshape(rows_p, -1), wts, row_off=0,
                       name="ple_prompt").reshape(bp, tp, d_model)
        xs = _ple_call(x2s_buf, dest, p_sample[l].reshape(rows_s, -1), wts, row_off=rows_p,
                       name="ple_sample").reshape(bs, ts, d_model)

        tail_p = min(BAND_PREV, tp)
        if tail_p < BAND_PREV:
            kp = jnp.pad(kp, ((0, 0), (BAND_PREV - tail_p, 0), (0, 0)))
            vp = jnp.pad(vp, ((0, 0), (BAND_PREV - tail_p, 0), (0, 0)))
        outs[0].append(kp.reshape(bp, BAND_PREV, N_HEADS, HEAD_DIM))
        outs[1].append(vp.reshape(bp, BAND_PREV, N_HEADS, HEAD_DIM))
        outs[2].append(pp)
        outs[3].append(kn.reshape(bs, ts, N_HEADS, HEAD_DIM))
        outs[4].append(vn.reshape(bs, ts, N_HEADS, HEAD_DIM))
        outs[5].append(pn)
    return (xp, xs) + tuple(jnp.stack(o) for o in outs)
```

```python
import functools

import jax
import jax.numpy as jnp
from jax import lax
from jax.experimental import pallas as pl
from jax.experimental.pallas import tpu as pltpu

CHUNK = 64
N_HEADS = 8
HEAD_DIM = 64
ATT_WIDTH = N_HEADS * HEAD_DIM
POOL_WINDOWS = (2, 4, 8, 16)
POOL_GROUP = 128
POOL_WIDTH = POOL_GROUP * len(POOL_WINDOWS)
POOL_STATE = max(POOL_WINDOWS) - 1
N_PREV_CHUNKS = 8
BAND_PREV = N_PREV_CHUNKS * CHUNK
REL_CLIP = 256
N_GROUPS = 4
EXPERTS_PER_GROUP = 8
N_EXPERTS = N_GROUPS * EXPERTS_PER_GROUP
PAST_LEN = 2048
EPS = 1e-6
MASKED = -1e30
LOG2E = 1.4426950408889634

LANES = 128
SUBLANES = 8
VMEM_LIMIT_BYTES = 56 * 1024 * 1024

POOL_HIST = 16
POOL_PAD = SUBLANES
POOL_BASE = POOL_PAD + POOL_HIST
POOL_LEVELS = 3
ATT_ROWS = 256
PAIRS_PER_GROUP = EXPERTS_PER_GROUP * (EXPERTS_PER_GROUP - 1) // 2
N_CLASSES = N_GROUPS * PAIRS_PER_GROUP
MOE_ROWS = 256
RANK_ROWS = (1536, 1024, 512)
ROUTE_LANES = 128
ROW_TILE = MOE_ROWS * SUBLANES

_f32 = jnp.float32
_bf16 = jnp.bfloat16


def _dot(a, b):
    return jnp.dot(a, b, preferred_element_type=_f32)


def _rms_scale(x):
    return lax.rsqrt(jnp.mean(x * x, axis=-1, keepdims=True) + EPS)


def _load_row_tiles(ref, rows, row0=0):
    return jnp.concatenate(
        [ref[pl.ds(row0 * SUBLANES + s, rows, stride=SUBLANES), :] for s in range(SUBLANES)], axis=-1)


def _store_row_tiles(ref, val, rows, row0=0):
    for s in range(SUBLANES):
        ref[pl.ds(row0 * SUBLANES + s, rows, stride=SUBLANES), :] = val[:, s * LANES:(s + 1) * LANES]


def _const_spec(shape):
    return pl.BlockSpec(shape, lambda *_: (0,) * len(shape), pipeline_mode=pl.Buffered(1))


def _route_rows(logits):
    lane = lax.broadcasted_iota(jnp.int32, (1, ROUTE_LANES), 1)
    lane_f = lane.astype(_f32)
    neg = jnp.float32(-jnp.inf)
    far = jnp.float32(ROUTE_LANES)
    cl = jnp.where(lane < N_GROUPS, logits, neg)
    cmax = jnp.max(cl, axis=-1, keepdims=True)
    grp = jnp.min(jnp.where(cl == cmax, lane_f, far), axis=-1, keepdims=True)
    fine_grp = ((lane - N_GROUPS) >> 3).astype(_f32)
    fl = jnp.where(fine_grp == grp, logits, neg)
    m1 = jnp.max(fl, axis=-1, keepdims=True)
    i1 = jnp.min(jnp.where(fl == m1, lane_f, far), axis=-1, keepdims=True)
    fl2 = jnp.where(lane_f == i1, neg, fl)
    m2 = jnp.max(fl2, axis=-1, keepdims=True)
    i2 = jnp.min(jnp.where(fl2 == m2, lane_f, far), axis=-1, keepdims=True)
    first_lane = N_GROUPS + EXPERTS_PER_GROUP * grp
    la = jnp.minimum(i1, i2) - first_lane
    lb = jnp.maximum(i1, i2) - first_lane
    return PAIRS_PER_GROUP * grp + la * (2 * EXPERTS_PER_GROUP - 1 - la) * 0.5 + (lb - la - 1.0)


def _mixer_kernel(x_ref, k0_ref, v0_ref, p0_ref, gmix_ref, win_ref, wpool_ref, pscale_ref, gq_ref, gk_ref,
                  bd_ref, bias_ref, wbp_ref, wba_ref, wout_ref, gffn_ref, wroute_ref, broute_ref,
                  *rest, tq, n_sub, n_steps, pos0):
    x1_ref, route_ref, kout_ref, vout_ref, pout_ref, cnt_ref, kt_buf, v_buf, u_buf, s_buf = rest[-10:]
    t = pl.program_id(1)

    @pl.when(jnp.logical_and(pl.program_id(0) == 0, t == 0))
    def _init_counts():
        cnt_ref[...] = jnp.zeros_like(cnt_ref)

    att_rows = bias_ref.shape[1]
    band = BAND_PREV + att_rows
    step_rows = n_sub * tq
    d_model = x_ref.shape[-1]
    lane = lax.broadcasted_iota(jnp.int32, (1, LANES), 1)
    even = lane < HEAD_DIM
    col = lax.broadcasted_iota(jnp.int32, (1, band), 1)
    row = lax.broadcasted_iota(jnp.int32, (tq, 1), 0)

    @pl.when(t == 0)
    def _init_history():
        kt_buf[:, 0:BAND_PREV] = k0_ref[0].T.astype(_bf16)
        v_buf[0:BAND_PREV, :] = v0_ref[0].astype(_bf16)
        u_buf[0:POOL_PAD, :] = jnp.zeros((POOL_PAD, POOL_WIDTH), _f32)
        u_buf[POOL_PAD:POOL_BASE, :] = p0_ref[0]
        s_buf[:, :, 0:POOL_PAD, :] = jnp.zeros(s_buf.shape[:2] + (POOL_PAD, POOL_WIDTH), _f32)

    def pool_phase(j, st):
        r0 = j * tq
        x = x_ref[0, r0:r0 + tq, :]
        h = (x * _rms_scale(x) * gmix_ref[...]).astype(_bf16)
        u = _dot(h, win_ref[:, 0:POOL_WIDTH])
        u_buf[POOL_BASE + r0:POOL_BASE + r0 + tq, :] = u
        pos1 = pos0 + t * step_rows + r0 + row + 1
        w0 = POOL_PAD + r0
        span = POOL_HIST + tq
        diffs = []
        for g, w in enumerate(POOL_WINDOWS):
            sl = slice(g * POOL_GROUP, (g + 1) * POOL_GROUP)
            acc = u_buf[w0:w0 + span, sl] + u_buf[w0 - 1:w0 - 1 + span, sl]
            shift = 2
            while shift < w:
                level = s_buf.at[j, shift.bit_length() - 2]
                level[POOL_PAD:POOL_PAD + span, sl] = acc
                acc = acc + level[POOL_PAD - shift:POOL_PAD - shift + span, sl]
                shift *= 2
            cnt = jnp.minimum(pos1, w).astype(_f32)
            diffs.append((acc[POOL_HIST:, :] / cnt - u[:, sl]).astype(_bf16))
        pooled = [_dot(jnp.concatenate(diffs[2 * i:2 * i + 2], axis=-1), wpool_ref[i])
                  for i in range(len(POOL_WINDOWS) // 2)]
        st["x"], st["h"] = x, h
        st["a"] = (jnp.concatenate(pooled, axis=-1) * pscale_ref[...]).astype(_bf16)

    def qkv_phase(j, st):
        r0 = j * tq
        h = st["h"]
        def head_mean_sq(z):
            sq = (z * z).astype(_bf16)
            half = ATT_WIDTH // 2
            return jnp.concatenate([_dot(sq[:, :half], bd_ref[...]), _dot(sq[:, half:], bd_ref[...])], axis=-1)

        q = _dot(h, win_ref[:, POOL_WIDTH:POOL_WIDTH + ATT_WIDTH])
        st["qn"] = (q * lax.rsqrt(head_mean_sq(q) + EPS) * gq_ref[...]).astype(_bf16)
        k = _dot(h, win_ref[:, POOL_WIDTH + ATT_WIDTH:POOL_WIDTH + 2 * ATT_WIDTH])
        kn = k * lax.rsqrt(head_mean_sq(k) + EPS) * gk_ref[...]
        v = _dot(h, win_ref[:, POOL_WIDTH + 2 * ATT_WIDTH:POOL_WIDTH + 3 * ATT_WIDTH])
        kout_ref[0, r0:r0 + tq, :] = kn
        vout_ref[0, r0:r0 + tq, :] = v
        kt_buf[:, BAND_PREV + r0:BAND_PREV + r0 + tq] = kn.T.astype(_bf16)
        v_buf[BAND_PREV + r0:BAND_PREV + r0 + tq, :] = v.astype(_bf16)

    def attention_phase(j, st):
        qn = st["qn"]
        heads = []
        for p in range(N_HEADS // 2):
            sl = slice(p * LANES, (p + 1) * LANES)
            blocks = []
            for r in range(tq // att_rows):
                k0 = j * tq + r * att_rows
                qp = qn[r * att_rows:(r + 1) * att_rows, sl]
                ktp = kt_buf[sl, k0:k0 + band]
                vp = v_buf[k0:k0 + band, sl]
                started = col >= BAND_PREV - (pos0 + t * step_rows + k0)
                acc = None
                inv = []
                for half in range(2):
                    keep = even if half == 0 else jnp.logical_not(even)
                    qh = jnp.where(keep, qp, jnp.zeros_like(qp))
                    vh = jnp.where(keep, vp, jnp.zeros_like(vp))
                    s = _dot(qh, ktp) + bias_ref[2 * p + half]
                    s = jnp.where(started, s, MASKED)
                    e = jnp.exp2(s - jnp.max(s, axis=-1, keepdims=True))
                    inv.append(1.0 / jnp.sum(e, axis=-1, keepdims=True))
                    part = _dot(e.astype(_bf16), vh)
                    acc = part if acc is None else acc + part
                blocks.append(acc * jnp.where(even, inv[0], inv[1]))
            heads.append(jnp.concatenate(blocks, axis=0))
        st["o"] = jnp.concatenate(heads, axis=-1).astype(_bf16)

    def merge_phase(j, st):
        r0 = j * tq
        x, h = st["x"], st["h"]
        gate_off = POOL_WIDTH + 3 * ATT_WIDTH
        ga = _dot(h, win_ref[:, gate_off:gate_off + d_model])
        gb = _dot(h, win_ref[:, gate_off + d_model:gate_off + 2 * d_model])
        m = (jax.nn.sigmoid(ga) * _dot(st["a"], wbp_ref[...])
             + jax.nn.sigmoid(gb) * _dot(st["o"], wba_ref[...]))
        x1 = x + _dot(m.astype(_bf16), wout_ref[...])
        _store_row_tiles(x1_ref, x1, tq, r0)
        h2 = x1 * _rms_scale(x1) * gffn_ref[...]
        h2_hi = h2.astype(_bf16)
        h2_lo = (h2 - h2_hi.astype(_f32)).astype(_bf16)
        hi_both = _dot(h2_hi, wroute_ref[...])
        logits = (hi_both[:, 0:ROUTE_LANES] + hi_both[:, ROUTE_LANES:]
                  + _dot(h2_lo, wroute_ref[:, 0:ROUTE_LANES]) + broute_ref[...])
        cls = jnp.broadcast_to(_route_rows(logits), (tq, ROUTE_LANES))
        route_ref[r0:r0 + tq, :] = cls
        cnt_ref[...] += jnp.sum(jnp.where(cls == lane.astype(_f32), 1.0, 0.0), axis=0, keepdims=True)

    states = [{} for _ in range(n_sub)]
    for phase in (pool_phase, qkv_phase, attention_phase, merge_phase):
        for j in range(n_sub):
            phase(j, states[j])

    pout_ref[0] = u_buf[POOL_BASE + step_rows - POOL_STATE:POOL_BASE + step_rows, :]
    if n_steps > 1:
        chunk = min(step_rows, BAND_PREV)
        for c in range(BAND_PREV // chunk):
            dst = slice(c * chunk, (c + 1) * chunk)
            src = slice(step_rows + c * chunk, step_rows + (c + 1) * chunk)
            kt_buf[:, dst] = kt_buf[:, src]
            v_buf[dst, :] = v_buf[src, :]
        u_buf[POOL_PAD:POOL_BASE, :] = u_buf[POOL_PAD + step_rows:POOL_BASE + step_rows, :]


def _mixer_call(x, k0, v0, p0, wts, shared, *, total_rows, tq, n_sub, pos0, row_off, name):
    bsz, t_len, d_model = x.shape
    step_rows = n_sub * tq
    n_steps = t_len // step_rows
    tail = min(BAND_PREV, t_len)
    assert t_len % step_rows == 0 and tail % step_rows == 0 and row_off % step_rows == 0
    tail_steps = tail // step_rows
    att_rows = min(ATT_ROWS, tq)
    assert tq % att_rows == 0
    band = BAND_PREV + att_rows
    blk_off = row_off // step_rows

    def tail_map(b, t):
        return (b, jnp.maximum(t - (n_steps - tail_steps), 0), 0)

    in_specs = [
        pl.BlockSpec((1, step_rows, d_model), lambda b, t: (b, t, 0)),
        pl.BlockSpec((1, BAND_PREV, ATT_WIDTH), lambda b, t: (b, 0, 0), pipeline_mode=pl.Buffered(1)),
        pl.BlockSpec((1, BAND_PREV, ATT_WIDTH), lambda b, t: (b, 0, 0), pipeline_mode=pl.Buffered(1)),
        pl.BlockSpec((1, POOL_HIST, POOL_WIDTH), lambda b, t: (b, 0, 0)),
        _const_spec((1, d_model)),
        _const_spec(wts["w_in"].shape),
        _const_spec(wts["w_pool"].shape),
        _const_spec((1, POOL_WIDTH)),
        _const_spec((1, ATT_WIDTH)),
        _const_spec((1, ATT_WIDTH)),
        _const_spec((ATT_WIDTH // 2, ATT_WIDTH // 2)),
        _const_spec((N_HEADS, att_rows, band)),
        _const_spec(wts["w_br_pool"].shape),
        _const_spec(wts["w_br_att"].shape),
        _const_spec(wts["w_out"].shape),
        _const_spec((1, d_model)),
        _const_spec((d_model, 2 * ROUTE_LANES)),
        _const_spec((1, ROUTE_LANES)),
    ]
    operands = [x, k0, v0, p0, wts["g_mix"], wts["w_in"], wts["w_pool"], wts["pool_scale"], wts["g_q"],
                wts["g_k"], wts["bd"], wts["bias"][:, :att_rows, :band], wts["w_br_pool"], wts["w_br_att"],
                wts["w_out"], wts["g_ffn"], wts["w_route"], wts["b_route"]]
    aliases = {}
    if shared is not None:
        aliases = {len(operands): 0, len(operands) + 1: 1}
        in_specs += [pl.BlockSpec(memory_space=pl.ANY)] * 2
        operands += list(shared)
    out_specs = [
        pl.BlockSpec((step_rows * SUBLANES, LANES), lambda b, t: (blk_off + b * n_steps + t, 0)),
        pl.BlockSpec((step_rows, ROUTE_LANES), lambda b, t: (blk_off + b * n_steps + t, 0)),
        pl.BlockSpec((1, step_rows, ATT_WIDTH), tail_map),
        pl.BlockSpec((1, step_rows, ATT_WIDTH), tail_map),
        pl.BlockSpec((1, POOL_STATE, POOL_WIDTH), lambda b, t: (b, 0, 0)),
        pl.BlockSpec((SUBLANES, LANES), lambda b, t: (0, 0)),
    ]
    out_shape = [
        jax.ShapeDtypeStruct((total_rows * SUBLANES, LANES), _f32),
        jax.ShapeDtypeStruct((total_rows, ROUTE_LANES), _f32),
        jax.ShapeDtypeStruct((bsz, tail, ATT_WIDTH), _f32),
        jax.ShapeDtypeStruct((bsz, tail, ATT_WIDTH), _f32),
        jax.ShapeDtypeStruct((bsz, POOL_STATE, POOL_WIDTH), _f32),
        jax.ShapeDtypeStruct((SUBLANES, LANES), _f32),
    ]
    kern = functools.partial(_mixer_kernel, tq=tq, n_sub=n_sub, n_steps=n_steps, pos0=pos0)
    return pl.pallas_call(
        kern,
        grid=(bsz, n_steps),
        in_specs=in_specs,
        out_specs=out_specs,
        out_shape=out_shape,
        scratch_shapes=[
            pltpu.VMEM((ATT_WIDTH, BAND_PREV + step_rows), _bf16),
            pltpu.VMEM((BAND_PREV + step_rows, ATT_WIDTH), _bf16),
            pltpu.VMEM((POOL_BASE + step_rows, POOL_WIDTH), _f32),
            pltpu.VMEM((n_sub, POOL_LEVELS, POOL_BASE + tq, POOL_WIDTH), _f32),
        ],
        input_output_aliases=aliases,
        compiler_params=pltpu.CompilerParams(
            dimension_semantics=("arbitrary", "arbitrary"), vmem_limit_bytes=VMEM_LIMIT_BYTES),
        name=name,
    )(*operands)


def _lane_cumsum(x):
    lane = lax.broadcasted_iota(jnp.int32, x.shape, 1)
    shift = 1
    while shift < LANES:
        x = x + jnp.where(lane >= shift, pltpu.roll(x, shift, axis=1), 0.0)
        shift *= 2
    return x


def _rank_kernel(zero_end_ref, route_ref, counts_ref, dest_ref, xs_hbm, base, before, zbuf, zsem, *, n_steps):
    i = pl.program_id(0)
    rows = route_ref.shape[0]
    lane = lax.broadcasted_iota(jnp.int32, (1, LANES), 1).astype(_f32)
    oh = jnp.where(route_ref[...] == lane, 1.0, 0.0)

    def zero_fill(c, carry, *, wait):
        end = zero_end_ref[c]

        @pl.when(end > 0)
        def _():
            copy = pltpu.make_async_copy(
                zbuf, xs_hbm.at[pl.ds((end - MOE_ROWS) * SUBLANES, ROW_TILE), :], zsem.at[0])
            copy.wait() if wait else copy.start()
        return carry

    @pl.when(i == 0)
    def _():
        zbuf[...] = jnp.zeros_like(zbuf)

    per_step = -(-N_CLASSES // n_steps)
    lax.fori_loop(i * per_step, jnp.minimum((i + 1) * per_step, N_CLASSES),
                  functools.partial(zero_fill, wait=False), 0)

    @pl.when(i == 0)
    def _():
        ri = lax.broadcasted_iota(jnp.int32, (rows, rows), 0)
        ci = lax.broadcasted_iota(jnp.int32, (rows, rows), 1)
        before[...] = jnp.where(ri < ci, 1.0, 0.0).astype(_bf16)
        cnt = counts_ref[...]
        padded = jnp.floor((cnt + (MOE_ROWS - 1)) * (1.0 / MOE_ROWS)) * MOE_ROWS
        first = _lane_cumsum(padded) - padded
        ri = lax.broadcasted_iota(jnp.int32, (LANES, LANES), 0)
        ci = lax.broadcasted_iota(jnp.int32, (LANES, LANES), 1)
        first_col = jnp.sum(jnp.where(ri == ci, first[0:1, :], 0.0), axis=-1, keepdims=True)
        base[...] = jnp.broadcast_to(first_col, base.shape)

    oh_t = oh.T
    earlier = _dot(oh_t.astype(_bf16), before[...])
    slot_row = jnp.sum(oh_t * (base[:, 0:1] + earlier), axis=0, keepdims=True)
    base[...] += jnp.sum(oh_t, axis=1, keepdims=True)
    dest_ref[0] = slot_row.astype(jnp.int32)

    @pl.when(i == n_steps - 1)
    def _():
        lax.fori_loop(0, N_CLASSES, functools.partial(zero_fill, wait=True), 0)


def _rank_call(route_buf, counts, zero_end, n_blocks):
    total = route_buf.shape[0]
    rows = next(r for r in RANK_ROWS if total % r == 0)
    n_tiles = total // rows
    grid_spec = pltpu.PrefetchScalarGridSpec(
        num_scalar_prefetch=1,
        grid=(n_tiles,),
        in_specs=[pl.BlockSpec((rows, ROUTE_LANES), lambda i, *_: (i, 0)),
                  pl.BlockSpec((SUBLANES, LANES), lambda i, *_: (0, 0))],
        out_specs=[pl.BlockSpec((1, 1, rows), lambda i, *_: (i, 0, 0)),
                   pl.BlockSpec(memory_space=pl.ANY)],
        scratch_shapes=[pltpu.VMEM((LANES, LANES), _f32), pltpu.VMEM((rows, rows), _bf16),
                        pltpu.VMEM((ROW_TILE, LANES), _f32), pltpu.SemaphoreType.DMA((1,))],
    )
    return pl.pallas_call(
        functools.partial(_rank_kernel, n_steps=n_tiles),
        grid_spec=grid_spec,
        out_shape=[jax.ShapeDtypeStruct((n_tiles, 1, rows), jnp.int32),
                   jax.ShapeDtypeStruct((n_blocks * ROW_TILE, LANES), _f32)],
        compiler_params=pltpu.CompilerParams(dimension_semantics=("arbitrary",)),
        name="rank",
    )(zero_end, route_buf, counts)


def _block_tables(counts, n_blocks):
    cnt = counts[0, :N_CLASSES].astype(jnp.int32)
    padded = (cnt + MOE_ROWS - 1) // MOE_ROWS * MOE_ROWS
    pend = jnp.cumsum(padded)
    nb = pend[-1] // MOE_ROWS
    blk = jnp.minimum(jnp.arange(n_blocks, dtype=jnp.int32), nb - 1)
    blk_cls = jnp.sum(pend[None, :] <= (blk * MOE_ROWS)[:, None], axis=1, dtype=jnp.int32)
    blk_cls = jnp.minimum(blk_cls, N_CLASSES - 1)
    grp = blk_cls // PAIRS_PER_GROUP
    pair = blk_cls % PAIRS_PER_GROUP
    firsts = jnp.arange(1, EXPERTS_PER_GROUP, dtype=jnp.int32)
    pair_start = firsts * (2 * EXPERTS_PER_GROUP - 1 - firsts) // 2
    la = jnp.sum(pair[:, None] >= pair_start[None, :], axis=1, dtype=jnp.int32)
    lb = pair - la * (2 * EXPERTS_PER_GROUP - 1 - la) // 2 + la + 1
    zero_end = jnp.where(cnt > 0, pend, 0).astype(jnp.int32)
    return {
        "ea": grp * EXPERTS_PER_GROUP + la, "eb": grp * EXPERTS_PER_GROUP + lb,
        "nb": nb.reshape(1).astype(jnp.int32), "zero_end": zero_end,
    }


N_DISPATCH_BUFS = 3
N_WEIGHT_SLABS = 256


def _dispatch_kernel(idx_ref, wg_ref, wu_ref, wd_ref, x1_hbm, xs_in_hbm, xs_hbm, wg_out, wu_out, wd_out,
                     tbuf, lsem, ssem, *, n_slabs):
    del xs_in_hbm
    i = pl.program_id(0)
    n = pl.num_programs(0)
    slot = i % N_DISPATCH_BUFS

    @pl.when(i < n_slabs)
    def _round_weights():
        wg_out[...] = wg_ref[...].astype(_bf16)
        wu_out[...] = wu_ref[...].astype(_bf16)
        wd_out[...] = wd_ref[...].astype(_bf16)

    def load_copy(tile, s):
        return pltpu.make_async_copy(x1_hbm.at[pl.ds(tile * ROW_TILE, ROW_TILE), :], tbuf.at[s], lsem.at[s])

    def wait_scatter(s):
        pltpu.make_async_copy(tbuf.at[s], xs_hbm.at[pl.ds(0, ROW_TILE), :], ssem.at[s]).wait()

    @pl.when(i == 0)
    def _():
        load_copy(0, 0).start()

    @pl.when(i >= N_DISPATCH_BUFS - 1)
    def _():
        wait_scatter((i + 1) % N_DISPATCH_BUFS)

    @pl.when(i + 1 < n)
    def _():
        load_copy(i + 1, (i + 1) % N_DISPATCH_BUFS).start()

    load_copy(i, slot).wait()

    def body(pair, carry):
        for priority in range(2):
            r = 2 * pair + priority
            pltpu.async_copy(
                tbuf.at[slot, pl.ds(r * SUBLANES, SUBLANES), :],
                xs_hbm.at[pl.ds(idx_ref[0, 0, r] * SUBLANES, SUBLANES), :],
                ssem.at[slot], priority=priority)
        return carry
    lax.fori_loop(0, MOE_ROWS // 2, body, 0, unroll=4)

    @pl.when(i == n - 1)
    def _():
        @pl.when(n > 1)
        def _():
            wait_scatter((i + N_DISPATCH_BUFS - 1) % N_DISPATCH_BUFS)
        wait_scatter(slot)


def _dispatch_call(x1_buf, dest, xs_buf, expert_weights):
    n_tiles = dest.shape[0]
    n_slabs = min(N_WEIGHT_SLABS, 1 << (n_tiles.bit_length() - 1))
    flat = [w.reshape(-1, w.shape[-1]) for w in expert_weights]
    slab_specs = []
    for w in flat:
        assert w.shape[0] % n_slabs == 0
        slab_specs.append(pl.BlockSpec((w.shape[0] // n_slabs, w.shape[1]),
                                       lambda i, *_: (jnp.minimum(i, n_slabs - 1), 0)))
    in_specs = ([pl.BlockSpec((1, 1, MOE_ROWS), lambda i: (i, 0, 0), memory_space=pltpu.SMEM)] + slab_specs
                + [pl.BlockSpec(memory_space=pl.ANY), pl.BlockSpec(memory_space=pl.ANY)])
    outs = pl.pallas_call(
        functools.partial(_dispatch_kernel, n_slabs=n_slabs),
        grid=(n_tiles,),
        in_specs=in_specs,
        out_specs=[pl.BlockSpec(memory_space=pl.ANY)] + slab_specs,
        out_shape=[jax.ShapeDtypeStruct(xs_buf.shape, _f32)]
        + [jax.ShapeDtypeStruct(w.shape, _bf16) for w in flat],
        scratch_shapes=[
            pltpu.VMEM((N_DISPATCH_BUFS, ROW_TILE, LANES), _f32),
            pltpu.SemaphoreType.DMA((N_DISPATCH_BUFS,)),
            pltpu.SemaphoreType.DMA((N_DISPATCH_BUFS,)),
        ],
        input_output_aliases={len(in_specs) - 1: 0},
        compiler_params=pltpu.CompilerParams(
            dimension_semantics=("arbitrary",), vmem_limit_bytes=VMEM_LIMIT_BYTES),
        name="dispatch",
    )(dest, *flat, x1_buf, xs_buf)
    return outs[0], [o.reshape(w.shape) for o, w in zip(outs[1:], expert_weights)]


MOE_PAIR = 2


def _moe_kernel(ea_ref, eb_ref, nb_ref, xs_ref, gffn_ref, wr_ref, br_ref, *rest):
    out_ref = rest[-1]
    step = pl.program_id(0)
    nb = nb_ref[0]
    lane = lax.broadcasted_iota(jnp.int32, (1, ROUTE_LANES), 1)

    def load_phase(k, st):
        x = _load_row_tiles(xs_ref, MOE_ROWS, k * MOE_ROWS)
        st["x"] = x
        st["h"] = (x * _rms_scale(x) * gffn_ref[...]).astype(_bf16)

    def weight_phase(k, st):
        ea = ea_ref[step * MOE_PAIR + k]
        eb = eb_ref[step * MOE_PAIR + k]
        grp = ea // EXPERTS_PER_GROUP
        logits = _dot(st["h"], wr_ref[...]) + br_ref[...]

        def pick(col):
            return jnp.sum(jnp.where(lane == col, logits, 0.0), axis=-1, keepdims=True)

        coarse = lane < N_GROUPS
        cmax = jnp.max(jnp.where(coarse, logits, -jnp.inf), axis=-1, keepdims=True)
        csum = jnp.sum(jnp.where(coarse, jnp.exp(logits - cmax), 0.0), axis=-1, keepdims=True)
        gp = jnp.exp(pick(grp) - cmax) / csum
        fa = pick(N_GROUPS + ea)
        fb = pick(N_GROUPS + eb)
        fmax = jnp.maximum(fa, fb)
        pa = jnp.exp(fa - fmax)
        pb = jnp.exp(fb - fmax)
        st["wa"] = gp * pa / (pa + pb)
        st["wb"] = gp * pb / (pa + pb)

    def expert_phase(k, st):
        wga, wua, wda, wgb, wub, wdb = rest[6 * k:6 * k + 6]
        h = st["h"]

        def expert(wg_ref, wu_ref, wd_ref):
            g = _dot(h, wg_ref[0])
            u = _dot(h, wu_ref[0])
            return _dot((jax.nn.silu(g) * u).astype(_bf16), wd_ref[0])

        y = st["wa"] * expert(wga, wua, wda) + st["wb"] * expert(wgb, wub, wdb)
        _store_row_tiles(out_ref, st["x"] + y, MOE_ROWS, k * MOE_ROWS)

    def run(n_live):
        states = [{} for _ in range(n_live)]
        for phase in (load_phase, weight_phase, expert_phase):
            for k in range(n_live):
                phase(k, states[k])

    first = step * MOE_PAIR
    for n_live in range(MOE_PAIR, 0, -1):
        cond = (first + n_live <= nb) if n_live == MOE_PAIR else (first + n_live == nb)
        pl.when(cond)(functools.partial(run, n_live))


def _moe_call(xs_buf, tables, wts, expert_weights, n_blocks):
    w_gate, w_up, w_down = expert_weights
    d_model = w_gate.shape[1]
    d_exp = w_gate.shape[2]
    assert n_blocks % MOE_PAIR == 0

    def used_step(s, ea, eb, nb):
        return (jnp.minimum(s, (nb[0] - 1) // MOE_PAIR), 0)

    def expert_of(which, k):
        def index_map(s, ea, eb, nb):
            blk = jnp.minimum(s * MOE_PAIR + k, nb[0] - 1)
            return ((ea, eb)[which][blk], 0, 0)
        return index_map

    weight_specs = []
    weight_args = []
    for k in range(MOE_PAIR):
        for which in range(2):
            weight_specs += [pl.BlockSpec((1, d_model, d_exp), expert_of(which, k)),
                             pl.BlockSpec((1, d_model, d_exp), expert_of(which, k)),
                             pl.BlockSpec((1, d_exp, d_model), expert_of(which, k))]
            weight_args += [w_gate, w_up, w_down]

    grid_spec = pltpu.PrefetchScalarGridSpec(
        num_scalar_prefetch=3,
        grid=(n_blocks // MOE_PAIR,),
        in_specs=[
            pl.BlockSpec((MOE_PAIR * ROW_TILE, LANES), used_step),
            pl.BlockSpec((1, d_model), lambda s, *_: (0, 0)),
            pl.BlockSpec((d_model, ROUTE_LANES), lambda s, *_: (0, 0)),
            pl.BlockSpec((1, ROUTE_LANES), lambda s, *_: (0, 0)),
        ] + weight_specs,
        out_specs=pl.BlockSpec((MOE_PAIR * ROW_TILE, LANES), used_step),
    )
    return pl.pallas_call(
        _moe_kernel,
        grid_spec=grid_spec,
        out_shape=jax.ShapeDtypeStruct(xs_buf.shape, _f32),
        compiler_params=pltpu.CompilerParams(
            dimension_semantics=("arbitrary",), vmem_limit_bytes=VMEM_LIMIT_BYTES),
        name="moe",
    )(tables["ea"], tables["eb"], tables["nb"], xs_buf, wts["g_ffn"], wts["w_route_hi"], wts["b_route"],
      *weight_args)


def _ple_kernel(idx_ref, idx_nxt_ref, p_ref, gple_ref, wgate_ref, wproj_ref, x2s_hbm, out_ref, buf0, buf1, gsem):
    i = pl.program_id(0)
    n = pl.num_programs(0)

    def row_copy(idx_ref, r, off, buf, s):
        return pltpu.make_async_copy(
            x2s_hbm.at[pl.ds(idx_ref[0, 0, off + r] * SUBLANES, SUBLANES), :],
            buf.at[pl.ds(r * SUBLANES, SUBLANES), :],
            gsem.at[s])

    def start_gather(idx_ref, off, buf, s):
        for r in range(MOE_ROWS):
            pltpu.async_copy(
                x2s_hbm.at[pl.ds(idx_ref[0, 0, off + r] * SUBLANES, SUBLANES), :],
                buf.at[pl.ds(r * SUBLANES, SUBLANES), :],
                gsem.at[s], priority=r % 2)

    def wait_gather(buf, s):
        pltpu.make_async_copy(x2s_hbm.at[pl.ds(0, ROW_TILE), :], buf, gsem.at[s]).wait()

    def tile(buf, half):
        rows = pl.ds(half * MOE_ROWS, MOE_ROWS)
        x2 = _load_row_tiles(buf, MOE_ROWS)
        hn = (x2 * _rms_scale(x2) * gple_ref[...]).astype(_bf16)
        gate = jax.nn.sigmoid(_dot(hn, wgate_ref[...]))
        out_ref[rows, :] = x2 + _dot(p_ref[rows, :].astype(_bf16), wproj_ref[...]) * gate

    @pl.when(i == 0)
    def _():
        def body(r, carry):
            row_copy(idx_ref, r, 0, buf0, 0).start()
            return carry
        lax.fori_loop(0, MOE_ROWS, body, 0, unroll=8)

    start_gather(idx_ref, MOE_ROWS, buf1, 1)
    wait_gather(buf0, 0)
    tile(buf0, 0)
    start_gather(idx_nxt_ref, 0, buf0, 0)
    wait_gather(buf1, 1)
    tile(buf1, 1)

    @pl.when(i == n - 1)
    def _():
        wait_gather(buf0, 0)


def _ple_call(x2s_buf, dest, p, wts, *, row_off, name):
    n_rows, d_ple = p.shape
    d_model = wts["w_ple_gate"].shape[0]
    step_rows = 2 * MOE_ROWS
    n_steps = n_rows // step_rows
    blk_off = row_off // step_rows
    last = blk_off + n_steps - 1
    dest = dest.reshape(-1, 1, step_rows)
    idx_spec = functools.partial(pl.BlockSpec, (1, 1, step_rows), memory_space=pltpu.SMEM)
    return pl.pallas_call(
        _ple_kernel,
        grid=(n_steps,),
        in_specs=[
            idx_spec(index_map=lambda i: (blk_off + i, 0, 0)),
            idx_spec(index_map=lambda i: (jnp.minimum(blk_off + i + 1, last), 0, 0)),
            pl.BlockSpec((step_rows, d_ple), lambda i: (i, 0)),
            _const_spec((1, d_model)),
            _const_spec((d_model, d_model)),
            _const_spec((d_ple, d_model)),
            pl.BlockSpec(memory_space=pl.ANY),
        ],
        out_specs=pl.BlockSpec((step_rows, d_model), lambda i: (i, 0)),
        out_shape=jax.ShapeDtypeStruct((n_rows, d_model), _f32),
        scratch_shapes=[pltpu.VMEM((ROW_TILE, LANES), _f32), pltpu.VMEM((ROW_TILE, LANES), _f32),
                        pltpu.SemaphoreType.DMA((2,))],
        compiler_params=pltpu.CompilerParams(
            dimension_semantics=("arbitrary",), vmem_limit_bytes=VMEM_LIMIT_BYTES),
        name=name,
    )(dest, dest, p, wts["g_ple"], wts["w_ple_gate"], wts["w_ple_proj"], x2s_buf)


def _band_bias(table, tq):
    band = BAND_PREV + tq
    n_heads, n_rel = table.shape
    n_far = band - 1 - REL_CLIP
    length = band + tq
    n_near = length - n_far - n_rel
    tab = table.astype(_f32)
    f = jnp.concatenate([jnp.broadcast_to(tab[:, n_rel - 1:], (n_heads, n_far)), tab[:, ::-1],
                         jnp.broadcast_to(tab[:, :1], (n_heads, n_near))], axis=1)
    skew = jnp.tile(f, (1, tq))[:, :tq * (length - 1)].reshape(n_heads, tq, length - 1)
    bias = skew[:, :, tq - 1:tq - 1 + band]
    kc = lax.broadcasted_iota(jnp.int32, (tq, band), 1) // CHUNK
    qc = lax.broadcasted_iota(jnp.int32, (tq, band), 0) // CHUNK
    in_band = (kc >= qc) & (kc <= qc + N_PREV_CHUNKS)
    return jnp.where(in_band[None], bias, MASKED)


def _pair_block_diag(w):
    n2, c, _ = w.shape
    w = w.reshape(n2 // 2, 2, c, c)
    zero = jnp.zeros_like(w[:, 0])
    top = jnp.concatenate([w[:, 0], zero], axis=2)
    bottom = jnp.concatenate([zero, w[:, 1]], axis=2)
    return jnp.concatenate([top, bottom], axis=1)


def _layer_weights(l, tq, g_mix, w_in, w_pool, pool_scale, g_q, g_k, rel_table, w_br_pool, w_br_att, w_out,
                   g_ffn, w_coarse, b_coarse, w_fine, b_fine, w_gate_e, w_up_e, w_down_e, g_ple, w_ple_gate,
                   w_ple_proj):
    d_model = w_in.shape[1]
    pad = ROUTE_LANES - N_GROUPS - N_EXPERTS
    head_of = jnp.arange(ATT_WIDTH // 2, dtype=jnp.int32) // HEAD_DIM
    head_sum = jnp.where(head_of[:, None] == head_of[None, :], 1.0 / HEAD_DIM, 0.0)
    w_route = jnp.concatenate([w_coarse[l], w_fine[l], jnp.zeros((d_model, pad), _f32)], axis=1)
    w_route_hi = w_route.astype(_bf16)
    w_route_lo = (w_route - w_route_hi.astype(_f32)).astype(_bf16)
    b_route = jnp.concatenate([b_coarse[l], b_fine[l], jnp.zeros((pad,), _f32)])
    return {
        "g_mix": g_mix[l].reshape(1, d_model),
        "w_in": w_in[l].astype(_bf16),
        "w_pool": _pair_block_diag(w_pool[l].astype(_bf16)),
        "pool_scale": pool_scale[l].reshape(1, POOL_WIDTH),
        "g_q": (jnp.tile(g_q[l], N_HEADS) * (HEAD_DIM ** -0.5 * LOG2E)).reshape(1, ATT_WIDTH),
        "g_k": jnp.tile(g_k[l], N_HEADS).reshape(1, ATT_WIDTH),
        "bd": head_sum.astype(_bf16),
        "bias": _band_bias(rel_table[l] * LOG2E, tq),
        "w_br_pool": w_br_pool[l].astype(_bf16),
        "w_br_att": w_br_att[l].astype(_bf16),
        "w_out": w_out[l].astype(_bf16),
        "g_ffn": g_ffn[l].reshape(1, d_model),
        "w_route": jnp.concatenate([w_route_hi, w_route_lo], axis=1),
        "b_route": b_route.reshape(1, ROUTE_LANES),
        "w_route_hi": w_route_hi,
        "experts_f32": (w_gate_e[l], w_up_e[l], w_down_e[l]),
        "g_ple": g_ple[l].reshape(1, d_model),
        "w_ple_gate": w_ple_gate[l].astype(_bf16),
        "w_ple_proj": w_ple_proj[l].astype(_bf16),
    }


def _pick_tile(t_len, want):
    tq = min(want, t_len)
    assert t_len % tq == 0 and tq % CHUNK == 0 and BAND_PREV % tq == 0
    return tq


def kernel(x_prompt, x_sample, cache_k, cache_v, state_pool, p_prompt, p_sample, g_mix, w_in, w_pool, pool_scale, g_q, g_k, rel_table, w_br_pool, w_br_att, w_out, g_ffn, w_coarse, b_coarse, w_fine, b_fine, w_gate_e, w_up_e, w_down_e, g_ple, w_ple_gate, w_ple_proj):
    depth = w_in.shape[0]
    bp, tp, d_model = x_prompt.shape
    bs, ts, _ = x_sample.shape
    assert d_model == SUBLANES * LANES
    tq_p = _pick_tile(tp, 512)
    tq_s = _pick_tile(ts, 256)
    sub_p = 2 if tp % (2 * tq_p) == 0 and BAND_PREV % (2 * tq_p) == 0 else 1
    rows_p, rows_s = bp * tp, bs * ts
    total = rows_p + rows_s
    assert rows_p % (2 * MOE_ROWS) == 0 and rows_s % (2 * MOE_ROWS) == 0 and total % RANK_ROWS[-1] == 0
    n_blocks = -(-(total + N_CLASSES * (MOE_ROWS - 1)) // MOE_ROWS)
    n_blocks = -(-n_blocks // MOE_PAIR) * MOE_PAIR

    xp, xs = x_prompt, x_sample
    outs = [[] for _ in range(6)]
    for l in range(depth):
        wts = _layer_weights(l, min(ATT_ROWS, max(tq_p, tq_s)), g_mix, w_in, w_pool, pool_scale, g_q, g_k, rel_table, w_br_pool,
                             w_br_att, w_out, g_ffn, w_coarse, b_coarse, w_fine, b_fine, w_gate_e, w_up_e,
                             w_down_e, g_ple, w_ple_gate, w_ple_proj)
        zeros_kv = jnp.zeros((bp, BAND_PREV, ATT_WIDTH), _f32)
        zeros_pool = jnp.zeros((bp, POOL_HIST, POOL_WIDTH), _f32)
        x1_buf, route_buf, kp, vp, pp, counts_p = _mixer_call(
            xp, zeros_kv, zeros_kv, zeros_pool, wts, None,
            total_rows=total, tq=tq_p, n_sub=sub_p, pos0=0, row_off=0, name="mixer_prompt")
        pool_hist = jnp.pad(state_pool[l], ((0, 0), (POOL_HIST - POOL_STATE, 0), (0, 0)))
        x1_buf, route_buf, kn, vn, pn, counts_s = _mixer_call(
            xs, cache_k[l].reshape(bs, BAND_PREV, ATT_WIDTH), cache_v[l].reshape(bs, BAND_PREV, ATT_WIDTH),
            pool_hist, wts, (x1_buf, route_buf),
            total_rows=total, tq=tq_s, n_sub=1, pos0=PAST_LEN, row_off=rows_p, name="mixer_sample")

        counts = counts_p + counts_s
        tables = _block_tables(counts, n_blocks)
        dest, xs_buf = _rank_call(route_buf, counts, tables["zero_end"], n_blocks)
        dest = dest.reshape(total // MOE_ROWS, 1, MOE_ROWS)
        xs_buf, experts_bf16 = _dispatch_call(x1_buf, dest, xs_buf, wts["experts_f32"])
        x2s_buf = _moe_call(xs_buf, tables, wts, experts_bf16, n_blocks)

        xp = _ple_call(x2s_buf, dest, p_prompt[l].reshape(rows_p, -1), wts, row_off=0,
                       name="ple_prompt").reshape(bp, tp, d_model)
        xs = _ple_call(x2s_buf, dest, p_sample[l].reshape(rows_s, -1), wts, row_off=rows_p,
                       name="ple_sample").reshape(bs, ts, d_model)

        tail_p = min(BAND_PREV, tp)
        if tail_p < BAND_PREV:
            kp = jnp.pad(kp, ((0, 0), (BAND_PREV - tail_p, 0), (0, 0)))
            vp = jnp.pad(vp, ((0, 0), (BAND_PREV - tail_p, 0), (0, 0)))
        outs[0].append(kp.reshape(bp, BAND_PREV, N_HEADS, HEAD_DIM))
        outs[1].append(vp.reshape(bp, BAND_PREV, N_HEADS, HEAD_DIM))
        outs[2].append(pp)
        outs[3].append(kn.reshape(bs, ts, N_HEADS, HEAD_DIM))
        outs[4].append(vn.reshape(bs, ts, N_HEADS, HEAD_DIM))
        outs[5].append(pn)
    return (xp, xs) + tuple(jnp.stack(o) for o in outs)
```

```python
import functools

import jax
import jax.numpy as jnp
from jax import lax
from jax.experimental import pallas as pl
from jax.experimental.pallas import tpu as pltpu

CHUNK = 64
N_HEADS = 8
HEAD_DIM = 64
ATT_WIDTH = N_HEADS * HEAD_DIM
POOL_WINDOWS = (2, 4, 8, 16)
POOL_GROUP = 128
POOL_WIDTH = POOL_GROUP * len(POOL_WINDOWS)
POOL_STATE = max(POOL_WINDOWS) - 1
N_PREV_CHUNKS = 8
BAND_PREV = N_PREV_CHUNKS * CHUNK
REL_CLIP = 256
N_GROUPS = 4
EXPERTS_PER_GROUP = 8
N_EXPERTS = N_GROUPS * EXPERTS_PER_GROUP
PAST_LEN = 2048
EPS = 1e-6
MASKED = -1e30
LOG2E = 1.4426950408889634

LANES = 128
SUBLANES = 8
VMEM_LIMIT_BYTES = 56 * 1024 * 1024

POOL_HIST = 16
POOL_PAD = SUBLANES
POOL_BASE = POOL_PAD + POOL_HIST
POOL_LEVELS = 3
ATT_ROWS = 256
PAIRS_PER_GROUP = EXPERTS_PER_GROUP * (EXPERTS_PER_GROUP - 1) // 2
N_CLASSES = N_GROUPS * PAIRS_PER_GROUP
MOE_ROWS = 256
RANK_ROWS = (1536, 1024, 512)
ROUTE_LANES = 128
ROW_TILE = MOE_ROWS * SUBLANES

_f32 = jnp.float32
_bf16 = jnp.bfloat16


def _dot(a, b):
    return jnp.dot(a, b, preferred_element_type=_f32)


def _rms_scale(x):
    return lax.rsqrt(jnp.mean(x * x, axis=-1, keepdims=True) + EPS)


def _load_row_tiles(ref, rows, row0=0):
    return jnp.concatenate(
        [ref[pl.ds(row0 * SUBLANES + s, rows, stride=SUBLANES), :] for s in range(SUBLANES)], axis=-1)


def _store_row_tiles(ref, val, rows, row0=0):
    for s in range(SUBLANES):
        ref[pl.ds(row0 * SUBLANES + s, rows, stride=SUBLANES), :] = val[:, s * LANES:(s + 1) * LANES]


def _const_spec(shape):
    return pl.BlockSpec(shape, lambda *_: (0,) * len(shape), pipeline_mode=pl.Buffered(1))


def _route_classes(logits_t):
    col = lax.broadcasted_iota(jnp.int32, (ROUTE_LANES, 1), 0)
    col_f = col.astype(_f32)
    neg = jnp.float32(-jnp.inf)
    far = jnp.float32(ROUTE_LANES)
    cl = jnp.where(col < N_GROUPS, logits_t, neg)
    cmax = jnp.max(cl, axis=0, keepdims=True)
    grp = jnp.min(jnp.where(cl == cmax, col_f, far), axis=0, keepdims=True)
    fine_grp = ((col - N_GROUPS) >> 3).astype(_f32)
    fl = jnp.where(fine_grp == grp, logits_t, neg)
    m1 = jnp.max(fl, axis=0, keepdims=True)
    i1 = jnp.min(jnp.where(fl == m1, col_f, far), axis=0, keepdims=True)
    fl2 = jnp.where(col_f == i1, neg, fl)
    m2 = jnp.max(fl2, axis=0, keepdims=True)
    i2 = jnp.min(jnp.where(fl2 == m2, col_f, far), axis=0, keepdims=True)
    first_lane = N_GROUPS + EXPERTS_PER_GROUP * grp
    la = jnp.minimum(i1, i2) - first_lane
    lb = jnp.maximum(i1, i2) - first_lane
    return PAIRS_PER_GROUP * grp + la * (2 * EXPERTS_PER_GROUP - 1 - la) * 0.5 + (lb - la - 1.0)


def _mixer_kernel(x_ref, k0_ref, v0_ref, p0_ref, gmix_ref, win_ref, wpool_ref, pscale_ref, gq_ref, gk_ref,
                  bd_ref, bias_ref, wbp_ref, wba_ref, wout_ref, gffn_ref, wroute_ref, broute_ref,
                  *rest, tq, n_sub, n_steps, pos0):
    x1_ref, route_ref, kout_ref, vout_ref, pout_ref, cnt_ref, kt_buf, v_buf, u_buf, s_buf = rest[-10:]
    t = pl.program_id(1)

    @pl.when(jnp.logical_and(pl.program_id(0) == 0, t == 0))
    def _init_counts():
        cnt_ref[...] = jnp.zeros_like(cnt_ref)

    att_rows = bias_ref.shape[1]
    band = BAND_PREV + att_rows
    step_rows = n_sub * tq
    d_model = x_ref.shape[-1]
    lane = lax.broadcasted_iota(jnp.int32, (1, LANES), 1)
    even = lane < HEAD_DIM
    col = lax.broadcasted_iota(jnp.int32, (1, band), 1)
    row = lax.broadcasted_iota(jnp.int32, (tq, 1), 0)

    @pl.when(t == 0)
    def _init_history():
        kt_buf[:, 0:BAND_PREV] = k0_ref[0].T.astype(_bf16)
        v_buf[0:BAND_PREV, :] = v0_ref[0].astype(_bf16)
        u_buf[0:POOL_PAD, :] = jnp.zeros((POOL_PAD, POOL_WIDTH), _f32)
        u_buf[POOL_PAD:POOL_BASE, :] = p0_ref[0]
        s_buf[:, :, 0:POOL_PAD, :] = jnp.zeros(s_buf.shape[:2] + (POOL_PAD, POOL_WIDTH), _f32)

    def pool_phase(j, st):
        r0 = j * tq
        x = x_ref[0, r0:r0 + tq, :]
        h = (x * _rms_scale(x) * gmix_ref[...]).astype(_bf16)
        u = _dot(h, win_ref[:, 0:POOL_WIDTH])
        u_buf[POOL_BASE + r0:POOL_BASE + r0 + tq, :] = u
        pos1 = pos0 + t * step_rows + r0 + row + 1
        w0 = POOL_PAD + r0
        span = POOL_HIST + tq
        diffs = []
        for g, w in enumerate(POOL_WINDOWS):
            sl = slice(g * POOL_GROUP, (g + 1) * POOL_GROUP)
            acc = u_buf[w0:w0 + span, sl] + u_buf[w0 - 1:w0 - 1 + span, sl]
            shift = 2
            while shift < w:
                level = s_buf.at[j, shift.bit_length() - 2]
                level[POOL_PAD:POOL_PAD + span, sl] = acc
                acc = acc + level[POOL_PAD - shift:POOL_PAD - shift + span, sl]
                shift *= 2
            cnt = jnp.minimum(pos1, w).astype(_f32)
            diffs.append((acc[POOL_HIST:, :] / cnt - u[:, sl]).astype(_bf16))
        pooled = [_dot(jnp.concatenate(diffs[2 * i:2 * i + 2], axis=-1), wpool_ref[i])
                  for i in range(len(POOL_WINDOWS) // 2)]
        st["x"], st["h"] = x, h
        st["a"] = (jnp.concatenate(pooled, axis=-1) * pscale_ref[...]).astype(_bf16)

    def qkv_phase(j, st):
        r0 = j * tq
        h = st["h"]
        def head_mean_sq(z):
            sq = (z * z).astype(_bf16)
            half = ATT_WIDTH // 2
            return jnp.concatenate([_dot(sq[:, :half], bd_ref[...]), _dot(sq[:, half:], bd_ref[...])], axis=-1)

        q = _dot(h, win_ref[:, POOL_WIDTH:POOL_WIDTH + ATT_WIDTH])
        st["qn"] = (q * lax.rsqrt(head_mean_sq(q) + EPS) * gq_ref[...]).astype(_bf16)
        k = _dot(h, win_ref[:, POOL_WIDTH + ATT_WIDTH:POOL_WIDTH + 2 * ATT_WIDTH])
        kn = k * lax.rsqrt(head_mean_sq(k) + EPS) * gk_ref[...]
        v = _dot(h, win_ref[:, POOL_WIDTH + 2 * ATT_WIDTH:POOL_WIDTH + 3 * ATT_WIDTH])
        kout_ref[0, r0:r0 + tq, :] = kn
        vout_ref[0, r0:r0 + tq, :] = v
        kt_buf[:, BAND_PREV + r0:BAND_PREV + r0 + tq] = kn.T.astype(_bf16)
        v_buf[BAND_PREV + r0:BAND_PREV + r0 + tq, :] = v.astype(_bf16)

    def attention_phase(j, st):
        qn = st["qn"]
        heads = []
        for p in range(N_HEADS // 2):
            sl = slice(p * LANES, (p + 1) * LANES)
            blocks = []
            for r in range(tq // att_rows):
                k0 = j * tq + r * att_rows
                qp = qn[r * att_rows:(r + 1) * att_rows, sl]
                ktp = kt_buf[sl, k0:k0 + band]
                vp = v_buf[k0:k0 + band, sl]
                started = col >= BAND_PREV - (pos0 + t * step_rows + k0)
                acc = None
                inv = []
                for half in range(2):
                    keep = even if half == 0 else jnp.logical_not(even)
                    qh = jnp.where(keep, qp, jnp.zeros_like(qp))
                    vh = jnp.where(keep, vp, jnp.zeros_like(vp))
                    s = _dot(qh, ktp) + bias_ref[2 * p + half]
                    s = jnp.where(started, s, MASKED)
                    e = jnp.exp2(s - jnp.max(s, axis=-1, keepdims=True))
                    inv.append(1.0 / jnp.sum(e, axis=-1, keepdims=True))
                    part = _dot(e.astype(_bf16), vh)
                    acc = part if acc is None else acc + part
                blocks.append(acc * jnp.where(even, inv[0], inv[1]))
            heads.append(jnp.concatenate(blocks, axis=0))
        st["o"] = jnp.concatenate(heads, axis=-1).astype(_bf16)

    def merge_phase(j, st):
        r0 = j * tq
        x, h = st["x"], st["h"]
        gate_off = POOL_WIDTH + 3 * ATT_WIDTH
        ga = _dot(h, win_ref[:, gate_off:gate_off + d_model])
        gb = _dot(h, win_ref[:, gate_off + d_model:gate_off + 2 * d_model])
        m = (jax.nn.sigmoid(ga) * _dot(st["a"], wbp_ref[...])
             + jax.nn.sigmoid(gb) * _dot(st["o"], wba_ref[...]))
        x1 = x + _dot(m.astype(_bf16), wout_ref[...])
        _store_row_tiles(x1_ref, x1, tq, r0)
        h2 = x1 * _rms_scale(x1) * gffn_ref[...]
        h2_hi = h2.astype(_bf16)
        h2_lo = (h2 - h2_hi.astype(_f32)).astype(_bf16)
        hi_both = _dot(h2_hi, wroute_ref[...])
        logits = (hi_both[:, 0:ROUTE_LANES] + hi_both[:, ROUTE_LANES:]
                  + _dot(h2_lo, wroute_ref[:, 0:ROUTE_LANES]) + broute_ref[...])
        cls = _route_classes(logits.T)
        route_ref[0, :, r0:r0 + tq] = cls
        class_id = lax.broadcasted_iota(jnp.int32, (LANES, 1), 0).astype(_f32)
        cnt_ref[...] += jnp.sum(jnp.where(class_id == cls, 1.0, 0.0), axis=1, keepdims=True)

    states = [{} for _ in range(n_sub)]
    for phase in (pool_phase, qkv_phase, attention_phase, merge_phase):
        for j in range(n_sub):
            phase(j, states[j])

    pout_ref[0] = u_buf[POOL_BASE + step_rows - POOL_STATE:POOL_BASE + step_rows, :]
    if n_steps > 1:
        chunk = min(step_rows, BAND_PREV)
        for c in range(BAND_PREV // chunk):
            dst = slice(c * chunk, (c + 1) * chunk)
            src = slice(step_rows + c * chunk, step_rows + (c + 1) * chunk)
            kt_buf[:, dst] = kt_buf[:, src]
            v_buf[dst, :] = v_buf[src, :]
        u_buf[POOL_PAD:POOL_BASE, :] = u_buf[POOL_PAD + step_rows:POOL_BASE + step_rows, :]


def _mixer_call(x, k0, v0, p0, wts, shared, *, total_rows, tq, n_sub, pos0, row_off, name):
    bsz, t_len, d_model = x.shape
    step_rows = n_sub * tq
    n_steps = t_len // step_rows
    tail = min(BAND_PREV, t_len)
    assert t_len % step_rows == 0 and tail % step_rows == 0 and row_off % step_rows == 0
    tail_steps = tail // step_rows
    att_rows = min(ATT_ROWS, tq)
    assert tq % att_rows == 0
    band = BAND_PREV + att_rows
    blk_off = row_off // step_rows

    def tail_map(b, t):
        return (b, jnp.maximum(t - (n_steps - tail_steps), 0), 0)

    in_specs = [
        pl.BlockSpec((1, step_rows, d_model), lambda b, t: (b, t, 0)),
        pl.BlockSpec((1, BAND_PREV, ATT_WIDTH), lambda b, t: (b, 0, 0), pipeline_mode=pl.Buffered(1)),
        pl.BlockSpec((1, BAND_PREV, ATT_WIDTH), lambda b, t: (b, 0, 0), pipeline_mode=pl.Buffered(1)),
        pl.BlockSpec((1, POOL_HIST, POOL_WIDTH), lambda b, t: (b, 0, 0)),
        _const_spec((1, d_model)),
        _const_spec(wts["w_in"].shape),
        _const_spec(wts["w_pool"].shape),
        _const_spec((1, POOL_WIDTH)),
        _const_spec((1, ATT_WIDTH)),
        _const_spec((1, ATT_WIDTH)),
        _const_spec((ATT_WIDTH // 2, ATT_WIDTH // 2)),
        _const_spec((N_HEADS, att_rows, band)),
        _const_spec(wts["w_br_pool"].shape),
        _const_spec(wts["w_br_att"].shape),
        _const_spec(wts["w_out"].shape),
        _const_spec((1, d_model)),
        _const_spec((d_model, 2 * ROUTE_LANES)),
        _const_spec((1, ROUTE_LANES)),
    ]
    operands = [x, k0, v0, p0, wts["g_mix"], wts["w_in"], wts["w_pool"], wts["pool_scale"], wts["g_q"],
                wts["g_k"], wts["bd"], wts["bias"][:, :att_rows, :band], wts["w_br_pool"], wts["w_br_att"],
                wts["w_out"], wts["g_ffn"], wts["w_route"], wts["b_route"]]
    aliases = {}
    if shared is not None:
        aliases = {len(operands): 0}
        in_specs += [pl.BlockSpec(memory_space=pl.ANY)]
        operands += [shared]
    out_specs = [
        pl.BlockSpec((step_rows * SUBLANES, LANES), lambda b, t: (blk_off + b * n_steps + t, 0)),
        pl.BlockSpec((1, 1, step_rows), lambda b, t: (b * n_steps + t, 0, 0)),
        pl.BlockSpec((1, step_rows, ATT_WIDTH), tail_map),
        pl.BlockSpec((1, step_rows, ATT_WIDTH), tail_map),
        pl.BlockSpec((1, POOL_STATE, POOL_WIDTH), lambda b, t: (b, 0, 0)),
        pl.BlockSpec((LANES, LANES), lambda b, t: (0, 0)),
    ]
    out_shape = [
        jax.ShapeDtypeStruct((total_rows * SUBLANES, LANES), _f32),
        jax.ShapeDtypeStruct((bsz * n_steps, 1, step_rows), _f32),
        jax.ShapeDtypeStruct((bsz, tail, ATT_WIDTH), _f32),
        jax.ShapeDtypeStruct((bsz, tail, ATT_WIDTH), _f32),
        jax.ShapeDtypeStruct((bsz, POOL_STATE, POOL_WIDTH), _f32),
        jax.ShapeDtypeStruct((LANES, LANES), _f32),
    ]
    kern = functools.partial(_mixer_kernel, tq=tq, n_sub=n_sub, n_steps=n_steps, pos0=pos0)
    return pl.pallas_call(
        kern,
        grid=(bsz, n_steps),
        in_specs=in_specs,
        out_specs=out_specs,
        out_shape=out_shape,
        scratch_shapes=[
            pltpu.VMEM((ATT_WIDTH, BAND_PREV + step_rows), _bf16),
            pltpu.VMEM((BAND_PREV + step_rows, ATT_WIDTH), _bf16),
            pltpu.VMEM((POOL_BASE + step_rows, POOL_WIDTH), _f32),
            pltpu.VMEM((n_sub, POOL_LEVELS, POOL_BASE + tq, POOL_WIDTH), _f32),
        ],
        input_output_aliases=aliases,
        compiler_params=pltpu.CompilerParams(
            dimension_semantics=("arbitrary", "arbitrary"), vmem_limit_bytes=VMEM_LIMIT_BYTES),
        name=name,
    )(*operands)


def _lane_cumsum(x):
    lane = lax.broadcasted_iota(jnp.int32, x.shape, 1)
    shift = 1
    while shift < LANES:
        x = x + jnp.where(lane >= shift, pltpu.roll(x, shift, axis=1), 0.0)
        shift *= 2
    return x


def _rank_kernel(zero_end_ref, route_ref, counts_ref, dest_ref, xs_hbm, base, before, zbuf, zsem, *, n_steps):
    i = pl.program_id(0)
    rows = route_ref.shape[-1]
    class_id = lax.broadcasted_iota(jnp.int32, (LANES, 1), 0).astype(_f32)
    oh_t = jnp.where(class_id == route_ref[0], 1.0, 0.0)

    def zero_fill(c, carry, *, wait):
        end = zero_end_ref[c]

        @pl.when(end > 0)
        def _():
            copy = pltpu.make_async_copy(
                zbuf, xs_hbm.at[pl.ds((end - MOE_ROWS) * SUBLANES, ROW_TILE), :], zsem.at[0])
            copy.wait() if wait else copy.start()
        return carry

    @pl.when(i == 0)
    def _():
        zbuf[...] = jnp.zeros_like(zbuf)

    per_step = -(-N_CLASSES // n_steps)
    lax.fori_loop(i * per_step, jnp.minimum((i + 1) * per_step, N_CLASSES),
                  functools.partial(zero_fill, wait=False), 0)

    @pl.when(i == 0)
    def _():
        ri = lax.broadcasted_iota(jnp.int32, (rows, rows), 0)
        ci = lax.broadcasted_iota(jnp.int32, (rows, rows), 1)
        before[...] = jnp.where(ri < ci, 1.0, 0.0).astype(_bf16)
        ri = lax.broadcasted_iota(jnp.int32, (LANES, LANES), 0)
        ci = lax.broadcasted_iota(jnp.int32, (LANES, LANES), 1)
        cnt = jnp.sum(jnp.where(ri == ci, counts_ref[...], 0.0), axis=0, keepdims=True)
        cnt = jnp.broadcast_to(cnt, (SUBLANES, LANES))
        padded = jnp.floor((cnt + (MOE_ROWS - 1)) * (1.0 / MOE_ROWS)) * MOE_ROWS
        first = _lane_cumsum(padded) - padded
        first_col = jnp.sum(jnp.where(ri == ci, first[0:1, :], 0.0), axis=-1, keepdims=True)
        base[...] = jnp.broadcast_to(first_col, base.shape)

    earlier = _dot(oh_t.astype(_bf16), before[...])
    slot_row = jnp.sum(oh_t * (base[:, 0:1] + earlier), axis=0, keepdims=True)
    base[...] += jnp.sum(oh_t, axis=1, keepdims=True)
    dest_ref[0] = slot_row.astype(jnp.int32)

    @pl.when(i == n_steps - 1)
    def _():
        lax.fori_loop(0, N_CLASSES, functools.partial(zero_fill, wait=True), 0)


def _rank_call(route_buf, counts, zero_end, n_blocks):
    total = route_buf.shape[0]
    rows = next(r for r in RANK_ROWS if total % r == 0)
    n_tiles = total // rows
    route_buf = route_buf.reshape(n_tiles, 1, rows)
    grid_spec = pltpu.PrefetchScalarGridSpec(
        num_scalar_prefetch=1,
        grid=(n_tiles,),
        in_specs=[pl.BlockSpec((1, 1, rows), lambda i, *_: (i, 0, 0)),
                  pl.BlockSpec((LANES, LANES), lambda i, *_: (0, 0))],
        out_specs=[pl.BlockSpec((1, 1, rows), lambda i, *_: (i, 0, 0)),
                   pl.BlockSpec(memory_space=pl.ANY)],
        scratch_shapes=[pltpu.VMEM((LANES, LANES), _f32), pltpu.VMEM((rows, rows), _bf16),
                        pltpu.VMEM((ROW_TILE, LANES), _f32), pltpu.SemaphoreType.DMA((1,))],
    )
    return pl.pallas_call(
        functools.partial(_rank_kernel, n_steps=n_tiles),
        grid_spec=grid_spec,
        out_shape=[jax.ShapeDtypeStruct((n_tiles, 1, rows), jnp.int32),
                   jax.ShapeDtypeStruct((n_blocks * ROW_TILE, LANES), _f32)],
        compiler_params=pltpu.CompilerParams(dimension_semantics=("arbitrary",)),
        name="rank",
    )(zero_end, route_buf, counts)


def _block_tables(counts, n_blocks):
    cnt = counts[:N_CLASSES, 0].astype(jnp.int32)
    padded = (cnt + MOE_ROWS - 1) // MOE_ROWS * MOE_ROWS
    pend = jnp.cumsum(padded)
    nb = pend[-1] // MOE_ROWS
    blk = jnp.minimum(jnp.arange(n_blocks, dtype=jnp.int32), nb - 1)
    blk_cls = jnp.sum(pend[None, :] <= (blk * MOE_ROWS)[:, None], axis=1, dtype=jnp.int32)
    blk_cls = jnp.minimum(blk_cls, N_CLASSES - 1)
    grp = blk_cls // PAIRS_PER_GROUP
    pair = blk_cls % PAIRS_PER_GROUP
    firsts = jnp.arange(1, EXPERTS_PER_GROUP, dtype=jnp.int32)
    pair_start = firsts * (2 * EXPERTS_PER_GROUP - 1 - firsts) // 2
    la = jnp.sum(pair[:, None] >= pair_start[None, :], axis=1, dtype=jnp.int32)
    lb = pair - la * (2 * EXPERTS_PER_GROUP - 1 - la) // 2 + la + 1
    zero_end = jnp.where(cnt > 0, pend, 0).astype(jnp.int32)
    return {
        "ea": grp * EXPERTS_PER_GROUP + la, "eb": grp * EXPERTS_PER_GROUP + lb,
        "nb": nb.reshape(1).astype(jnp.int32), "zero_end": zero_end,
    }


N_DISPATCH_BUFS = 3
N_WEIGHT_SLABS = 256


def _dispatch_kernel(idx_ref, wg_ref, wu_ref, wd_ref, x1_hbm, xs_in_hbm, xs_hbm, wg_out, wu_out, wd_out,
                     tbuf, lsem, ssem, *, n_slabs):
    del xs_in_hbm
    i = pl.program_id(0)
    n = pl.num_programs(0)
    slot = i % N_DISPATCH_BUFS

    @pl.when(i < n_slabs)
    def _round_weights():
        wg_out[...] = wg_ref[...].astype(_bf16)
        wu_out[...] = wu_ref[...].astype(_bf16)
        wd_out[...] = wd_ref[...].astype(_bf16)

    def load_copy(tile, s):
        return pltpu.make_async_copy(x1_hbm.at[pl.ds(tile * ROW_TILE, ROW_TILE), :], tbuf.at[s], lsem.at[s])

    def wait_scatter(s):
        pltpu.make_async_copy(tbuf.at[s], xs_hbm.at[pl.ds(0, ROW_TILE), :], ssem.at[s]).wait()

    @pl.when(i == 0)
    def _():
        load_copy(0, 0).start()

    @pl.when(i >= N_DISPATCH_BUFS - 1)
    def _():
        wait_scatter((i + 1) % N_DISPATCH_BUFS)

    @pl.when(i + 1 < n)
    def _():
        load_copy(i + 1, (i + 1) % N_DISPATCH_BUFS).start()

    load_copy(i, slot).wait()

    def body(pair, carry):
        for priority in range(2):
            r = 2 * pair + priority
            pltpu.async_copy(
                tbuf.at[slot, pl.ds(r * SUBLANES, SUBLANES), :],
                xs_hbm.at[pl.ds(idx_ref[0, 0, r] * SUBLANES, SUBLANES), :],
                ssem.at[slot], priority=priority)
        return carry
    lax.fori_loop(0, MOE_ROWS // 2, body, 0, unroll=4)

    @pl.when(i == n - 1)
    def _():
        @pl.when(n > 1)
        def _():
            wait_scatter((i + N_DISPATCH_BUFS - 1) % N_DISPATCH_BUFS)
        wait_scatter(slot)


def _dispatch_call(x1_buf, dest, xs_buf, expert_weights):
    n_tiles = dest.shape[0]
    n_slabs = min(N_WEIGHT_SLABS, 1 << (n_tiles.bit_length() - 1))
    flat = [w.reshape(-1, w.shape[-1]) for w in expert_weights]
    slab_specs = []
    for w in flat:
        assert w.shape[0] % n_slabs == 0
        slab_specs.append(pl.BlockSpec((w.shape[0] // n_slabs, w.shape[1]),
                                       lambda i, *_: (jnp.minimum(i, n_slabs - 1), 0)))
    in_specs = ([pl.BlockSpec((1, 1, MOE_ROWS), lambda i: (i, 0, 0), memory_space=pltpu.SMEM)] + slab_specs
                + [pl.BlockSpec(memory_space=pl.ANY), pl.BlockSpec(memory_space=pl.ANY)])
    outs = pl.pallas_call(
        functools.partial(_dispatch_kernel, n_slabs=n_slabs),
        grid=(n_tiles,),
        in_specs=in_specs,
        out_specs=[pl.BlockSpec(memory_space=pl.ANY)] + slab_specs,
        out_shape=[jax.ShapeDtypeStruct(xs_buf.shape, _f32)]
        + [jax.ShapeDtypeStruct(w.shape, _bf16) for w in flat],
        scratch_shapes=[
            pltpu.VMEM((N_DISPATCH_BUFS, ROW_TILE, LANES), _f32),
            pltpu.SemaphoreType.DMA((N_DISPATCH_BUFS,)),
            pltpu.SemaphoreType.DMA((N_DISPATCH_BUFS,)),
        ],
        input_output_aliases={len(in_specs) - 1: 0},
        compiler_params=pltpu.CompilerParams(
            dimension_semantics=("arbitrary",), vmem_limit_bytes=VMEM_LIMIT_BYTES),
        name="dispatch",
    )(dest, *flat, x1_buf, xs_buf)
    return outs[0], [o.reshape(w.shape) for o, w in zip(outs[1:], expert_weights)]


MOE_PAIR = 2


def _moe_kernel(ea_ref, eb_ref, nb_ref, xs_ref, gffn_ref, wr_ref, br_ref, *rest):
    out_ref = rest[-1]
    step = pl.program_id(0)
    nb = nb_ref[0]
    lane = lax.broadcasted_iota(jnp.int32, (1, ROUTE_LANES), 1)

    def load_phase(k, st):
        x = _load_row_tiles(xs_ref, MOE_ROWS, k * MOE_ROWS)
        st["x"] = x
        st["h"] = (x * _rms_scale(x) * gffn_ref[...]).astype(_bf16)

    def weight_phase(k, st):
        ea = ea_ref[step * MOE_PAIR + k]
        eb = eb_ref[step * MOE_PAIR + k]
        grp = ea // EXPERTS_PER_GROUP
        logits = _dot(st["h"], wr_ref[...]) + br_ref[...]

        def pick(col):
            return jnp.sum(jnp.where(lane == col, logits, 0.0), axis=-1, keepdims=True)

        coarse = lane < N_GROUPS
        cmax = jnp.max(jnp.where(coarse, logits, -jnp.inf), axis=-1, keepdims=True)
        csum = jnp.sum(jnp.where(coarse, jnp.exp(logits - cmax), 0.0), axis=-1, keepdims=True)
        gp = jnp.exp(pick(grp) - cmax) / csum
        fa = pick(N_GROUPS + ea)
        fb = pick(N_GROUPS + eb)
        fmax = jnp.maximum(fa, fb)
        pa = jnp.exp(fa - fmax)
        pb = jnp.exp(fb - fmax)
        st["wa"] = gp * pa / (pa + pb)
        st["wb"] = gp * pb / (pa + pb)

    def expert_phase(k, st):
        wga, wua, wda, wgb, wub, wdb = rest[6 * k:6 * k + 6]
        h = st["h"]

        def expert(wg_ref, wu_ref, wd_ref):
            g = _dot(h, wg_ref[0])
            u = _dot(h, wu_ref[0])
            return _dot((jax.nn.silu(g) * u).astype(_bf16), wd_ref[0])

        y = st["wa"] * expert(wga, wua, wda) + st["wb"] * expert(wgb, wub, wdb)
        _store_row_tiles(out_ref, st["x"] + y, MOE_ROWS, k * MOE_ROWS)

    def run(n_live):
        states = [{} for _ in range(n_live)]
        for phase in (load_phase, weight_phase, expert_phase):
            for k in range(n_live):
                phase(k, states[k])

    first = step * MOE_PAIR
    for n_live in range(MOE_PAIR, 0, -1):
        cond = (first + n_live <= nb) if n_live == MOE_PAIR else (first + n_live == nb)
        pl.when(cond)(functools.partial(run, n_live))


def _moe_call(xs_buf, tables, wts, expert_weights, n_blocks):
    w_gate, w_up, w_down = expert_weights
    d_model = w_gate.shape[1]
    d_exp = w_gate.shape[2]
    assert n_blocks % MOE_PAIR == 0

    def used_step(s, ea, eb, nb):
        return (jnp.minimum(s, (nb[0] - 1) // MOE_PAIR), 0)

    def expert_of(which, k):
        def index_map(s, ea, eb, nb):
            blk = jnp.minimum(s * MOE_PAIR + k, nb[0] - 1)
            return ((ea, eb)[which][blk], 0, 0)
        return index_map

    weight_specs = []
    weight_args = []
    for k in range(MOE_PAIR):
        for which in range(2):
            weight_specs += [pl.BlockSpec((1, d_model, d_exp), expert_of(which, k)),
                             pl.BlockSpec((1, d_model, d_exp), expert_of(which, k)),
                             pl.BlockSpec((1, d_exp, d_model), expert_of(which, k))]
            weight_args += [w_gate, w_up, w_down]

    grid_spec = pltpu.PrefetchScalarGridSpec(
        num_scalar_prefetch=3,
        grid=(n_blocks // MOE_PAIR,),
        in_specs=[
            pl.BlockSpec((MOE_PAIR * ROW_TILE, LANES), used_step),
            pl.BlockSpec((1, d_model), lambda s, *_: (0, 0)),
            pl.BlockSpec((d_model, ROUTE_LANES), lambda s, *_: (0, 0)),
            pl.BlockSpec((1, ROUTE_LANES), lambda s, *_: (0, 0)),
        ] + weight_specs,
        out_specs=pl.BlockSpec((MOE_PAIR * ROW_TILE, LANES), used_step),
    )
    return pl.pallas_call(
        _moe_kernel,
        grid_spec=grid_spec,
        out_shape=jax.ShapeDtypeStruct(xs_buf.shape, _f32),
        compiler_params=pltpu.CompilerParams(
            dimension_semantics=("arbitrary",), vmem_limit_bytes=VMEM_LIMIT_BYTES),
        name="moe",
    )(tables["ea"], tables["eb"], tables["nb"], xs_buf, wts["g_ffn"], wts["w_route_hi"], wts["b_route"],
      *weight_args)


def _ple_kernel(idx_ref, idx_nxt_ref, p_ref, gple_ref, wgate_ref, wproj_ref, x2s_hbm, out_ref, buf0, buf1, gsem):
    i = pl.program_id(0)
    n = pl.num_programs(0)

    def row_copy(idx_ref, r, off, buf, s):
        return pltpu.make_async_copy(
            x2s_hbm.at[pl.ds(idx_ref[0, 0, off + r] * SUBLANES, SUBLANES), :],
            buf.at[pl.ds(r * SUBLANES, SUBLANES), :],
            gsem.at[s])

    def start_gather(idx_ref, off, buf, s):
        for r in range(MOE_ROWS):
            pltpu.async_copy(
                x2s_hbm.at[pl.ds(idx_ref[0, 0, off + r] * SUBLANES, SUBLANES), :],
                buf.at[pl.ds(r * SUBLANES, SUBLANES), :],
                gsem.at[s], priority=r % 2)

    def wait_gather(buf, s):
        pltpu.make_async_copy(x2s_hbm.at[pl.ds(0, ROW_TILE), :], buf, gsem.at[s]).wait()

    def tile(buf, half):
        rows = pl.ds(half * MOE_ROWS, MOE_ROWS)
        x2 = _load_row_tiles(buf, MOE_ROWS)
        hn = (x2 * _rms_scale(x2) * gple_ref[...]).astype(_bf16)
        gate = jax.nn.sigmoid(_dot(hn, wgate_ref[...]))
        out_ref[rows, :] = x2 + _dot(p_ref[rows, :].astype(_bf16), wproj_ref[...]) * gate

    @pl.when(i == 0)
    def _():
        def body(r, carry):
            row_copy(idx_ref, r, 0, buf0, 0).start()
            return carry
        lax.fori_loop(0, MOE_ROWS, body, 0, unroll=8)

    start_gather(idx_ref, MOE_ROWS, buf1, 1)
    wait_gather(buf0, 0)
    tile(buf0, 0)
    start_gather(idx_nxt_ref, 0, buf0, 0)
    wait_gather(buf1, 1)
    tile(buf1, 1)

    @pl.when(i == n - 1)
    def _():
        wait_gather(buf0, 0)


def _ple_call(x2s_buf, dest, p, wts, *, row_off, name):
    n_rows, d_ple = p.shape
    d_model = wts["w_ple_gate"].shape[0]
    step_rows = 2 * MOE_ROWS
    n_steps = n_rows // step_rows
    blk_off = row_off // step_rows
    last = blk_off + n_steps - 1
    dest = dest.reshape(-1, 1, step_rows)
    idx_spec = functools.partial(pl.BlockSpec, (1, 1, step_rows), memory_space=pltpu.SMEM)
    return pl.pallas_call(
        _ple_kernel,
        grid=(n_steps,),
        in_specs=[
            idx_spec(index_map=lambda i: (blk_off + i, 0, 0)),
            idx_spec(index_map=lambda i: (jnp.minimum(blk_off + i + 1, last), 0, 0)),
            pl.BlockSpec((step_rows, d_ple), lambda i: (i, 0)),
            _const_spec((1, d_model)),
            _const_spec((d_model, d_model)),
            _const_spec((d_ple, d_model)),
            pl.BlockSpec(memory_space=pl.ANY),
        ],
        out_specs=pl.BlockSpec((step_rows, d_model), lambda i: (i, 0)),
        out_shape=jax.ShapeDtypeStruct((n_rows, d_model), _f32),
        scratch_shapes=[pltpu.VMEM((ROW_TILE, LANES), _f32), pltpu.VMEM((ROW_TILE, LANES), _f32),
                        pltpu.SemaphoreType.DMA((2,))],
        compiler_params=pltpu.CompilerParams(
            dimension_semantics=("arbitrary",), vmem_limit_bytes=VMEM_LIMIT_BYTES),
        name=name,
    )(dest, dest, p, wts["g_ple"], wts["w_ple_gate"], wts["w_ple_proj"], x2s_buf)


def _band_bias(table, tq):
    band = BAND_PREV + tq
    n_heads, n_rel = table.shape
    n_far = band - 1 - REL_CLIP
    length = band + tq
    n_near = length - n_far - n_rel
    tab = table.astype(_f32)
    f = jnp.concatenate([jnp.broadcast_to(tab[:, n_rel - 1:], (n_heads, n_far)), tab[:, ::-1],
                         jnp.broadcast_to(tab[:, :1], (n_heads, n_near))], axis=1)
    skew = jnp.tile(f, (1, tq))[:, :tq * (length - 1)].reshape(n_heads, tq, length - 1)
    bias = skew[:, :, tq - 1:tq - 1 + band]
    kc = lax.broadcasted_iota(jnp.int32, (tq, band), 1) // CHUNK
    qc = lax.broadcasted_iota(jnp.int32, (tq, band), 0) // CHUNK
    in_band = (kc >= qc) & (kc <= qc + N_PREV_CHUNKS)
    return jnp.where(in_band[None], bias, MASKED)


def _pair_block_diag(w):
    n2, c, _ = w.shape
    w = w.reshape(n2 // 2, 2, c, c)
    zero = jnp.zeros_like(w[:, 0])
    top = jnp.concatenate([w[:, 0], zero], axis=2)
    bottom = jnp.concatenate([zero, w[:, 1]], axis=2)
    return jnp.concatenate([top, bottom], axis=1)


def _layer_weights(l, tq, g_mix, w_in, w_pool, pool_scale, g_q, g_k, rel_table, w_br_pool, w_br_att, w_out,
                   g_ffn, w_coarse, b_coarse, w_fine, b_fine, w_gate_e, w_up_e, w_down_e, g_ple, w_ple_gate,
                   w_ple_proj):
    d_model = w_in.shape[1]
    pad = ROUTE_LANES - N_GROUPS - N_EXPERTS
    head_of = jnp.arange(ATT_WIDTH // 2, dtype=jnp.int32) // HEAD_DIM
    head_sum = jnp.where(head_of[:, None] == head_of[None, :], 1.0 / HEAD_DIM, 0.0)
    w_route = jnp.concatenate([w_coarse[l], w_fine[l], jnp.zeros((d_model, pad), _f32)], axis=1)
    w_route_hi = w_route.astype(_bf16)
    w_route_lo = (w_route - w_route_hi.astype(_f32)).astype(_bf16)
    b_route = jnp.concatenate([b_coarse[l], b_fine[l], jnp.zeros((pad,), _f32)])
    return {
        "g_mix": g_mix[l].reshape(1, d_model),
        "w_in": w_in[l].astype(_bf16),
        "w_pool": _pair_block_diag(w_pool[l].astype(_bf16)),
        "pool_scale": pool_scale[l].reshape(1, POOL_WIDTH),
        "g_q": (jnp.tile(g_q[l], N_HEADS) * (HEAD_DIM ** -0.5 * LOG2E)).reshape(1, ATT_WIDTH),
        "g_k": jnp.tile(g_k[l], N_HEADS).reshape(1, ATT_WIDTH),
        "bd": head_sum.astype(_bf16),
        "bias": _band_bias(rel_table[l] * LOG2E, tq),
        "w_br_pool": w_br_pool[l].astype(_bf16),
        "w_br_att": w_br_att[l].astype(_bf16),
        "w_out": w_out[l].astype(_bf16),
        "g_ffn": g_ffn[l].reshape(1, d_model),
        "w_route": jnp.concatenate([w_route_hi, w_route_lo], axis=1),
        "b_route": b_route.reshape(1, ROUTE_LANES),
        "w_route_hi": w_route_hi,
        "experts_f32": (w_gate_e[l], w_up_e[l], w_down_e[l]),
        "g_ple": g_ple[l].reshape(1, d_model),
        "w_ple_gate": w_ple_gate[l].astype(_bf16),
        "w_ple_proj": w_ple_proj[l].astype(_bf16),
    }


def _pick_tile(t_len, want):
    tq = min(want, t_len)
    assert t_len % tq == 0 and tq % CHUNK == 0 and BAND_PREV % tq == 0
    return tq


def kernel(x_prompt, x_sample, cache_k, cache_v, state_pool, p_prompt, p_sample, g_mix, w_in, w_pool, pool_scale, g_q, g_k, rel_table, w_br_pool, w_br_att, w_out, g_ffn, w_coarse, b_coarse, w_fine, b_fine, w_gate_e, w_up_e, w_down_e, g_ple, w_ple_gate, w_ple_proj):
    depth = w_in.shape[0]
    bp, tp, d_model = x_prompt.shape
    bs, ts, _ = x_sample.shape
    assert d_model == SUBLANES * LANES
    tq_p = _pick_tile(tp, 512)
    tq_s = _pick_tile(ts, 256)
    sub_p = 2 if tp % (2 * tq_p) == 0 and BAND_PREV % (2 * tq_p) == 0 else 1
    rows_p, rows_s = bp * tp, bs * ts
    total = rows_p + rows_s
    assert rows_p % (2 * MOE_ROWS) == 0 and rows_s % (2 * MOE_ROWS) == 0 and total % RANK_ROWS[-1] == 0
    n_blocks = -(-(total + N_CLASSES * (MOE_ROWS - 1)) // MOE_ROWS)
    n_blocks = -(-n_blocks // MOE_PAIR) * MOE_PAIR

    xp, xs = x_prompt, x_sample
    outs = [[] for _ in range(6)]
    for l in range(depth):
        wts = _layer_weights(l, min(ATT_ROWS, max(tq_p, tq_s)), g_mix, w_in, w_pool, pool_scale, g_q, g_k, rel_table, w_br_pool,
                             w_br_att, w_out, g_ffn, w_coarse, b_coarse, w_fine, b_fine, w_gate_e, w_up_e,
                             w_down_e, g_ple, w_ple_gate, w_ple_proj)
        zeros_kv = jnp.zeros((bp, BAND_PREV, ATT_WIDTH), _f32)
        zeros_pool = jnp.zeros((bp, POOL_HIST, POOL_WIDTH), _f32)
        x1_buf, route_p, kp, vp, pp, counts_p = _mixer_call(
            xp, zeros_kv, zeros_kv, zeros_pool, wts, None,
            total_rows=total, tq=tq_p, n_sub=sub_p, pos0=0, row_off=0, name="mixer_prompt")
        pool_hist = jnp.pad(state_pool[l], ((0, 0), (POOL_HIST - POOL_STATE, 0), (0, 0)))
        x1_buf, route_s, kn, vn, pn, counts_s = _mixer_call(
            xs, cache_k[l].reshape(bs, BAND_PREV, ATT_WIDTH), cache_v[l].reshape(bs, BAND_PREV, ATT_WIDTH),
            pool_hist, wts, x1_buf,
            total_rows=total, tq=tq_s, n_sub=1, pos0=PAST_LEN, row_off=rows_p, name="mixer_sample")

        route_buf = jnp.concatenate([route_p.reshape(-1), route_s.reshape(-1)])
        counts = counts_p + counts_s
        tables = _block_tables(counts, n_blocks)
        dest, xs_buf = _rank_call(route_buf, counts, tables["zero_end"], n_blocks)
        dest = dest.reshape(total // MOE_ROWS, 1, MOE_ROWS)
        xs_buf, experts_bf16 = _dispatch_call(x1_buf, dest, xs_buf, wts["experts_f32"])
        x2s_buf = _moe_call(xs_buf, tables, wts, experts_bf16, n_blocks)

        xp = _ple_call(x2s_buf, dest, p_prompt[l].reshape(rows_p, -1), wts, row_off=0,
                       name="ple_prompt").reshape(bp, tp, d_model)
        xs = _ple_call(x2s_buf, dest, p_sample[l].reshape(rows_s, -1), wts, row_off=rows_p,
                       name="ple_sample").reshape(bs, ts, d_model)

        tail_p = min(BAND_PREV, tp)
        if tail_p < BAND_PREV:
            kp = jnp.pad(kp, ((0, 0), (BAND_PREV - tail_p, 0), (0, 0)))
            vp = jnp.pad(vp, ((0, 0), (BAND_PREV - tail_p, 0), (0, 0)))
        outs[0].append(kp.reshape(bp, BAND_PREV, N_HEADS, HEAD_DIM))
        outs[1].append(vp.reshape(bp, BAND_PREV, N_HEADS, HEAD_DIM))
        outs[2].append(pp)
        outs[3].append(kn.reshape(bs, ts, N_HEADS, HEAD_DIM))
        outs[4].append(vn.reshape(bs, ts, N_HEADS, HEAD_DIM))
        outs[5].append(pn)
    return (xp, xs) + tuple(jnp.stack(o) for o in outs)
```

```python
import functools

import jax
import jax.numpy as jnp
from jax import lax
from jax.experimental import pallas as pl
from jax.experimental.pallas import tpu as pltpu

CHUNK = 64
N_HEADS = 8
HEAD_DIM = 64
ATT_WIDTH = N_HEADS * HEAD_DIM
POOL_WINDOWS = (2, 4, 8, 16)
POOL_GROUP = 128
POOL_WIDTH = POOL_GROUP * len(POOL_WINDOWS)
POOL_STATE = max(POOL_WINDOWS) - 1
N_PREV_CHUNKS = 8
BAND_PREV = N_PREV_CHUNKS * CHUNK
REL_CLIP = 256
N_GROUPS = 4
EXPERTS_PER_GROUP = 8
N_EXPERTS = N_GROUPS * EXPERTS_PER_GROUP
PAST_LEN = 2048
EPS = 1e-6
MASKED = -1e30
LOG2E = 1.4426950408889634

LANES = 128
SUBLANES = 8
VMEM_LIMIT_BYTES = 56 * 1024 * 1024

POOL_HIST = 16
POOL_PAD = SUBLANES
POOL_BASE = POOL_PAD + POOL_HIST
POOL_LEVELS = 3
ATT_ROWS = 256
PAIRS_PER_GROUP = EXPERTS_PER_GROUP * (EXPERTS_PER_GROUP - 1) // 2
N_CLASSES = N_GROUPS * PAIRS_PER_GROUP
MOE_ROWS = 256
RANK_ROWS = (1536, 1024, 512)
ROUTE_LANES = 128
ROW_TILE = MOE_ROWS * SUBLANES

_f32 = jnp.float32
_bf16 = jnp.bfloat16


def _dot(a, b):
    return jnp.dot(a, b, preferred_element_type=_f32)


def _rms_scale(x):
    return lax.rsqrt(jnp.mean(x * x, axis=-1, keepdims=True) + EPS)


def _load_row_tiles(ref, rows, row0=0):
    return jnp.concatenate(
        [ref[pl.ds(row0 * SUBLANES + s, rows, stride=SUBLANES), :] for s in range(SUBLANES)], axis=-1)


def _store_row_tiles(ref, val, rows, row0=0):
    for s in range(SUBLANES):
        ref[pl.ds(row0 * SUBLANES + s, rows, stride=SUBLANES), :] = val[:, s * LANES:(s + 1) * LANES]


def _const_spec(shape):
    return pl.BlockSpec(shape, lambda *_: (0,) * len(shape), pipeline_mode=pl.Buffered(1))


def _route_classes(logits_t):
    col = lax.broadcasted_iota(jnp.int32, (ROUTE_LANES, 1), 0)
    col_f = col.astype(_f32)
    neg = jnp.float32(-jnp.inf)
    far = jnp.float32(ROUTE_LANES)
    cl = jnp.where(col < N_GROUPS, logits_t, neg)
    cmax = jnp.max(cl, axis=0, keepdims=True)
    grp = jnp.min(jnp.where(cl == cmax, col_f, far), axis=0, keepdims=True)
    fine_grp = ((col - N_GROUPS) >> 3).astype(_f32)
    fl = jnp.where(fine_grp == grp, logits_t, neg)
    m1 = jnp.max(fl, axis=0, keepdims=True)
    i1 = jnp.min(jnp.where(fl == m1, col_f, far), axis=0, keepdims=True)
    fl2 = jnp.where(col_f == i1, neg, fl)
    m2 = jnp.max(fl2, axis=0, keepdims=True)
    i2 = jnp.min(jnp.where(fl2 == m2, col_f, far), axis=0, keepdims=True)
    first_lane = N_GROUPS + EXPERTS_PER_GROUP * grp
    la = jnp.minimum(i1, i2) - first_lane
    lb = jnp.maximum(i1, i2) - first_lane
    return PAIRS_PER_GROUP * grp + la * (2 * EXPERTS_PER_GROUP - 1 - la) * 0.5 + (lb - la - 1.0)


def _mixer_kernel(x_ref, k0_ref, v0_ref, p0_ref, gmix_ref, win_ref, wpool_ref, pscale_ref, gq_ref, gk_ref,
                  bd_ref, rel_ref, wbp_ref, wba_ref, wout_ref, gffn_ref, wroute_ref, broute_ref,
                  *rest, tq, n_sub, n_steps, pos0):
    x1_ref, route_ref, kout_ref, vout_ref, pout_ref, cnt_ref, kt_buf, v_buf, u_buf, s_buf, bias_ref = rest[-11:]
    t = pl.program_id(1)
    att_rows = bias_ref.shape[1]
    band = BAND_PREV + att_rows

    @pl.when(jnp.logical_and(pl.program_id(0) == 0, t == 0))
    def _init_call():
        cnt_ref[...] = jnp.zeros_like(cnt_ref)
        kc = lax.broadcasted_iota(jnp.int32, (att_rows, band), 1) // CHUNK
        qc = lax.broadcasted_iota(jnp.int32, (att_rows, band), 0) // CHUNK
        in_band = jnp.logical_and(kc >= qc, kc <= qc + N_PREV_CHUNKS)
        for hd in range(N_HEADS):
            rel = jnp.broadcast_to(rel_ref[hd], (att_rows, rel_ref.shape[-1]))
            skewed = pltpu.roll(rel, 0, 1, stride=1, stride_axis=0)
            bias_ref[hd] = jnp.where(in_band, skewed[:, 0:band], MASKED)

    step_rows = n_sub * tq
    d_model = x_ref.shape[-1]
    lane = lax.broadcasted_iota(jnp.int32, (1, LANES), 1)
    even = lane < HEAD_DIM
    col = lax.broadcasted_iota(jnp.int32, (1, band), 1)
    row = lax.broadcasted_iota(jnp.int32, (tq, 1), 0)

    @pl.when(t == 0)
    def _init_history():
        kt_buf[:, 0:BAND_PREV] = k0_ref[0].T.astype(_bf16)
        v_buf[0:BAND_PREV, :] = v0_ref[0].astype(_bf16)
        u_buf[0:POOL_PAD, :] = jnp.zeros((POOL_PAD, POOL_WIDTH), _f32)
        u_buf[POOL_PAD:POOL_BASE, :] = p0_ref[0]
        s_buf[:, :, 0:POOL_PAD, :] = jnp.zeros(s_buf.shape[:2] + (POOL_PAD, POOL_WIDTH), _f32)

    def pool_phase(j, st):
        r0 = j * tq
        x = x_ref[0, r0:r0 + tq, :]
        h = (x * _rms_scale(x) * gmix_ref[...]).astype(_bf16)
        u = _dot(h, win_ref[:, 0:POOL_WIDTH])
        u_buf[POOL_BASE + r0:POOL_BASE + r0 + tq, :] = u
        pos1 = pos0 + t * step_rows + r0 + row + 1
        w0 = POOL_PAD + r0
        span = POOL_HIST + tq
        diffs = []
        for g, w in enumerate(POOL_WINDOWS):
            sl = slice(g * POOL_GROUP, (g + 1) * POOL_GROUP)
            acc = u_buf[w0:w0 + span, sl] + u_buf[w0 - 1:w0 - 1 + span, sl]
            shift = 2
            while shift < w:
                level = s_buf.at[j, shift.bit_length() - 2]
                level[POOL_PAD:POOL_PAD + span, sl] = acc
                acc = acc + level[POOL_PAD - shift:POOL_PAD - shift + span, sl]
                shift *= 2
            cnt = jnp.minimum(pos1, w).astype(_f32)
            diffs.append((acc[POOL_HIST:, :] / cnt - u[:, sl]).astype(_bf16))
        pooled = [_dot(jnp.concatenate(diffs[2 * i:2 * i + 2], axis=-1), wpool_ref[i])
                  for i in range(len(POOL_WINDOWS) // 2)]
        st["x"], st["h"] = x, h
        st["a"] = (jnp.concatenate(pooled, axis=-1) * pscale_ref[...]).astype(_bf16)

    def qkv_phase(j, st):
        r0 = j * tq
        h = st["h"]
        def head_mean_sq(z):
            sq = (z * z).astype(_bf16)
            half = ATT_WIDTH // 2
            return jnp.concatenate([_dot(sq[:, :half], bd_ref[...]), _dot(sq[:, half:], bd_ref[...])], axis=-1)

        q = _dot(h, win_ref[:, POOL_WIDTH:POOL_WIDTH + ATT_WIDTH])
        st["qn"] = (q * lax.rsqrt(head_mean_sq(q) + EPS) * gq_ref[...]).astype(_bf16)
        k = _dot(h, win_ref[:, POOL_WIDTH + ATT_WIDTH:POOL_WIDTH + 2 * ATT_WIDTH])
        kn = k * lax.rsqrt(head_mean_sq(k) + EPS) * gk_ref[...]
        v = _dot(h, win_ref[:, POOL_WIDTH + 2 * ATT_WIDTH:POOL_WIDTH + 3 * ATT_WIDTH])
        kout_ref[0, r0:r0 + tq, :] = kn
        vout_ref[0, r0:r0 + tq, :] = v
        kt_buf[:, BAND_PREV + r0:BAND_PREV + r0 + tq] = kn.T.astype(_bf16)
        v_buf[BAND_PREV + r0:BAND_PREV + r0 + tq, :] = v.astype(_bf16)

    def attention_phase(j, st):
        qn = st["qn"]
        heads = []
        for p in range(N_HEADS // 2):
            sl = slice(p * LANES, (p + 1) * LANES)
            blocks = []
            for r in range(tq // att_rows):
                k0 = j * tq + r * att_rows
                qp = qn[r * att_rows:(r + 1) * att_rows, sl]
                ktp = kt_buf[sl, k0:k0 + band]
                vp = v_buf[k0:k0 + band, sl]
                started = col >= BAND_PREV - (pos0 + t * step_rows + k0)
                acc = None
                inv = []
                for half in range(2):
                    keep = even if half == 0 else jnp.logical_not(even)
                    qh = jnp.where(keep, qp, jnp.zeros_like(qp))
                    vh = jnp.where(keep, vp, jnp.zeros_like(vp))
                    s = _dot(qh, ktp) + bias_ref[2 * p + half]
                    s = jnp.where(started, s, MASKED)
                    e = jnp.exp2(s - jnp.max(s, axis=-1, keepdims=True))
                    inv.append(1.0 / jnp.sum(e, axis=-1, keepdims=True))
                    part = _dot(e.astype(_bf16), vh)
                    acc = part if acc is None else acc + part
                blocks.append(acc * jnp.where(even, inv[0], inv[1]))
            heads.append(jnp.concatenate(blocks, axis=0))
        st["o"] = jnp.concatenate(heads, axis=-1).astype(_bf16)

    def merge_phase(j, st):
        r0 = j * tq
        x, h = st["x"], st["h"]
        gate_off = POOL_WIDTH + 3 * ATT_WIDTH
        ga = _dot(h, win_ref[:, gate_off:gate_off + d_model])
        gb = _dot(h, win_ref[:, gate_off + d_model:gate_off + 2 * d_model])
        m = (jax.nn.sigmoid(ga) * _dot(st["a"], wbp_ref[...])
             + jax.nn.sigmoid(gb) * _dot(st["o"], wba_ref[...]))
        x1 = x + _dot(m.astype(_bf16), wout_ref[...])
        _store_row_tiles(x1_ref, x1, tq, r0)
        h2 = x1 * _rms_scale(x1) * gffn_ref[...]
        h2_hi = h2.astype(_bf16)
        h2_lo = (h2 - h2_hi.astype(_f32)).astype(_bf16)
        hi_both = _dot(h2_hi, wroute_ref[...])
        logits = (hi_both[:, 0:ROUTE_LANES] + hi_both[:, ROUTE_LANES:]
                  + _dot(h2_lo, wroute_ref[:, 0:ROUTE_LANES]) + broute_ref[...])
        cls = _route_classes(logits.T)
        route_ref[0, :, r0:r0 + tq] = cls
        class_id = lax.broadcasted_iota(jnp.int32, (LANES, 1), 0).astype(_f32)
        cnt_ref[...] += jnp.sum(jnp.where(class_id == cls, 1.0, 0.0), axis=1, keepdims=True)

    states = [{} for _ in range(n_sub)]
    for phase in (pool_phase, qkv_phase, attention_phase, merge_phase):
        for j in range(n_sub):
            phase(j, states[j])

    pout_ref[0] = u_buf[POOL_BASE + step_rows - POOL_STATE:POOL_BASE + step_rows, :]
    if n_steps > 1:
        chunk = min(step_rows, BAND_PREV)
        for c in range(BAND_PREV // chunk):
            dst = slice(c * chunk, (c + 1) * chunk)
            src = slice(step_rows + c * chunk, step_rows + (c + 1) * chunk)
            kt_buf[:, dst] = kt_buf[:, src]
            v_buf[dst, :] = v_buf[src, :]
        u_buf[POOL_PAD:POOL_BASE, :] = u_buf[POOL_PAD + step_rows:POOL_BASE + step_rows, :]


def _mixer_call(x, k0, v0, p0, wts, shared, *, total_rows, tq, n_sub, pos0, row_off, name):
    bsz, t_len, d_model = x.shape
    step_rows = n_sub * tq
    n_steps = t_len // step_rows
    tail = min(BAND_PREV, t_len)
    assert t_len % step_rows == 0 and tail % step_rows == 0 and row_off % step_rows == 0
    tail_steps = tail // step_rows
    att_rows = min(ATT_ROWS, tq)
    assert tq % att_rows == 0
    band = BAND_PREV + att_rows
    blk_off = row_off // step_rows

    def tail_map(b, t):
        return (b, jnp.maximum(t - (n_steps - tail_steps), 0), 0)

    in_specs = [
        pl.BlockSpec((1, step_rows, d_model), lambda b, t: (b, t, 0)),
        pl.BlockSpec((1, BAND_PREV, ATT_WIDTH), lambda b, t: (b, 0, 0), pipeline_mode=pl.Buffered(1)),
        pl.BlockSpec((1, BAND_PREV, ATT_WIDTH), lambda b, t: (b, 0, 0), pipeline_mode=pl.Buffered(1)),
        pl.BlockSpec((1, POOL_HIST, POOL_WIDTH), lambda b, t: (b, 0, 0)),
        _const_spec((1, d_model)),
        _const_spec(wts["w_in"].shape),
        _const_spec(wts["w_pool"].shape),
        _const_spec((1, POOL_WIDTH)),
        _const_spec((1, ATT_WIDTH)),
        _const_spec((1, ATT_WIDTH)),
        _const_spec((ATT_WIDTH // 2, ATT_WIDTH // 2)),
        _const_spec(wts["rel"].shape),
        _const_spec(wts["w_br_pool"].shape),
        _const_spec(wts["w_br_att"].shape),
        _const_spec(wts["w_out"].shape),
        _const_spec((1, d_model)),
        _const_spec((d_model, 2 * ROUTE_LANES)),
        _const_spec((1, ROUTE_LANES)),
    ]
    operands = [x, k0, v0, p0, wts["g_mix"], wts["w_in"], wts["w_pool"], wts["pool_scale"], wts["g_q"],
                wts["g_k"], wts["bd"], wts["rel"], wts["w_br_pool"], wts["w_br_att"],
                wts["w_out"], wts["g_ffn"], wts["w_route"], wts["b_route"]]
    aliases = {}
    if shared is not None:
        aliases = {len(operands): 0}
        in_specs += [pl.BlockSpec(memory_space=pl.ANY)]
        operands += [shared]
    out_specs = [
        pl.BlockSpec((step_rows * SUBLANES, LANES), lambda b, t: (blk_off + b * n_steps + t, 0)),
        pl.BlockSpec((1, 1, step_rows), lambda b, t: (b * n_steps + t, 0, 0)),
        pl.BlockSpec((1, step_rows, ATT_WIDTH), tail_map),
        pl.BlockSpec((1, step_rows, ATT_WIDTH), tail_map),
        pl.BlockSpec((1, POOL_STATE, POOL_WIDTH), lambda b, t: (b, 0, 0)),
        pl.BlockSpec((LANES, LANES), lambda b, t: (0, 0)),
    ]
    out_shape = [
        jax.ShapeDtypeStruct((total_rows * SUBLANES, LANES), _f32),
        jax.ShapeDtypeStruct((bsz * n_steps, 1, step_rows), _f32),
        jax.ShapeDtypeStruct((bsz, tail, ATT_WIDTH), _f32),
        jax.ShapeDtypeStruct((bsz, tail, ATT_WIDTH), _f32),
        jax.ShapeDtypeStruct((bsz, POOL_STATE, POOL_WIDTH), _f32),
        jax.ShapeDtypeStruct((LANES, LANES), _f32),
    ]
    kern = functools.partial(_mixer_kernel, tq=tq, n_sub=n_sub, n_steps=n_steps, pos0=pos0)
    return pl.pallas_call(
        kern,
        grid=(bsz, n_steps),
        in_specs=in_specs,
        out_specs=out_specs,
        out_shape=out_shape,
        scratch_shapes=[
            pltpu.VMEM((ATT_WIDTH, BAND_PREV + step_rows), _bf16),
            pltpu.VMEM((BAND_PREV + step_rows, ATT_WIDTH), _bf16),
            pltpu.VMEM((POOL_BASE + step_rows, POOL_WIDTH), _f32),
            pltpu.VMEM((n_sub, POOL_LEVELS, POOL_BASE + tq, POOL_WIDTH), _f32),
            pltpu.VMEM((N_HEADS, att_rows, band), _f32),
        ],
        input_output_aliases=aliases,
        compiler_params=pltpu.CompilerParams(
            dimension_semantics=("arbitrary", "arbitrary"), vmem_limit_bytes=VMEM_LIMIT_BYTES),
        name=name,
    )(*operands)


def _lane_cumsum(x):
    lane = lax.broadcasted_iota(jnp.int32, x.shape, 1)
    shift = 1
    while shift < LANES:
        x = x + jnp.where(lane >= shift, pltpu.roll(x, shift, axis=1), 0.0)
        shift *= 2
    return x


def _rank_kernel(zero_end_ref, route_ref, counts_ref, dest_ref, xs_hbm, base, before, zbuf, zsem, *, n_steps):
    i = pl.program_id(0)
    rows = route_ref.shape[-1]
    class_id = lax.broadcasted_iota(jnp.int32, (LANES, 1), 0).astype(_f32)
    oh_t = jnp.where(class_id == route_ref[0], 1.0, 0.0)

    def zero_fill(c, carry, *, wait):
        end = zero_end_ref[c]

        @pl.when(end > 0)
        def _():
            copy = pltpu.make_async_copy(
                zbuf, xs_hbm.at[pl.ds((end - MOE_ROWS) * SUBLANES, ROW_TILE), :], zsem.at[0])
            copy.wait() if wait else copy.start()
        return carry

    @pl.when(i == 0)
    def _():
        zbuf[...] = jnp.zeros_like(zbuf)

    per_step = -(-N_CLASSES // n_steps)
    lax.fori_loop(i * per_step, jnp.minimum((i + 1) * per_step, N_CLASSES),
                  functools.partial(zero_fill, wait=False), 0)

    @pl.when(i == 0)
    def _():
        ri = lax.broadcasted_iota(jnp.int32, (rows, rows), 0)
        ci = lax.broadcasted_iota(jnp.int32, (rows, rows), 1)
        before[...] = jnp.where(ri < ci, 1.0, 0.0).astype(_bf16)
        ri = lax.broadcasted_iota(jnp.int32, (LANES, LANES), 0)
        ci = lax.broadcasted_iota(jnp.int32, (LANES, LANES), 1)
        cnt = jnp.sum(jnp.where(ri == ci, counts_ref[...], 0.0), axis=0, keepdims=True)
        cnt = jnp.broadcast_to(cnt, (SUBLANES, LANES))
        padded = jnp.floor((cnt + (MOE_ROWS - 1)) * (1.0 / MOE_ROWS)) * MOE_ROWS
        first = _lane_cumsum(padded) - padded
        first_col = jnp.sum(jnp.where(ri == ci, first[0:1, :], 0.0), axis=-1, keepdims=True)
        base[...] = jnp.broadcast_to(first_col, base.shape)

    earlier = _dot(oh_t.astype(_bf16), before[...])
    slot_row = jnp.sum(oh_t * (base[:, 0:1] + earlier), axis=0, keepdims=True)
    base[...] += jnp.sum(oh_t, axis=1, keepdims=True)
    dest_ref[0] = slot_row.astype(jnp.int32)

    @pl.when(i == n_steps - 1)
    def _():
        lax.fori_loop(0, N_CLASSES, functools.partial(zero_fill, wait=True), 0)


def _rank_call(route_buf, counts, zero_end, n_blocks):
    total = route_buf.shape[0]
    rows = next(r for r in RANK_ROWS if total % r == 0)
    n_tiles = total // rows
    route_buf = route_buf.reshape(n_tiles, 1, rows)
    grid_spec = pltpu.PrefetchScalarGridSpec(
        num_scalar_prefetch=1,
        grid=(n_tiles,),
        in_specs=[pl.BlockSpec((1, 1, rows), lambda i, *_: (i, 0, 0)),
                  pl.BlockSpec((LANES, LANES), lambda i, *_: (0, 0))],
        out_specs=[pl.BlockSpec((1, 1, rows), lambda i, *_: (i, 0, 0)),
                   pl.BlockSpec(memory_space=pl.ANY)],
        scratch_shapes=[pltpu.VMEM((LANES, LANES), _f32), pltpu.VMEM((rows, rows), _bf16),
                        pltpu.VMEM((ROW_TILE, LANES), _f32), pltpu.SemaphoreType.DMA((1,))],
    )
    return pl.pallas_call(
        functools.partial(_rank_kernel, n_steps=n_tiles),
        grid_spec=grid_spec,
        out_shape=[jax.ShapeDtypeStruct((n_tiles, 1, rows), jnp.int32),
                   jax.ShapeDtypeStruct((n_blocks * ROW_TILE, LANES), _f32)],
        compiler_params=pltpu.CompilerParams(dimension_semantics=("arbitrary",)),
        name="rank",
    )(zero_end, route_buf, counts)


def _block_tables(counts, n_blocks):
    cnt = counts[:N_CLASSES, 0].astype(jnp.int32)
    padded = (cnt + MOE_ROWS - 1) // MOE_ROWS * MOE_ROWS
    pend = jnp.cumsum(padded)
    nb = pend[-1] // MOE_ROWS
    blk = jnp.minimum(jnp.arange(n_blocks, dtype=jnp.int32), nb - 1)
    blk_cls = jnp.sum(pend[None, :] <= (blk * MOE_ROWS)[:, None], axis=1, dtype=jnp.int32)
    blk_cls = jnp.minimum(blk_cls, N_CLASSES - 1)
    grp = blk_cls // PAIRS_PER_GROUP
    pair = blk_cls % PAIRS_PER_GROUP
    firsts = jnp.arange(1, EXPERTS_PER_GROUP, dtype=jnp.int32)
    pair_start = firsts * (2 * EXPERTS_PER_GROUP - 1 - firsts) // 2
    la = jnp.sum(pair[:, None] >= pair_start[None, :], axis=1, dtype=jnp.int32)
    lb = pair - la * (2 * EXPERTS_PER_GROUP - 1 - la) // 2 + la + 1
    zero_end = jnp.where(cnt > 0, pend, 0).astype(jnp.int32)
    return {
        "ea": grp * EXPERTS_PER_GROUP + la, "eb": grp * EXPERTS_PER_GROUP + lb,
        "nb": nb.reshape(1).astype(jnp.int32), "zero_end": zero_end,
    }


N_DISPATCH_BUFS = 3
N_WEIGHT_SLABS = 256


def _dispatch_kernel(idx_ref, wg_ref, wu_ref, wd_ref, x1_hbm, xs_in_hbm, xs_hbm, wg_out, wu_out, wd_out,
                     tbuf, lsem, ssem, *, n_slabs):
    del xs_in_hbm
    i = pl.program_id(0)
    n = pl.num_programs(0)
    slot = i % N_DISPATCH_BUFS

    @pl.when(i < n_slabs)
    def _round_weights():
        wg_out[...] = wg_ref[...].astype(_bf16)
        wu_out[...] = wu_ref[...].astype(_bf16)
        wd_out[...] = wd_ref[...].astype(_bf16)

    def load_copy(tile, s):
        return pltpu.make_async_copy(x1_hbm.at[pl.ds(tile * ROW_TILE, ROW_TILE), :], tbuf.at[s], lsem.at[s])

    def wait_scatter(s):
        pltpu.make_async_copy(tbuf.at[s], xs_hbm.at[pl.ds(0, ROW_TILE), :], ssem.at[s]).wait()

    @pl.when(i == 0)
    def _():
        load_copy(0, 0).start()

    @pl.when(i >= N_DISPATCH_BUFS - 1)
    def _():
        wait_scatter((i + 1) % N_DISPATCH_BUFS)

    @pl.when(i + 1 < n)
    def _():
        load_copy(i + 1, (i + 1) % N_DISPATCH_BUFS).start()

    load_copy(i, slot).wait()

    def body(pair, carry):
        for priority in range(2):
            r = 2 * pair + priority
            pltpu.async_copy(
                tbuf.at[slot, pl.ds(r * SUBLANES, SUBLANES), :],
                xs_hbm.at[pl.ds(idx_ref[0, 0, r] * SUBLANES, SUBLANES), :],
                ssem.at[slot], priority=priority)
        return carry
    lax.fori_loop(0, MOE_ROWS // 2, body, 0, unroll=4)

    @pl.when(i == n - 1)
    def _():
        @pl.when(n > 1)
        def _():
            wait_scatter((i + N_DISPATCH_BUFS - 1) % N_DISPATCH_BUFS)
        wait_scatter(slot)


def _dispatch_call(x1_buf, dest, xs_buf, expert_weights):
    n_tiles = dest.shape[0]
    n_slabs = min(N_WEIGHT_SLABS, 1 << (n_tiles.bit_length() - 1))
    flat = [w.reshape(-1, w.shape[-1]) for w in expert_weights]
    slab_specs = []
    for w in flat:
        assert w.shape[0] % n_slabs == 0
        slab_specs.append(pl.BlockSpec((w.shape[0] // n_slabs, w.shape[1]),
                                       lambda i, *_: (jnp.minimum(i, n_slabs - 1), 0)))
    in_specs = ([pl.BlockSpec((1, 1, MOE_ROWS), lambda i: (i, 0, 0), memory_space=pltpu.SMEM)] + slab_specs
                + [pl.BlockSpec(memory_space=pl.ANY), pl.BlockSpec(memory_space=pl.ANY)])
    outs = pl.pallas_call(
        functools.partial(_dispatch_kernel, n_slabs=n_slabs),
        grid=(n_tiles,),
        in_specs=in_specs,
        out_specs=[pl.BlockSpec(memory_space=pl.ANY)] + slab_specs,
        out_shape=[jax.ShapeDtypeStruct(xs_buf.shape, _f32)]
        + [jax.ShapeDtypeStruct(w.shape, _bf16) for w in flat],
        scratch_shapes=[
            pltpu.VMEM((N_DISPATCH_BUFS, ROW_TILE, LANES), _f32),
            pltpu.SemaphoreType.DMA((N_DISPATCH_BUFS,)),
            pltpu.SemaphoreType.DMA((N_DISPATCH_BUFS,)),
        ],
        input_output_aliases={len(in_specs) - 1: 0},
        compiler_params=pltpu.CompilerParams(
            dimension_semantics=("arbitrary",), vmem_limit_bytes=VMEM_LIMIT_BYTES),
        name="dispatch",
    )(dest, *flat, x1_buf, xs_buf)
    return outs[0], [o.reshape(w.shape) for o, w in zip(outs[1:], expert_weights)]


MOE_PAIR = 2


def _moe_kernel(ea_ref, eb_ref, nb_ref, xs_ref, gffn_ref, wr_ref, br_ref, *rest):
    out_ref = rest[-1]
    step = pl.program_id(0)
    nb = nb_ref[0]
    lane = lax.broadcasted_iota(jnp.int32, (1, ROUTE_LANES), 1)

    def load_phase(k, st):
        x = _load_row_tiles(xs_ref, MOE_ROWS, k * MOE_ROWS)
        st["x"] = x
        st["h"] = (x * _rms_scale(x) * gffn_ref[...]).astype(_bf16)

    def weight_phase(k, st):
        ea = ea_ref[step * MOE_PAIR + k]
        eb = eb_ref[step * MOE_PAIR + k]
        grp = ea // EXPERTS_PER_GROUP
        logits = _dot(st["h"], wr_ref[...]) + br_ref[...]

        def pick(col):
            return jnp.sum(jnp.where(lane == col, logits, 0.0), axis=-1, keepdims=True)

        coarse = lane < N_GROUPS
        cmax = jnp.max(jnp.where(coarse, logits, -jnp.inf), axis=-1, keepdims=True)
        csum = jnp.sum(jnp.where(coarse, jnp.exp(logits - cmax), 0.0), axis=-1, keepdims=True)
        gp = jnp.exp(pick(grp) - cmax) / csum
        fa = pick(N_GROUPS + ea)
        fb = pick(N_GROUPS + eb)
        fmax = jnp.maximum(fa, fb)
        pa = jnp.exp(fa - fmax)
        pb = jnp.exp(fb - fmax)
        st["wa"] = gp * pa / (pa + pb)
        st["wb"] = gp * pb / (pa + pb)

    def expert_phase(k, st):
        wga, wua, wda, wgb, wub, wdb = rest[6 * k:6 * k + 6]
        h = st["h"]

        def expert(wg_ref, wu_ref, wd_ref):
            g = _dot(h, wg_ref[0])
            u = _dot(h, wu_ref[0])
            return _dot((jax.nn.silu(g) * u).astype(_bf16), wd_ref[0])

        y = st["wa"] * expert(wga, wua, wda) + st["wb"] * expert(wgb, wub, wdb)
        _store_row_tiles(out_ref, st["x"] + y, MOE_ROWS, k * MOE_ROWS)

    def run(n_live):
        states = [{} for _ in range(n_live)]
        for phase in (load_phase, weight_phase, expert_phase):
            for k in range(n_live):
                phase(k, states[k])

    first = step * MOE_PAIR
    for n_live in range(MOE_PAIR, 0, -1):
        cond = (first + n_live <= nb) if n_live == MOE_PAIR else (first + n_live == nb)
        pl.when(cond)(functools.partial(run, n_live))


def _moe_call(xs_buf, tables, wts, expert_weights, n_blocks):
    w_gate, w_up, w_down = expert_weights
    d_model = w_gate.shape[1]
    d_exp = w_gate.shape[2]
    assert n_blocks % MOE_PAIR == 0

    def used_step(s, ea, eb, nb):
        return (jnp.minimum(s, (nb[0] - 1) // MOE_PAIR), 0)

    def expert_of(which, k):
        def index_map(s, ea, eb, nb):
            blk = jnp.minimum(s * MOE_PAIR + k, nb[0] - 1)
            return ((ea, eb)[which][blk], 0, 0)
        return index_map

    weight_specs = []
    weight_args = []
    for k in range(MOE_PAIR):
        for which in range(2):
            weight_specs += [pl.BlockSpec((1, d_model, d_exp), expert_of(which, k)),
                             pl.BlockSpec((1, d_model, d_exp), expert_of(which, k)),
                             pl.BlockSpec((1, d_exp, d_model), expert_of(which, k))]
            weight_args += [w_gate, w_up, w_down]

    grid_spec = pltpu.PrefetchScalarGridSpec(
        num_scalar_prefetch=3,
        grid=(n_blocks // MOE_PAIR,),
        in_specs=[
            pl.BlockSpec((MOE_PAIR * ROW_TILE, LANES), used_step),
            pl.BlockSpec((1, d_model), lambda s, *_: (0, 0)),
            pl.BlockSpec((d_model, ROUTE_LANES), lambda s, *_: (0, 0)),
            pl.BlockSpec((1, ROUTE_LANES), lambda s, *_: (0, 0)),
        ] + weight_specs,
        out_specs=pl.BlockSpec((MOE_PAIR * ROW_TILE, LANES), used_step),
    )
    return pl.pallas_call(
        _moe_kernel,
        grid_spec=grid_spec,
        out_shape=jax.ShapeDtypeStruct(xs_buf.shape, _f32),
        compiler_params=pltpu.CompilerParams(
            dimension_semantics=("arbitrary",), vmem_limit_bytes=VMEM_LIMIT_BYTES),
        name="moe",
    )(tables["ea"], tables["eb"], tables["nb"], xs_buf, wts["g_ffn"], wts["w_route_hi"], wts["b_route"],
      *weight_args)


def _ple_kernel(idx_ref, idx_nxt_ref, p_ref, gple_ref, wgate_ref, wproj_ref, x2s_hbm, out_ref, buf0, buf1, gsem):
    i = pl.program_id(0)
    n = pl.num_programs(0)

    def row_copy(idx_ref, r, off, buf, s):
        return pltpu.make_async_copy(
            x2s_hbm.at[pl.ds(idx_ref[0, 0, off + r] * SUBLANES, SUBLANES), :],
            buf.at[pl.ds(r * SUBLANES, SUBLANES), :],
            gsem.at[s])

    def start_gather(idx_ref, off, buf, s):
        for r in range(MOE_ROWS):
            pltpu.async_copy(
                x2s_hbm.at[pl.ds(idx_ref[0, 0, off + r] * SUBLANES, SUBLANES), :],
                buf.at[pl.ds(r * SUBLANES, SUBLANES), :],
                gsem.at[s], priority=r % 2)

    def wait_gather(buf, s):
        pltpu.make_async_copy(x2s_hbm.at[pl.ds(0, ROW_TILE), :], buf, gsem.at[s]).wait()

    def tile(buf, half):
        rows = pl.ds(half * MOE_ROWS, MOE_ROWS)
        x2 = _load_row_tiles(buf, MOE_ROWS)
        hn = (x2 * _rms_scale(x2) * gple_ref[...]).astype(_bf16)
        gate = jax.nn.sigmoid(_dot(hn, wgate_ref[...]))
        out_ref[rows, :] = x2 + _dot(p_ref[rows, :].astype(_bf16), wproj_ref[...]) * gate

    @pl.when(i == 0)
    def _():
        def body(r, carry):
            row_copy(idx_ref, r, 0, buf0, 0).start()
            return carry
        lax.fori_loop(0, MOE_ROWS, body, 0, unroll=8)

    start_gather(idx_ref, MOE_ROWS, buf1, 1)
    wait_gather(buf0, 0)
    tile(buf0, 0)
    start_gather(idx_nxt_ref, 0, buf0, 0)
    wait_gather(buf1, 1)
    tile(buf1, 1)

    @pl.when(i == n - 1)
    def _():
        wait_gather(buf0, 0)


def _ple_call(x2s_buf, dest, p, wts, *, row_off, name):
    n_rows, d_ple = p.shape
    d_model = wts["w_ple_gate"].shape[0]
    step_rows = 2 * MOE_ROWS
    n_steps = n_rows // step_rows
    blk_off = row_off // step_rows
    last = blk_off + n_steps - 1
    dest = dest.reshape(-1, 1, step_rows)
    idx_spec = functools.partial(pl.BlockSpec, (1, 1, step_rows), memory_space=pltpu.SMEM)
    return pl.pallas_call(
        _ple_kernel,
        grid=(n_steps,),
        in_specs=[
            idx_spec(index_map=lambda i: (blk_off + i, 0, 0)),
            idx_spec(index_map=lambda i: (jnp.minimum(blk_off + i + 1, last), 0, 0)),
            pl.BlockSpec((step_rows, d_ple), lambda i: (i, 0)),
            _const_spec((1, d_model)),
            _const_spec((d_model, d_model)),
            _const_spec((d_ple, d_model)),
            pl.BlockSpec(memory_space=pl.ANY),
        ],
        out_specs=pl.BlockSpec((step_rows, d_model), lambda i: (i, 0)),
        out_shape=jax.ShapeDtypeStruct((n_rows, d_model), _f32),
        scratch_shapes=[pltpu.VMEM((ROW_TILE, LANES), _f32), pltpu.VMEM((ROW_TILE, LANES), _f32),
                        pltpu.SemaphoreType.DMA((2,))],
        compiler_params=pltpu.CompilerParams(
            dimension_semantics=("arbitrary",), vmem_limit_bytes=VMEM_LIMIT_BYTES),
        name=name,
    )(dest, dest, p, wts["g_ple"], wts["w_ple_gate"], wts["w_ple_proj"], x2s_buf)


def _rel_offsets(table):
    n_heads, n_rel = table.shape
    length = BAND_PREV + 2 * ATT_ROWS
    n_far = BAND_PREV - REL_CLIP + 1
    n_near = BAND_PREV + ATT_ROWS - n_far - (n_rel - 1)
    n_neg = length - (BAND_PREV + ATT_ROWS)
    tab = table.astype(_f32)
    last, first = tab[:, n_rel - 1:], tab[:, :1]
    g = jnp.concatenate([jnp.broadcast_to(last, (n_heads, n_far)), tab[:, n_rel - 2::-1],
                         jnp.broadcast_to(first, (n_heads, n_near)), jnp.broadcast_to(last, (n_heads, n_neg))],
                        axis=1)
    return g.reshape(n_heads, 1, length)


def _pair_block_diag(w):
    n2, c, _ = w.shape
    w = w.reshape(n2 // 2, 2, c, c)
    zero = jnp.zeros_like(w[:, 0])
    top = jnp.concatenate([w[:, 0], zero], axis=2)
    bottom = jnp.concatenate([zero, w[:, 1]], axis=2)
    return jnp.concatenate([top, bottom], axis=1)


def _layer_weights(l, tq, g_mix, w_in, w_pool, pool_scale, g_q, g_k, rel_table, w_br_pool, w_br_att, w_out,
                   g_ffn, w_coarse, b_coarse, w_fine, b_fine, w_gate_e, w_up_e, w_down_e, g_ple, w_ple_gate,
                   w_ple_proj):
    d_model = w_in.shape[1]
    pad = ROUTE_LANES - N_GROUPS - N_EXPERTS
    head_of = jnp.arange(ATT_WIDTH // 2, dtype=jnp.int32) // HEAD_DIM
    head_sum = jnp.where(head_of[:, None] == head_of[None, :], 1.0 / HEAD_DIM, 0.0)
    w_route = jnp.concatenate([w_coarse[l], w_fine[l], jnp.zeros((d_model, pad), _f32)], axis=1)
    w_route_hi = w_route.astype(_bf16)
    w_route_lo = (w_route - w_route_hi.astype(_f32)).astype(_bf16)
    b_route = jnp.concatenate([b_coarse[l], b_fine[l], jnp.zeros((pad,), _f32)])
    return {
        "g_mix": g_mix[l].reshape(1, d_model),
        "w_in": w_in[l].astype(_bf16),
        "w_pool": _pair_block_diag(w_pool[l].astype(_bf16)),
        "pool_scale": pool_scale[l].reshape(1, POOL_WIDTH),
        "g_q": (jnp.tile(g_q[l], N_HEADS) * (HEAD_DIM ** -0.5 * LOG2E)).reshape(1, ATT_WIDTH),
        "g_k": jnp.tile(g_k[l], N_HEADS).reshape(1, ATT_WIDTH),
        "bd": head_sum.astype(_bf16),
        "rel": _rel_offsets(rel_table[l] * LOG2E),
        "w_br_pool": w_br_pool[l].astype(_bf16),
        "w_br_att": w_br_att[l].astype(_bf16),
        "w_out": w_out[l].astype(_bf16),
        "g_ffn": g_ffn[l].reshape(1, d_model),
        "w_route": jnp.concatenate([w_route_hi, w_route_lo], axis=1),
        "b_route": b_route.reshape(1, ROUTE_LANES),
        "w_route_hi": w_route_hi,
        "experts_f32": (w_gate_e[l], w_up_e[l], w_down_e[l]),
        "g_ple": g_ple[l].reshape(1, d_model),
        "w_ple_gate": w_ple_gate[l].astype(_bf16),
        "w_ple_proj": w_ple_proj[l].astype(_bf16),
    }


def _pick_tile(t_len, want):
    tq = min(want, t_len)
    assert t_len % tq == 0 and tq % CHUNK == 0 and BAND_PREV % tq == 0
    return tq


def kernel(x_prompt, x_sample, cache_k, cache_v, state_pool, p_prompt, p_sample, g_mix, w_in, w_pool, pool_scale, g_q, g_k, rel_table, w_br_pool, w_br_att, w_out, g_ffn, w_coarse, b_coarse, w_fine, b_fine, w_gate_e, w_up_e, w_down_e, g_ple, w_ple_gate, w_ple_proj):
    depth = w_in.shape[0]
    bp, tp, d_model = x_prompt.shape
    bs, ts, _ = x_sample.shape
    assert d_model == SUBLANES * LANES
    tq_p = _pick_tile(tp, 512)
    tq_s = _pick_tile(ts, 256)
    sub_p = 2 if tp % (2 * tq_p) == 0 and BAND_PREV % (2 * tq_p) == 0 else 1
    rows_p, rows_s = bp * tp, bs * ts
    total = rows_p + rows_s
    assert rows_p % (2 * MOE_ROWS) == 0 and rows_s % (2 * MOE_ROWS) == 0 and total % RANK_ROWS[-1] == 0
    n_blocks = -(-(total + N_CLASSES * (MOE_ROWS - 1)) // MOE_ROWS)
    n_blocks = -(-n_blocks // MOE_PAIR) * MOE_PAIR

    xp, xs = x_prompt, x_sample
    outs = [[] for _ in range(6)]
    for l in range(depth):
        wts = _layer_weights(l, min(ATT_ROWS, max(tq_p, tq_s)), g_mix, w_in, w_pool, pool_scale, g_q, g_k, rel_table, w_br_pool,
                             w_br_att, w_out, g_ffn, w_coarse, b_coarse, w_fine, b_fine, w_gate_e, w_up_e,
                             w_down_e, g_ple, w_ple_gate, w_ple_proj)
        zeros_kv = jnp.zeros((bp, BAND_PREV, ATT_WIDTH), _f32)
        zeros_pool = jnp.zeros((bp, POOL_HIST, POOL_WIDTH), _f32)
        x1_buf, route_p, kp, vp, pp, counts_p = _mixer_call(
            xp, zeros_kv, zeros_kv, zeros_pool, wts, None,
            total_rows=total, tq=tq_p, n_sub=sub_p, pos0=0, row_off=0, name="mixer_prompt")
        pool_hist = jnp.pad(state_pool[l], ((0, 0), (POOL_HIST - POOL_STATE, 0), (0, 0)))
        x1_buf, route_s, kn, vn, pn, counts_s = _mixer_call(
            xs, cache_k[l].reshape(bs, BAND_PREV, ATT_WIDTH), cache_v[l].reshape(bs, BAND_PREV, ATT_WIDTH),
            pool_hist, wts, x1_buf,
            total_rows=total, tq=tq_s, n_sub=1, pos0=PAST_LEN, row_off=rows_p, name="mixer_sample")

        route_buf = jnp.concatenate([route_p.reshape(-1), route_s.reshape(-1)])
        counts = counts_p + counts_s
        tables = _block_tables(counts, n_blocks)
        dest, xs_buf = _rank_call(route_buf, counts, tables["zero_end"], n_blocks)
        dest = dest.reshape(total // MOE_ROWS, 1, MOE_ROWS)
        xs_buf, experts_bf16 = _dispatch_call(x1_buf, dest, xs_buf, wts["experts_f32"])
        x2s_buf = _moe_call(xs_buf, tables, wts, experts_bf16, n_blocks)

        xp = _ple_call(x2s_buf, dest, p_prompt[l].reshape(rows_p, -1), wts, row_off=0,
                       name="ple_prompt").reshape(bp, tp, d_model)
        xs = _ple_call(x2s_buf, dest, p_sample[l].reshape(rows_s, -1), wts, row_off=rows_p,
                       name="ple_sample").reshape(bs, ts, d_model)

        tail_p = min(BAND_PREV, tp)
        if tail_p < BAND_PREV:
            kp = jnp.pad(kp, ((0, 0), (BAND_PREV - tail_p, 0), (0, 0)))
            vp = jnp.pad(vp, ((0, 0), (BAND_PREV - tail_p, 0), (0, 0)))
        outs[0].append(kp.reshape(bp, BAND_PREV, N_HEADS, HEAD_DIM))
        outs[1].append(vp.reshape(bp, BAND_PREV, N_HEADS, HEAD_DIM))
        outs[2].append(pp)
        outs[3].append(kn.reshape(bs, ts, N_HEADS, HEAD_DIM))
        outs[4].append(vn.reshape(bs, ts, N_HEADS, HEAD_DIM))
        outs[5].append(pn)
    return (xp, xs) + tuple(jnp.stack(o) for o in outs)
```

```python
import functools

import jax
import jax.numpy as jnp
from jax import lax
from jax.experimental import pallas as pl
from jax.experimental.pallas import tpu as pltpu

CHUNK = 64
N_HEADS = 8
HEAD_DIM = 64
ATT_WIDTH = N_HEADS * HEAD_DIM
POOL_WINDOWS = (2, 4, 8, 16)
POOL_GROUP = 128
POOL_WIDTH = POOL_GROUP * len(POOL_WINDOWS)
POOL_STATE = max(POOL_WINDOWS) - 1
N_PREV_CHUNKS = 8
BAND_PREV = N_PREV_CHUNKS * CHUNK
REL_CLIP = 256
N_GROUPS = 4
EXPERTS_PER_GROUP = 8
N_EXPERTS = N_GROUPS * EXPERTS_PER_GROUP
PAST_LEN = 2048
EPS = 1e-6
MASKED = -1e30
LOG2E = 1.4426950408889634

LANES = 128
SUBLANES = 8
VMEM_LIMIT_BYTES = 56 * 1024 * 1024

POOL_HIST = 16
POOL_PAD = SUBLANES
POOL_BASE = POOL_PAD + POOL_HIST
POOL_LEVELS = 3
ATT_ROWS = 256
PAIRS_PER_GROUP = EXPERTS_PER_GROUP * (EXPERTS_PER_GROUP - 1) // 2
N_CLASSES = N_GROUPS * PAIRS_PER_GROUP
MOE_ROWS = 256
RANK_ROWS = (1536, 1024, 512)
ROUTE_LANES = 128
ROW_TILE = MOE_ROWS * SUBLANES

_f32 = jnp.float32
_bf16 = jnp.bfloat16


def _dot(a, b):
    return jnp.dot(a, b, preferred_element_type=_f32)


def _rms_scale(x):
    return lax.rsqrt(jnp.mean(x * x, axis=-1, keepdims=True) + EPS)


def _load_row_tiles(ref, rows, row0=0):
    return jnp.concatenate(
        [ref[pl.ds(row0 * SUBLANES + s, rows, stride=SUBLANES), :] for s in range(SUBLANES)], axis=-1)


def _store_row_tiles(ref, val, rows, row0=0):
    for s in range(SUBLANES):
        ref[pl.ds(row0 * SUBLANES + s, rows, stride=SUBLANES), :] = val[:, s * LANES:(s + 1) * LANES]


def _const_spec(shape):
    return pl.BlockSpec(shape, lambda *_: (0,) * len(shape), pipeline_mode=pl.Buffered(1))


def _route_classes(logits_t):
    col = lax.broadcasted_iota(jnp.int32, (ROUTE_LANES, 1), 0)
    col_f = col.astype(_f32)
    neg = jnp.float32(-jnp.inf)
    far = jnp.float32(ROUTE_LANES)
    cl = jnp.where(col < N_GROUPS, logits_t, neg)
    cmax = jnp.max(cl, axis=0, keepdims=True)
    grp = jnp.min(jnp.where(cl == cmax, col_f, far), axis=0, keepdims=True)
    fine_grp = ((col - N_GROUPS) >> 3).astype(_f32)
    fl = jnp.where(fine_grp == grp, logits_t, neg)
    m1 = jnp.max(fl, axis=0, keepdims=True)
    i1 = jnp.min(jnp.where(fl == m1, col_f, far), axis=0, keepdims=True)
    fl2 = jnp.where(col_f == i1, neg, fl)
    m2 = jnp.max(fl2, axis=0, keepdims=True)
    i2 = jnp.min(jnp.where(fl2 == m2, col_f, far), axis=0, keepdims=True)
    first_lane = N_GROUPS + EXPERTS_PER_GROUP * grp
    la = jnp.minimum(i1, i2) - first_lane
    lb = jnp.maximum(i1, i2) - first_lane
    return PAIRS_PER_GROUP * grp + la * (2 * EXPERTS_PER_GROUP - 1 - la) * 0.5 + (lb - la - 1.0)


def _mixer_kernel(x_ref, k0_ref, v0_ref, p0_ref, gmix_ref, win_ref, wpool_ref, pscale_ref, gq_ref, gk_ref,
                  bd_ref, rel_ref, wbp_ref, wba_ref, wout_ref, gffn_ref, wroute_ref, broute_ref,
                  *rest, tq, n_sub, n_steps, pos0):
    x1_ref, route_ref, kout_ref, vout_ref, pout_ref, cnt_ref, kt_buf, v_buf, u_buf, s_buf, bias_ref = rest[-11:]
    t = pl.program_id(1)
    att_rows = bias_ref.shape[1]
    band = BAND_PREV + att_rows

    @pl.when(jnp.logical_and(pl.program_id(0) == 0, t == 0))
    def _init_call():
        cnt_ref[...] = jnp.zeros_like(cnt_ref)
        kc = lax.broadcasted_iota(jnp.int32, (att_rows, band), 1) // CHUNK
        qc = lax.broadcasted_iota(jnp.int32, (att_rows, band), 0) // CHUNK
        in_band = jnp.logical_and(kc >= qc, kc <= qc + N_PREV_CHUNKS)
        for hd in range(N_HEADS):
            rel = jnp.broadcast_to(rel_ref[hd], (att_rows, rel_ref.shape[-1]))
            skewed = pltpu.roll(rel, 0, 1, stride=1, stride_axis=0)
            bias_ref[hd] = jnp.where(in_band, skewed[:, 0:band], MASKED)

    step_rows = n_sub * tq
    d_model = x_ref.shape[-1]
    lane = lax.broadcasted_iota(jnp.int32, (1, LANES), 1)
    even = lane < HEAD_DIM
    col = lax.broadcasted_iota(jnp.int32, (1, band), 1)
    row = lax.broadcasted_iota(jnp.int32, (tq, 1), 0)

    @pl.when(t == 0)
    def _init_history():
        kt_buf[:, 0:BAND_PREV] = k0_ref[0].T.astype(_bf16)
        v_buf[0:BAND_PREV, :] = v0_ref[0].astype(_bf16)
        u_buf[0:POOL_PAD, :] = jnp.zeros((POOL_PAD, POOL_WIDTH), _f32)
        u_buf[POOL_PAD:POOL_BASE, :] = p0_ref[0]
        s_buf[:, :, 0:POOL_PAD, :] = jnp.zeros(s_buf.shape[:2] + (POOL_PAD, POOL_WIDTH), _f32)

    def pool_phase(j, st):
        r0 = j * tq
        x = x_ref[0, r0:r0 + tq, :]
        h = (x * _rms_scale(x) * gmix_ref[...]).astype(_bf16)
        u = _dot(h, win_ref[:, 0:POOL_WIDTH])
        u_buf[POOL_BASE + r0:POOL_BASE + r0 + tq, :] = u
        pos1 = pos0 + t * step_rows + r0 + row + 1
        w0 = POOL_PAD + r0
        span = POOL_HIST + tq
        diffs = []
        for g, w in enumerate(POOL_WINDOWS):
            sl = slice(g * POOL_GROUP, (g + 1) * POOL_GROUP)
            acc = u_buf[w0:w0 + span, sl] + u_buf[w0 - 1:w0 - 1 + span, sl]
            shift = 2
            while shift < w:
                level = s_buf.at[j, shift.bit_length() - 2]
                level[POOL_PAD:POOL_PAD + span, sl] = acc
                acc = acc + level[POOL_PAD - shift:POOL_PAD - shift + span, sl]
                shift *= 2
            inv_cnt = 1.0 / jnp.minimum(pos1, w).astype(_f32)
            diffs.append((acc[POOL_HIST:, :] * inv_cnt - u[:, sl]).astype(_bf16))
        pooled = [_dot(jnp.concatenate(diffs[2 * i:2 * i + 2], axis=-1), wpool_ref[i])
                  for i in range(len(POOL_WINDOWS) // 2)]
        st["x"], st["h"] = x, h
        st["a"] = (jnp.concatenate(pooled, axis=-1) * pscale_ref[...]).astype(_bf16)

    def qkv_phase(j, st):
        r0 = j * tq
        h = st["h"]
        def head_mean_sq(z):
            sq = (z * z).astype(_bf16)
            half = ATT_WIDTH // 2
            return jnp.concatenate([_dot(sq[:, :half], bd_ref[...]), _dot(sq[:, half:], bd_ref[...])], axis=-1)

        q = _dot(h, win_ref[:, POOL_WIDTH:POOL_WIDTH + ATT_WIDTH])
        st["qn"] = (q * lax.rsqrt(head_mean_sq(q) + EPS) * gq_ref[...]).astype(_bf16)
        k = _dot(h, win_ref[:, POOL_WIDTH + ATT_WIDTH:POOL_WIDTH + 2 * ATT_WIDTH])
        kn = k * lax.rsqrt(head_mean_sq(k) + EPS) * gk_ref[...]
        v = _dot(h, win_ref[:, POOL_WIDTH + 2 * ATT_WIDTH:POOL_WIDTH + 3 * ATT_WIDTH])
        kout_ref[0, r0:r0 + tq, :] = kn
        vout_ref[0, r0:r0 + tq, :] = v
        kt_buf[:, BAND_PREV + r0:BAND_PREV + r0 + tq] = kn.T.astype(_bf16)
        v_buf[BAND_PREV + r0:BAND_PREV + r0 + tq, :] = v.astype(_bf16)

    def attention_phase(j, st):
        qn = st["qn"]
        heads = []
        for p in range(N_HEADS // 2):
            sl = slice(p * LANES, (p + 1) * LANES)
            blocks = []
            for r in range(tq // att_rows):
                k0 = j * tq + r * att_rows
                qp = qn[r * att_rows:(r + 1) * att_rows, sl]
                ktp = kt_buf[sl, k0:k0 + band]
                vp = v_buf[k0:k0 + band, sl]
                started = col >= BAND_PREV - (pos0 + t * step_rows + k0)
                acc = None
                inv = []
                for half in range(2):
                    keep = even if half == 0 else jnp.logical_not(even)
                    qh = jnp.where(keep, qp, jnp.zeros_like(qp))
                    vh = jnp.where(keep, vp, jnp.zeros_like(vp))
                    s = _dot(qh, ktp) + bias_ref[2 * p + half]
                    s = jnp.where(started, s, MASKED)
                    e = jnp.exp2(s - jnp.max(s, axis=-1, keepdims=True))
                    inv.append(1.0 / jnp.sum(e, axis=-1, keepdims=True))
                    part = _dot(e.astype(_bf16), vh)
                    acc = part if acc is None else acc + part
                blocks.append(acc * jnp.where(even, inv[0], inv[1]))
            heads.append(jnp.concatenate(blocks, axis=0))
        st["o"] = jnp.concatenate(heads, axis=-1).astype(_bf16)

    def merge_phase(j, st):
        r0 = j * tq
        x, h = st["x"], st["h"]
        gate_off = POOL_WIDTH + 3 * ATT_WIDTH
        ga = _dot(h, win_ref[:, gate_off:gate_off + d_model])
        gb = _dot(h, win_ref[:, gate_off + d_model:gate_off + 2 * d_model])
        m = (jax.nn.sigmoid(ga) * _dot(st["a"], wbp_ref[...])
             + jax.nn.sigmoid(gb) * _dot(st["o"], wba_ref[...]))
        x1 = x + _dot(m.astype(_bf16), wout_ref[...])
        _store_row_tiles(x1_ref, x1, tq, r0)
        h2 = x1 * _rms_scale(x1) * gffn_ref[...]
        h2_hi = h2.astype(_bf16)
        h2_lo = (h2 - h2_hi.astype(_f32)).astype(_bf16)
        hi_both = _dot(h2_hi, wroute_ref[...])
        logits = (hi_both[:, 0:ROUTE_LANES] + hi_both[:, ROUTE_LANES:]
                  + _dot(h2_lo, wroute_ref[:, 0:ROUTE_LANES]) + broute_ref[...])
        cls = _route_classes(logits.T)
        route_ref[0, :, r0:r0 + tq] = cls
        class_id = lax.broadcasted_iota(jnp.int32, (LANES, 1), 0).astype(_f32)
        cnt_ref[...] += jnp.sum(jnp.where(class_id == cls, 1.0, 0.0), axis=1, keepdims=True)

    states = [{} for _ in range(n_sub)]
    for phase in (pool_phase, qkv_phase, attention_phase, merge_phase):
        for j in range(n_sub):
            phase(j, states[j])

    pout_ref[0] = u_buf[POOL_BASE + step_rows - POOL_STATE:POOL_BASE + step_rows, :]
    if n_steps > 1:
        chunk = min(step_rows, BAND_PREV)
        for c in range(BAND_PREV // chunk):
            dst = slice(c * chunk, (c + 1) * chunk)
            src = slice(step_rows + c * chunk, step_rows + (c + 1) * chunk)
            kt_buf[:, dst] = kt_buf[:, src]
            v_buf[dst, :] = v_buf[src, :]
        u_buf[POOL_PAD:POOL_BASE, :] = u_buf[POOL_PAD + step_rows:POOL_BASE + step_rows, :]


def _mixer_call(x, k0, v0, p0, wts, shared, *, total_rows, tq, n_sub, pos0, row_off, name):
    bsz, t_len, d_model = x.shape
    step_rows = n_sub * tq
    n_steps = t_len // step_rows
    tail = min(BAND_PREV, t_len)
    assert t_len % step_rows == 0 and tail % step_rows == 0 and row_off % step_rows == 0
    tail_steps = tail // step_rows
    att_rows = min(ATT_ROWS, tq)
    assert tq % att_rows == 0
    band = BAND_PREV + att_rows
    blk_off = row_off // step_rows

    def tail_map(b, t):
        return (b, jnp.maximum(t - (n_steps - tail_steps), 0), 0)

    in_specs = [
        pl.BlockSpec((1, step_rows, d_model), lambda b, t: (b, t, 0)),
        pl.BlockSpec((1, BAND_PREV, ATT_WIDTH), lambda b, t: (b, 0, 0), pipeline_mode=pl.Buffered(1)),
        pl.BlockSpec((1, BAND_PREV, ATT_WIDTH), lambda b, t: (b, 0, 0), pipeline_mode=pl.Buffered(1)),
        pl.BlockSpec((1, POOL_HIST, POOL_WIDTH), lambda b, t: (b, 0, 0)),
        _const_spec((1, d_model)),
        _const_spec(wts["w_in"].shape),
        _const_spec(wts["w_pool"].shape),
        _const_spec((1, POOL_WIDTH)),
        _const_spec((1, ATT_WIDTH)),
        _const_spec((1, ATT_WIDTH)),
        _const_spec((ATT_WIDTH // 2, ATT_WIDTH // 2)),
        _const_spec(wts["rel"].shape),
        _const_spec(wts["w_br_pool"].shape),
        _const_spec(wts["w_br_att"].shape),
        _const_spec(wts["w_out"].shape),
        _const_spec((1, d_model)),
        _const_spec((d_model, 2 * ROUTE_LANES)),
        _const_spec((1, ROUTE_LANES)),
    ]
    operands = [x, k0, v0, p0, wts["g_mix"], wts["w_in"], wts["w_pool"], wts["pool_scale"], wts["g_q"],
                wts["g_k"], wts["bd"], wts["rel"], wts["w_br_pool"], wts["w_br_att"],
                wts["w_out"], wts["g_ffn"], wts["w_route"], wts["b_route"]]
    aliases = {}
    if shared is not None:
        aliases = {len(operands): 0}
        in_specs += [pl.BlockSpec(memory_space=pl.ANY)]
        operands += [shared]
    out_specs = [
        pl.BlockSpec((step_rows * SUBLANES, LANES), lambda b, t: (blk_off + b * n_steps + t, 0)),
        pl.BlockSpec((1, 1, step_rows), lambda b, t: (b * n_steps + t, 0, 0)),
        pl.BlockSpec((1, step_rows, ATT_WIDTH), tail_map),
        pl.BlockSpec((1, step_rows, ATT_WIDTH), tail_map),
        pl.BlockSpec((1, POOL_STATE, POOL_WIDTH), lambda b, t: (b, 0, 0)),
        pl.BlockSpec((LANES, LANES), lambda b, t: (0, 0)),
    ]
    out_shape = [
        jax.ShapeDtypeStruct((total_rows * SUBLANES, LANES), _f32),
        jax.ShapeDtypeStruct((bsz * n_steps, 1, step_rows), _f32),
        jax.ShapeDtypeStruct((bsz, tail, ATT_WIDTH), _f32),
        jax.ShapeDtypeStruct((bsz, tail, ATT_WIDTH), _f32),
        jax.ShapeDtypeStruct((bsz, POOL_STATE, POOL_WIDTH), _f32),
        jax.ShapeDtypeStruct((LANES, LANES), _f32),
    ]
    kern = functools.partial(_mixer_kernel, tq=tq, n_sub=n_sub, n_steps=n_steps, pos0=pos0)
    return pl.pallas_call(
        kern,
        grid=(bsz, n_steps),
        in_specs=in_specs,
        out_specs=out_specs,
        out_shape=out_shape,
        scratch_shapes=[
            pltpu.VMEM((ATT_WIDTH, BAND_PREV + step_rows), _bf16),
            pltpu.VMEM((BAND_PREV + step_rows, ATT_WIDTH), _bf16),
            pltpu.VMEM((POOL_BASE + step_rows, POOL_WIDTH), _f32),
            pltpu.VMEM((n_sub, POOL_LEVELS, POOL_BASE + tq, POOL_WIDTH), _f32),
            pltpu.VMEM((N_HEADS, att_rows, band), _f32),
        ],
        input_output_aliases=aliases,
        compiler_params=pltpu.CompilerParams(
            dimension_semantics=("arbitrary", "arbitrary"), vmem_limit_bytes=VMEM_LIMIT_BYTES),
        name=name,
    )(*operands)


def _lane_cumsum(x):
    lane = lax.broadcasted_iota(jnp.int32, x.shape, 1)
    shift = 1
    while shift < LANES:
        x = x + jnp.where(lane >= shift, pltpu.roll(x, shift, axis=1), 0.0)
        shift *= 2
    return x


def _rank_kernel(zero_end_ref, route_ref, counts_ref, dest_ref, xs_hbm, base, before, zbuf, zsem, *, n_steps):
    i = pl.program_id(0)
    rows = route_ref.shape[-1]
    class_id = lax.broadcasted_iota(jnp.int32, (LANES, 1), 0).astype(_f32)
    oh_t = jnp.where(class_id == route_ref[0], 1.0, 0.0)

    def zero_fill(c, carry, *, wait):
        end = zero_end_ref[c]

        @pl.when(end > 0)
        def _():
            copy = pltpu.make_async_copy(
                zbuf, xs_hbm.at[pl.ds((end - MOE_ROWS) * SUBLANES, ROW_TILE), :], zsem.at[0])
            copy.wait() if wait else copy.start()
        return carry

    @pl.when(i == 0)
    def _():
        zbuf[...] = jnp.zeros_like(zbuf)

    per_step = -(-N_CLASSES // n_steps)
    lax.fori_loop(i * per_step, jnp.minimum((i + 1) * per_step, N_CLASSES),
                  functools.partial(zero_fill, wait=False), 0)

    @pl.when(i == 0)
    def _():
        ri = lax.broadcasted_iota(jnp.int32, (rows, rows), 0)
        ci = lax.broadcasted_iota(jnp.int32, (rows, rows), 1)
        before[...] = jnp.where(ri < ci, 1.0, 0.0).astype(_bf16)
        ri = lax.broadcasted_iota(jnp.int32, (LANES, LANES), 0)
        ci = lax.broadcasted_iota(jnp.int32, (LANES, LANES), 1)
        cnt = jnp.sum(jnp.where(ri == ci, counts_ref[...], 0.0), axis=0, keepdims=True)
        cnt = jnp.broadcast_to(cnt, (SUBLANES, LANES))
        padded = jnp.floor((cnt + (MOE_ROWS - 1)) * (1.0 / MOE_ROWS)) * MOE_ROWS
        first = _lane_cumsum(padded) - padded
        first_col = jnp.sum(jnp.where(ri == ci, first[0:1, :], 0.0), axis=-1, keepdims=True)
        base[...] = jnp.broadcast_to(first_col, base.shape)

    earlier = _dot(oh_t.astype(_bf16), before[...])
    slot_row = jnp.sum(oh_t * (base[:, 0:1] + earlier), axis=0, keepdims=True)
    base[...] += jnp.sum(oh_t, axis=1, keepdims=True)
    dest_ref[0] = slot_row.astype(jnp.int32)

    @pl.when(i == n_steps - 1)
    def _():
        lax.fori_loop(0, N_CLASSES, functools.partial(zero_fill, wait=True), 0)


def _rank_call(route_buf, counts, zero_end, n_blocks):
    total = route_buf.shape[0]
    rows = next(r for r in RANK_ROWS if total % r == 0)
    n_tiles = total // rows
    route_buf = route_buf.reshape(n_tiles, 1, rows)
    grid_spec = pltpu.PrefetchScalarGridSpec(
        num_scalar_prefetch=1,
        grid=(n_tiles,),
        in_specs=[pl.BlockSpec((1, 1, rows), lambda i, *_: (i, 0, 0)),
                  pl.BlockSpec((LANES, LANES), lambda i, *_: (0, 0))],
        out_specs=[pl.BlockSpec((1, 1, rows), lambda i, *_: (i, 0, 0)),
                   pl.BlockSpec(memory_space=pl.ANY)],
        scratch_shapes=[pltpu.VMEM((LANES, LANES), _f32), pltpu.VMEM((rows, rows), _bf16),
                        pltpu.VMEM((ROW_TILE, LANES), _f32), pltpu.SemaphoreType.DMA((1,))],
    )
    return pl.pallas_call(
        functools.partial(_rank_kernel, n_steps=n_tiles),
        grid_spec=grid_spec,
        out_shape=[jax.ShapeDtypeStruct((n_tiles, 1, rows), jnp.int32),
                   jax.ShapeDtypeStruct((n_blocks * ROW_TILE, LANES), _f32)],
        compiler_params=pltpu.CompilerParams(dimension_semantics=("arbitrary",)),
        name="rank",
    )(zero_end, route_buf, counts)


def _block_tables(counts, n_blocks):
    cnt = counts[:N_CLASSES, 0].astype(jnp.int32)
    padded = (cnt + MOE_ROWS - 1) // MOE_ROWS * MOE_ROWS
    pend = jnp.cumsum(padded)
    nb = pend[-1] // MOE_ROWS
    blk = jnp.minimum(jnp.arange(n_blocks, dtype=jnp.int32), nb - 1)
    blk_cls = jnp.sum(pend[None, :] <= (blk * MOE_ROWS)[:, None], axis=1, dtype=jnp.int32)
    blk_cls = jnp.minimum(blk_cls, N_CLASSES - 1)
    grp = blk_cls // PAIRS_PER_GROUP
    pair = blk_cls % PAIRS_PER_GROUP
    firsts = jnp.arange(1, EXPERTS_PER_GROUP, dtype=jnp.int32)
    pair_start = firsts * (2 * EXPERTS_PER_GROUP - 1 - firsts) // 2
    la = jnp.sum(pair[:, None] >= pair_start[None, :], axis=1, dtype=jnp.int32)
    lb = pair - la * (2 * EXPERTS_PER_GROUP - 1 - la) // 2 + la + 1
    zero_end = jnp.where(cnt > 0, pend, 0).astype(jnp.int32)
    return {
        "ea": grp * EXPERTS_PER_GROUP + la, "eb": grp * EXPERTS_PER_GROUP + lb,
        "nb": nb.reshape(1).astype(jnp.int32), "zero_end": zero_end,
    }


N_DISPATCH_BUFS = 3
N_WEIGHT_SLABS = 256


def _dispatch_kernel(idx_ref, wg_ref, wu_ref, wd_ref, x1_hbm, xs_in_hbm, xs_hbm, wg_out, wu_out, wd_out,
                     tbuf, lsem, ssem, *, n_slabs):
    del xs_in_hbm
    i = pl.program_id(0)
    n = pl.num_programs(0)
    slot = i % N_DISPATCH_BUFS

    @pl.when(i < n_slabs)
    def _round_weights():
        wg_out[...] = wg_ref[...].astype(_bf16)
        wu_out[...] = wu_ref[...].astype(_bf16)
        wd_out[...] = wd_ref[...].astype(_bf16)

    def load_copy(tile, s):
        return pltpu.make_async_copy(x1_hbm.at[pl.ds(tile * ROW_TILE, ROW_TILE), :], tbuf.at[s], lsem.at[s])

    def wait_scatter(s):
        pltpu.make_async_copy(tbuf.at[s], xs_hbm.at[pl.ds(0, ROW_TILE), :], ssem.at[s]).wait()

    @pl.when(i == 0)
    def _():
        load_copy(0, 0).start()

    @pl.when(i >= N_DISPATCH_BUFS - 1)
    def _():
        wait_scatter((i + 1) % N_DISPATCH_BUFS)

    @pl.when(i + 1 < n)
    def _():
        load_copy(i + 1, (i + 1) % N_DISPATCH_BUFS).start()

    load_copy(i, slot).wait()

    def body(pair, carry):
        for priority in range(2):
            r = 2 * pair + priority
            pltpu.async_copy(
                tbuf.at[slot, pl.ds(r * SUBLANES, SUBLANES), :],
                xs_hbm.at[pl.ds(idx_ref[0, 0, r] * SUBLANES, SUBLANES), :],
                ssem.at[slot], priority=priority)
        return carry
    lax.fori_loop(0, MOE_ROWS // 2, body, 0, unroll=4)

    @pl.when(i == n - 1)
    def _():
        @pl.when(n > 1)
        def _():
            wait_scatter((i + N_DISPATCH_BUFS - 1) % N_DISPATCH_BUFS)
        wait_scatter(slot)


def _dispatch_call(x1_buf, dest, xs_buf, expert_weights):
    n_tiles = dest.shape[0]
    n_slabs = min(N_WEIGHT_SLABS, 1 << (n_tiles.bit_length() - 1))
    flat = [w.reshape(-1, w.shape[-1]) for w in expert_weights]
    slab_specs = []
    for w in flat:
        assert w.shape[0] % n_slabs == 0
        slab_specs.append(pl.BlockSpec((w.shape[0] // n_slabs, w.shape[1]),
                                       lambda i, *_: (jnp.minimum(i, n_slabs - 1), 0)))
    in_specs = ([pl.BlockSpec((1, 1, MOE_ROWS), lambda i: (i, 0, 0), memory_space=pltpu.SMEM)] + slab_specs
                + [pl.BlockSpec(memory_space=pl.ANY), pl.BlockSpec(memory_space=pl.ANY)])
    outs = pl.pallas_call(
        functools.partial(_dispatch_kernel, n_slabs=n_slabs),
        grid=(n_tiles,),
        in_specs=in_specs,
        out_specs=[pl.BlockSpec(memory_space=pl.ANY)] + slab_specs,
        out_shape=[jax.ShapeDtypeStruct(xs_buf.shape, _f32)]
        + [jax.ShapeDtypeStruct(w.shape, _bf16) for w in flat],
        scratch_shapes=[
            pltpu.VMEM((N_DISPATCH_BUFS, ROW_TILE, LANES), _f32),
            pltpu.SemaphoreType.DMA((N_DISPATCH_BUFS,)),
            pltpu.SemaphoreType.DMA((N_DISPATCH_BUFS,)),
        ],
        input_output_aliases={len(in_specs) - 1: 0},
        compiler_params=pltpu.CompilerParams(
            dimension_semantics=("arbitrary",), vmem_limit_bytes=VMEM_LIMIT_BYTES),
        name="dispatch",
    )(dest, *flat, x1_buf, xs_buf)
    return outs[0], [o.reshape(w.shape) for o, w in zip(outs[1:], expert_weights)]


MOE_PAIR = 2


def _moe_kernel(ea_ref, eb_ref, nb_ref, xs_ref, gffn_ref, wr_ref, br_ref, *rest):
    out_ref = rest[-1]
    step = pl.program_id(0)
    nb = nb_ref[0]
    lane = lax.broadcasted_iota(jnp.int32, (1, ROUTE_LANES), 1)

    def load_phase(k, st):
        x = _load_row_tiles(xs_ref, MOE_ROWS, k * MOE_ROWS)
        st["x"] = x
        st["h"] = (x * _rms_scale(x) * gffn_ref[...]).astype(_bf16)

    def weight_phase(k, st):
        ea = ea_ref[step * MOE_PAIR + k]
        eb = eb_ref[step * MOE_PAIR + k]
        grp = ea // EXPERTS_PER_GROUP
        logits = _dot(st["h"], wr_ref[...]) + br_ref[...]

        def pick(col):
            return jnp.sum(jnp.where(lane == col, logits, 0.0), axis=-1, keepdims=True)

        coarse = lane < N_GROUPS
        cmax = jnp.max(jnp.where(coarse, logits, -jnp.inf), axis=-1, keepdims=True)
        csum = jnp.sum(jnp.where(coarse, jnp.exp(logits - cmax), 0.0), axis=-1, keepdims=True)
        gp = jnp.exp(pick(grp) - cmax) / csum
        fa = pick(N_GROUPS + ea)
        fb = pick(N_GROUPS + eb)
        fmax = jnp.maximum(fa, fb)
        pa = jnp.exp(fa - fmax)
        pb = jnp.exp(fb - fmax)
        st["wa"] = gp * pa / (pa + pb)
        st["wb"] = gp * pb / (pa + pb)

    def expert_phase(k, st):
        wga, wua, wda, wgb, wub, wdb = rest[6 * k:6 * k + 6]
        h = st["h"]

        def expert(wg_ref, wu_ref, wd_ref):
            g = _dot(h, wg_ref[0])
            u = _dot(h, wu_ref[0])
            return _dot((jax.nn.silu(g) * u).astype(_bf16), wd_ref[0])

        y = st["wa"] * expert(wga, wua, wda) + st["wb"] * expert(wgb, wub, wdb)
        _store_row_tiles(out_ref, st["x"] + y, MOE_ROWS, k * MOE_ROWS)

    def run(n_live):
        states = [{} for _ in range(n_live)]
        for phase in (load_phase, weight_phase, expert_phase):
            for k in range(n_live):
                phase(k, states[k])

    first = step * MOE_PAIR
    for n_live in range(MOE_PAIR, 0, -1):
        cond = (first + n_live <= nb) if n_live == MOE_PAIR else (first + n_live == nb)
        pl.when(cond)(functools.partial(run, n_live))


def _moe_call(xs_buf, tables, wts, expert_weights, n_blocks):
    w_gate, w_up, w_down = expert_weights
    d_model = w_gate.shape[1]
    d_exp = w_gate.shape[2]
    assert n_blocks % MOE_PAIR == 0

    def used_step(s, ea, eb, nb):
        return (jnp.minimum(s, (nb[0] - 1) // MOE_PAIR), 0)

    def expert_of(which, k):
        def index_map(s, ea, eb, nb):
            blk = jnp.minimum(s * MOE_PAIR + k, nb[0] - 1)
            return ((ea, eb)[which][blk], 0, 0)
        return index_map

    weight_specs = []
    weight_args = []
    for k in range(MOE_PAIR):
        for which in range(2):
            weight_specs += [pl.BlockSpec((1, d_model, d_exp), expert_of(which, k)),
                             pl.BlockSpec((1, d_model, d_exp), expert_of(which, k)),
                             pl.BlockSpec((1, d_exp, d_model), expert_of(which, k))]
            weight_args += [w_gate, w_up, w_down]

    grid_spec = pltpu.PrefetchScalarGridSpec(
        num_scalar_prefetch=3,
        grid=(n_blocks // MOE_PAIR,),
        in_specs=[
            pl.BlockSpec((MOE_PAIR * ROW_TILE, LANES), used_step),
            pl.BlockSpec((1, d_model), lambda s, *_: (0, 0)),
            pl.BlockSpec((d_model, ROUTE_LANES), lambda s, *_: (0, 0)),
            pl.BlockSpec((1, ROUTE_LANES), lambda s, *_: (0, 0)),
        ] + weight_specs,
        out_specs=pl.BlockSpec((MOE_PAIR * ROW_TILE, LANES), used_step),
    )
    return pl.pallas_call(
        _moe_kernel,
        grid_spec=grid_spec,
        out_shape=jax.ShapeDtypeStruct(xs_buf.shape, _f32),
        compiler_params=pltpu.CompilerParams(
            dimension_semantics=("arbitrary",), vmem_limit_bytes=VMEM_LIMIT_BYTES),
        name="moe",
    )(tables["ea"], tables["eb"], tables["nb"], xs_buf, wts["g_ffn"], wts["w_route_hi"], wts["b_route"],
      *weight_args)


def _ple_kernel(idx_ref, idx_nxt_ref, p_ref, gple_ref, wgate_ref, wproj_ref, x2s_hbm, out_ref, buf0, buf1, gsem):
    i = pl.program_id(0)
    n = pl.num_programs(0)

    def row_copy(idx_ref, r, off, buf, s):
        return pltpu.make_async_copy(
            x2s_hbm.at[pl.ds(idx_ref[0, 0, off + r] * SUBLANES, SUBLANES), :],
            buf.at[pl.ds(r * SUBLANES, SUBLANES), :],
            gsem.at[s])

    def start_gather(idx_ref, off, buf, s):
        for r in range(MOE_ROWS):
            pltpu.async_copy(
                x2s_hbm.at[pl.ds(idx_ref[0, 0, off + r] * SUBLANES, SUBLANES), :],
                buf.at[pl.ds(r * SUBLANES, SUBLANES), :],
                gsem.at[s], priority=r % 2)

    def wait_gather(buf, s):
        pltpu.make_async_copy(x2s_hbm.at[pl.ds(0, ROW_TILE), :], buf, gsem.at[s]).wait()

    def tile(buf, half):
        rows = pl.ds(half * MOE_ROWS, MOE_ROWS)
        x2 = _load_row_tiles(buf, MOE_ROWS)
        hn = (x2 * _rms_scale(x2) * gple_ref[...]).astype(_bf16)
        gate = jax.nn.sigmoid(_dot(hn, wgate_ref[...]))
        out_ref[rows, :] = x2 + _dot(p_ref[rows, :].astype(_bf16), wproj_ref[...]) * gate

    @pl.when(i == 0)
    def _():
        def body(r, carry):
            row_copy(idx_ref, r, 0, buf0, 0).start()
            return carry
        lax.fori_loop(0, MOE_ROWS, body, 0, unroll=8)

    start_gather(idx_ref, MOE_ROWS, buf1, 1)
    wait_gather(buf0, 0)
    tile(buf0, 0)
    start_gather(idx_nxt_ref, 0, buf0, 0)
    wait_gather(buf1, 1)
    tile(buf1, 1)

    @pl.when(i == n - 1)
    def _():
        wait_gather(buf0, 0)


def _ple_call(x2s_buf, dest, p, wts, *, row_off, name):
    n_rows, d_ple = p.shape
    d_model = wts["w_ple_gate"].shape[0]
    step_rows = 2 * MOE_ROWS
    n_steps = n_rows // step_rows
    blk_off = row_off // step_rows
    last = blk_off + n_steps - 1
    dest = dest.reshape(-1, 1, step_rows)
    idx_spec = functools.partial(pl.BlockSpec, (1, 1, step_rows), memory_space=pltpu.SMEM)
    return pl.pallas_call(
        _ple_kernel,
        grid=(n_steps,),
        in_specs=[
            idx_spec(index_map=lambda i: (blk_off + i, 0, 0)),
            idx_spec(index_map=lambda i: (jnp.minimum(blk_off + i + 1, last), 0, 0)),
            pl.BlockSpec((step_rows, d_ple), lambda i: (i, 0)),
            _const_spec((1, d_model)),
            _const_spec((d_model, d_model)),
            _const_spec((d_ple, d_model)),
            pl.BlockSpec(memory_space=pl.ANY),
        ],
        out_specs=pl.BlockSpec((step_rows, d_model), lambda i: (i, 0)),
        out_shape=jax.ShapeDtypeStruct((n_rows, d_model), _f32),
        scratch_shapes=[pltpu.VMEM((ROW_TILE, LANES), _f32), pltpu.VMEM((ROW_TILE, LANES), _f32),
                        pltpu.SemaphoreType.DMA((2,))],
        compiler_params=pltpu.CompilerParams(
            dimension_semantics=("arbitrary",), vmem_limit_bytes=VMEM_LIMIT_BYTES),
        name=name,
    )(dest, dest, p, wts["g_ple"], wts["w_ple_gate"], wts["w_ple_proj"], x2s_buf)


def _rel_offsets(table):
    n_heads, n_rel = table.shape
    length = BAND_PREV + 2 * ATT_ROWS
    n_far = BAND_PREV - REL_CLIP + 1
    n_near = BAND_PREV + ATT_ROWS - n_far - (n_rel - 1)
    n_neg = length - (BAND_PREV + ATT_ROWS)
    tab = table.astype(_f32)
    last, first = tab[:, n_rel - 1:], tab[:, :1]
    g = jnp.concatenate([jnp.broadcast_to(last, (n_heads, n_far)), tab[:, n_rel - 2::-1],
                         jnp.broadcast_to(first, (n_heads, n_near)), jnp.broadcast_to(last, (n_heads, n_neg))],
                        axis=1)
    return g.reshape(n_heads, 1, length)


def _pair_block_diag(w):
    n2, c, _ = w.shape
    w = w.reshape(n2 // 2, 2, c, c)
    zero = jnp.zeros_like(w[:, 0])
    top = jnp.concatenate([w[:, 0], zero], axis=2)
    bottom = jnp.concatenate([zero, w[:, 1]], axis=2)
    return jnp.concatenate([top, bottom], axis=1)


def _layer_weights(l, g_mix, w_in, w_pool, pool_scale, g_q, g_k, rel_table, w_br_pool, w_br_att, w_out,
                   g_ffn, w_coarse, b_coarse, w_fine, b_fine, w_gate_e, w_up_e, w_down_e, g_ple, w_ple_gate,
                   w_ple_proj):
    d_model = w_in.shape[1]
    pad = ROUTE_LANES - N_GROUPS - N_EXPERTS
    head_of = jnp.arange(ATT_WIDTH // 2, dtype=jnp.int32) // HEAD_DIM
    head_sum = jnp.where(head_of[:, None] == head_of[None, :], 1.0 / HEAD_DIM, 0.0)
    w_route = jnp.concatenate([w_coarse[l], w_fine[l], jnp.zeros((d_model, pad), _f32)], axis=1)
    w_route_hi = w_route.astype(_bf16)
    w_route_lo = (w_route - w_route_hi.astype(_f32)).astype(_bf16)
    b_route = jnp.concatenate([b_coarse[l], b_fine[l], jnp.zeros((pad,), _f32)])
    return {
        "g_mix": g_mix[l].reshape(1, d_model),
        "w_in": w_in[l].astype(_bf16),
        "w_pool": _pair_block_diag(w_pool[l].astype(_bf16)),
        "pool_scale": pool_scale[l].reshape(1, POOL_WIDTH),
        "g_q": (jnp.tile(g_q[l], N_HEADS) * (HEAD_DIM ** -0.5 * LOG2E)).reshape(1, ATT_WIDTH),
        "g_k": jnp.tile(g_k[l], N_HEADS).reshape(1, ATT_WIDTH),
        "bd": head_sum.astype(_bf16),
        "rel": _rel_offsets(rel_table[l] * LOG2E),
        "w_br_pool": w_br_pool[l].astype(_bf16),
        "w_br_att": w_br_att[l].astype(_bf16),
        "w_out": w_out[l].astype(_bf16),
        "g_ffn": g_ffn[l].reshape(1, d_model),
        "w_route": jnp.concatenate([w_route_hi, w_route_lo], axis=1),
        "b_route": b_route.reshape(1, ROUTE_LANES),
        "w_route_hi": w_route_hi,
        "experts_f32": (w_gate_e[l], w_up_e[l], w_down_e[l]),
        "g_ple": g_ple[l].reshape(1, d_model),
        "w_ple_gate": w_ple_gate[l].astype(_bf16),
        "w_ple_proj": w_ple_proj[l].astype(_bf16),
    }


def _pick_tile(t_len, want):
    tq = min(want, t_len)
    assert t_len % tq == 0 and tq % CHUNK == 0 and BAND_PREV % tq == 0
    return tq


def kernel(x_prompt, x_sample, cache_k, cache_v, state_pool, p_prompt, p_sample, g_mix, w_in, w_pool, pool_scale, g_q, g_k, rel_table, w_br_pool, w_br_att, w_out, g_ffn, w_coarse, b_coarse, w_fine, b_fine, w_gate_e, w_up_e, w_down_e, g_ple, w_ple_gate, w_ple_proj):
    depth = w_in.shape[0]
    bp, tp, d_model = x_prompt.shape
    bs, ts, _ = x_sample.shape
    assert d_model == SUBLANES * LANES
    tq_p = _pick_tile(tp, 512)
    tq_s = _pick_tile(ts, 256)
    sub_p = 2 if tp % (2 * tq_p) == 0 and BAND_PREV % (2 * tq_p) == 0 else 1
    rows_p, rows_s = bp * tp, bs * ts
    total = rows_p + rows_s
    assert rows_p % (2 * MOE_ROWS) == 0 and rows_s % (2 * MOE_ROWS) == 0 and total % RANK_ROWS[-1] == 0
    n_blocks = -(-(total + N_CLASSES * (MOE_ROWS - 1)) // MOE_ROWS)
    n_blocks = -(-n_blocks // MOE_PAIR) * MOE_PAIR

    xp, xs = x_prompt, x_sample
    outs = [[] for _ in range(6)]
    for l in range(depth):
        wts = _layer_weights(l, g_mix, w_in, w_pool, pool_scale, g_q, g_k, rel_table, w_br_pool,
                             w_br_att, w_out, g_ffn, w_coarse, b_coarse, w_fine, b_fine, w_gate_e, w_up_e,
                             w_down_e, g_ple, w_ple_gate, w_ple_proj)
        zeros_kv = jnp.zeros((bp, BAND_PREV, ATT_WIDTH), _f32)
        zeros_pool = jnp.zeros((bp, POOL_HIST, POOL_WIDTH), _f32)
        x1_buf, route_p, kp, vp, pp, counts_p = _mixer_call(
            xp, zeros_kv, zeros_kv, zeros_pool, wts, None,
            total_rows=total, tq=tq_p, n_sub=sub_p, pos0=0, row_off=0, name="mixer_prompt")
        pool_hist = jnp.pad(state_pool[l], ((0, 0), (POOL_HIST - POOL_STATE, 0), (0, 0)))
        x1_buf, route_s, kn, vn, pn, counts_s = _mixer_call(
            xs, cache_k[l].reshape(bs, BAND_PREV, ATT_WIDTH), cache_v[l].reshape(bs, BAND_PREV, ATT_WIDTH),
            pool_hist, wts, x1_buf,
            total_rows=total, tq=tq_s, n_sub=1, pos0=PAST_LEN, row_off=rows_p, name="mixer_sample")

        route_buf = jnp.concatenate([route_p.reshape(-1), route_s.reshape(-1)])
        counts = counts_p + counts_s
        tables = _block_tables(counts, n_blocks)
        dest, xs_buf = _rank_call(route_buf, counts, tables["zero_end"], n_blocks)
        dest = dest.reshape(total // MOE_ROWS, 1, MOE_ROWS)
        xs_buf, experts_bf16 = _dispatch_call(x1_buf, dest, xs_buf, wts["experts_f32"])
        x2s_buf = _moe_call(xs_buf, tables, wts, experts_bf16, n_blocks)

        xp = _ple_call(x2s_buf, dest, p_prompt[l].reshape(rows_p, -1), wts, row_off=0,
                       name="ple_prompt").reshape(bp, tp, d_model)
        xs = _ple_call(x2s_buf, dest, p_sample[l].reshape(rows_s, -1), wts, row_off=rows_p,
                       name="ple_sample").reshape(bs, ts, d_model)

        tail_p = min(BAND_PREV, tp)
        if tail_p < BAND_PREV:
            kp = jnp.pad(kp, ((0, 0), (BAND_PREV - tail_p, 0), (0, 0)))
            vp = jnp.pad(vp, ((0, 0), (BAND_PREV - tail_p, 0), (0, 0)))
        outs[0].append(kp.reshape(bp, BAND_PREV, N_HEADS, HEAD_DIM))
        outs[1].append(vp.reshape(bp, BAND_PREV, N_HEADS, HEAD_DIM))
        outs[2].append(pp)
        outs[3].append(kn.reshape(bs, ts, N_HEADS, HEAD_DIM))
        outs[4].append(vn.reshape(bs, ts, N_HEADS, HEAD_DIM))
        outs[5].append(pn)
    return (xp, xs) + tuple(jnp.stack(o) for o in outs)
```
